```python
import jax, jax.numpy as jnp
from jax import lax
import numpy as np

D_MODEL = 2048
BATCH = 8
SEQ = 2048
DEPTH = 2

CTX_LEN = 256
GRID_W = 64
D_MIX = D_MODEL
HEAD_DIM = 128
ATTN_W = D_MIX // 2
N_HEADS = ATTN_W // HEAD_DIM
N_KV_HEADS = 2
GQA_GROUP = N_HEADS // N_KV_HEADS
KV_W = N_KV_HEADS * HEAD_DIM
POOL_W = D_MIX // 4
POOL_WINDOWS = (2, 4, 8, 16)
POOL_GROUPS = len(POOL_WINDOWS)
POOL_GC = POOL_W // POOL_GROUPS
FOURIER_W = D_MIX - ATTN_W - POOL_W
FOURIER_GROUPS = 4
FOURIER_GC = FOURIER_W // FOURIER_GROUPS
OFF_Q = 0
OFF_K = OFF_Q + ATTN_W
OFF_V = OFF_K + KV_W
OFF_POOL = OFF_V + KV_W
OFF_FOURIER = OFF_POOL + POOL_W
OFF_GATE = OFF_FOURIER + FOURIER_W
IN_W = OFF_GATE + D_MIX
Q_BLOCK = 128
ROPE_THETA = 10000.0
AXIS_ROT = HEAD_DIM // 2
EPS = 1e-6

kernel_name = "hybrid_gqa_pool_fourier_prefix_dit"


def rmsnorm(x, g):
    xf = x.astype(jnp.float32)
    y = xf * lax.rsqrt(jnp.mean(xf * xf, axis=-1, keepdims=True) + EPS)
    return (y * g.astype(jnp.float32)).astype(x.dtype)


def axial_rope_tables(n):
    rows_count = n // GRID_W
    row = jnp.broadcast_to(jnp.arange(rows_count, dtype=jnp.float32)[:, None], (rows_count, GRID_W)).reshape(-1)
    col = jnp.broadcast_to(jnp.arange(GRID_W, dtype=jnp.float32)[None, :], (rows_count, GRID_W)).reshape(-1)
    inv = ROPE_THETA ** (-jnp.arange(0, AXIS_ROT, 2, dtype=jnp.float32) / AXIS_ROT)
    ang = jnp.stack([row[:, None] * inv, col[:, None] * inv], axis=1)
    return jnp.cos(ang), jnp.sin(ang)


def apply_axial_rope(x, cos, sin):
    B, N, H, _ = x.shape
    xr = x.astype(jnp.float32).reshape(B, N, H, 2, 2, AXIS_ROT // 2)
    x1, x2 = xr[..., 0, :], xr[..., 1, :]
    c = cos[None, :, None]
    s = sin[None, :, None]
    out = jnp.stack([x1 * c - x2 * s, x2 * c + x1 * s], axis=-2)
    return out.reshape(B, N, H, HEAD_DIM).astype(x.dtype)


def attend(qg, keys, vals):
    s = jnp.einsum('bqkgd,bskd->bkgqs', qg, keys, preferred_element_type=jnp.float32) * (HEAD_DIM ** -0.5)
    p = jax.nn.softmax(s, axis=-1).astype(vals.dtype)
    return jnp.einsum('bkgqs,bskd->bqkgd', p, vals)


def latent_attention(q, k, v, kc, vc):
    B, N = q.shape[:2]
    keys = jnp.concatenate([kc, k], axis=1)
    vals = jnp.concatenate([vc, v], axis=1)
    nb = N // Q_BLOCK
    qb = q.reshape(B, nb, Q_BLOCK, N_KV_HEADS, GQA_GROUP, HEAD_DIM).transpose(1, 0, 2, 3, 4, 5)
    o = lax.map(lambda qi: attend(qi, keys, vals), qb)
    return o.transpose(1, 0, 2, 3, 4, 5).reshape(B, N, ATTN_W)


def context_attention(qc, kc, vc):
    B, C = qc.shape[:2]
    qg = qc.reshape(B, C, N_KV_HEADS, GQA_GROUP, HEAD_DIM)
    return attend(qg, kc, vc).reshape(B, C, ATTN_W)


def multiscale_pool(u, pool_w, pool_scale):
    B, N, _ = u.shape
    uf = u.astype(jnp.float32).reshape(B, N, POOL_GROUPS, POOL_GC)
    cs = jnp.concatenate([jnp.zeros((B, 1, POOL_GROUPS, POOL_GC), jnp.float32), jnp.cumsum(uf, axis=1)], axis=1)
    t = jnp.arange(N, dtype=jnp.int32)
    outs = []
    for gi, win in enumerate(POOL_WINDOWS):
        lo = jnp.clip(t - win // 2, 0, N - 1)
        hi = jnp.clip(t + (win - win // 2) - 1, 0, N - 1)
        cnt = (hi - lo + 1).astype(jnp.float32)
        csg = cs[:, :, gi]
        win_sum = jnp.take(csg, hi + 1, axis=1) - jnp.take(csg, lo, axis=1)
        outs.append(win_sum / cnt[None, :, None] - uf[:, :, gi])
    pooled = jnp.stack(outs, axis=2).astype(u.dtype)
    y = jnp.einsum('bngc,gcd->bngd', pooled, pool_w).reshape(B, N, POOL_W)
    return y * pool_scale


def fourier_mix(u, fourier_w):
    B, N, _ = u.shape
    uf = u.astype(jnp.float32).reshape(B, N, FOURIER_GROUPS, FOURIER_GC)
    f = jnp.fft.fft2(uf, axes=(1, 3), norm='ortho').real.astype(u.dtype)
    return jnp.einsum('bngc,gcd->bngd', f, fourier_w).reshape(B, N, FOURIER_W)


def split_proj(p):
    return (p[..., OFF_Q:OFF_K], p[..., OFF_K:OFF_V], p[..., OFF_V:OFF_POOL],
            p[..., OFF_POOL:OFF_FOURIER], p[..., OFF_FOURIER:OFF_GATE], p[..., OFF_GATE:])


def merge_branches(att, u_pool, u_four, g, pool_w, pool_scale, fourier_w, w_out):
    mixed = jnp.concatenate([att, multiscale_pool(u_pool, pool_w, pool_scale), fourier_mix(u_four, fourier_w)], axis=-1)
    return (mixed * jax.nn.silu(g)) @ w_out


def setup_inputs(seed: int = 0) -> dict:
    key = jax.random.key(seed)
    ks = jax.random.split(key, 20)
    f32 = jnp.float32
    nrm = lambda k, shape: jax.random.normal(k, shape, f32)
    return {
        "x": nrm(ks[0], (BATCH, SEQ, D_MODEL)),
        "c": nrm(ks[1], (BATCH, D_MODEL)),
        "ctx": nrm(ks[2], (BATCH, CTX_LEN, D_MODEL)),
        "c_ctx": nrm(ks[3], (D_MODEL,)),
        "ada_w": nrm(ks[4], (DEPTH, D_MODEL, 3 * D_MODEL)) * (0.5 * D_MODEL ** -0.5),
        "ada_b": nrm(ks[5], (DEPTH, 3 * D_MODEL)) * 0.01,
        "norm_g": 1.0 + 0.02 * nrm(ks[6], (DEPTH, D_MODEL)),
        "w_in": nrm(ks[7], (DEPTH, D_MODEL, IN_W)) * (D_MODEL ** -0.5),
        "q_norm_g": 1.0 + 0.02 * nrm(ks[8], (DEPTH, HEAD_DIM)),
        "k_norm_g": 1.0 + 0.02 * nrm(ks[9], (DEPTH, HEAD_DIM)),
        "pool_w": nrm(ks[10], (DEPTH, POOL_GROUPS, POOL_GC, POOL_GC)) * (POOL_GC ** -0.5),
        "pool_scale": 1.0 + 0.02 * nrm(ks[11], (DEPTH, POOL_W)),
        "fourier_w": nrm(ks[12], (DEPTH, FOURIER_GROUPS, FOURIER_GC, FOURIER_GC)) * (FOURIER_GC ** -0.5),
        "w_out": nrm(ks[13], (DEPTH, D_MIX, D_MODEL)) * (D_MIX ** -0.5),
        "final_norm_g": 1.0 + 0.02 * nrm(ks[14], (D_MODEL,)),
    }


def reference(x, c, ctx, c_ctx, ada_w, ada_b, norm_g, w_in, q_norm_g, k_norm_g,
              pool_w, pool_scale, fourier_w, w_out, final_norm_g):
    B, N, _ = x.shape
    C = ctx.shape[1]
    cos, sin = axial_rope_tables(N)
    xc = ctx
    for l in range(DEPTH):
        last = l == DEPTH - 1
        shift, scale, gate = jnp.split(jax.nn.silu(c) @ ada_w[l] + ada_b[l], 3, axis=-1)
        shift_c, scale_c, gate_c = jnp.split(jax.nn.silu(c_ctx) @ ada_w[l] + ada_b[l], 3, axis=-1)
        h = rmsnorm(x, norm_g[l]) * (1.0 + scale[:, None]) + shift[:, None]
        hc = rmsnorm(xc, norm_g[l]) * (1.0 + scale_c) + shift_c

        if last:
            pkv = hc @ w_in[l][:, OFF_K:OFF_POOL]
            kc, vc = pkv[..., :KV_W], pkv[..., KV_W:]
        else:
            qc, kc, vc, upc, ufc, gc = split_proj(hc @ w_in[l])
        kc = rmsnorm(kc.reshape(B, C, N_KV_HEADS, HEAD_DIM), k_norm_g[l])
        vc = vc.reshape(B, C, N_KV_HEADS, HEAD_DIM)

        q, k, v, up, uf, g = split_proj(h @ w_in[l])
        q = apply_axial_rope(rmsnorm(q.reshape(B, N, N_HEADS, HEAD_DIM), q_norm_g[l]), cos, sin)
        k = apply_axial_rope(rmsnorm(k.reshape(B, N, N_KV_HEADS, HEAD_DIM), k_norm_g[l]), cos, sin)
        v = v.reshape(B, N, N_KV_HEADS, HEAD_DIM)
        att = latent_attention(q, k, v, kc, vc)
        out = merge_branches(att, up, uf, g, pool_w[l], pool_scale[l], fourier_w[l], w_out[l])
        x_new = x + gate[:, None] * out

        if not last:
            qc = rmsnorm(qc.reshape(B, C, N_HEADS, HEAD_DIM), q_norm_g[l])
            attc = context_attention(qc, kc, vc)
            outc = merge_branches(attc, upc, ufc, gc, pool_w[l], pool_scale[l], fourier_w[l], w_out[l])
            xc = xc + gate_c * outc
        x = x_new
    return rmsnorm(x, final_norm_g)
```

```python
import functools
import math

import jax
import jax.numpy as jnp
from jax import lax
from jax.experimental import pallas as pl
from jax.experimental.pallas import tpu as pltpu

D_MODEL = 2048
DEPTH = 2
GRID_W = 64
HEAD_DIM = 128
ATTN_W = 1024
N_HEADS = 8
N_KV_HEADS = 2
GQA_GROUP = 4
KV_W = 256
POOL_W = 512
POOL_WINDOWS = (2, 4, 8, 16)
FOURIER_W = 512
MIX_GROUPS = 4
GROUP_C = 128
OFF_K = ATTN_W
OFF_V = OFF_K + KV_W
OFF_POOL = OFF_V + KV_W
OFF_FOURIER = OFF_POOL + POOL_W
OFF_GATE = OFF_FOURIER + FOURIER_W
IN_W = OFF_GATE + D_MODEL
ROPE_THETA = 10000.0
AXIS_ROT = HEAD_DIM // 2
EPS = 1e-6

MOD_ROWS = 16
CTX_MOD_ROW = 8
POOL_HALO = 8
V7X_VMEM_BYTES = 64 * 1024 * 1024
VMEM_LIMIT = V7X_VMEM_BYTES - 8 * 1024 * 1024

BF16 = jnp.bfloat16
F32 = jnp.float32


def _silu(x):
    return x * jax.nn.sigmoid(x)


def _dot(a, b):
    return jnp.dot(a, b, preferred_element_type=F32)


def _dot_nt(a, b):
    return lax.dot_general(a, b, (((1,), (1,)), ((), ())), preferred_element_type=F32)


def _resident(shape):
    zeros = (0,) * len(shape)
    return pl.BlockSpec(shape, lambda *_: zeros, pipeline_mode=pl.Buffered(1))


def _params(n_grid):
    return pltpu.CompilerParams(dimension_semantics=("arbitrary",) * n_grid,
                                vmem_limit_bytes=VMEM_LIMIT)


def _adaln_kernel(c_ref, w_ref, b_ref, o_ref):
    a = _silu(c_ref[...]).astype(BF16)
    o_ref[...] = _dot(a, w_ref[...].astype(BF16)) + b_ref[...]


def _adaln(cond, ada_w, ada_b):
    tn = 1024
    return pl.pallas_call(
        _adaln_kernel,
        grid=(DEPTH, 3 * D_MODEL // tn),
        in_specs=[
            pl.BlockSpec((MOD_ROWS, D_MODEL), lambda l, j: (0, 0)),
            pl.BlockSpec((None, D_MODEL, tn), lambda l, j: (l, 0, j)),
            pl.BlockSpec((None, 1, tn), lambda l, j: (l, 0, j)),
        ],
        out_specs=pl.BlockSpec((None, MOD_ROWS, tn), lambda l, j: (l, 0, j)),
        out_shape=jax.ShapeDtypeStruct((DEPTH, MOD_ROWS, 3 * D_MODEL), F32),
        compiler_params=_params(2),
        name="adaln",
    )(cond, ada_w, ada_b.reshape(DEPTH, 1, 3 * D_MODEL))


def _modulated_norm(x_ref, mod_ref, ng_ref, row):
    x = x_ref[...]
    ms = jnp.mean(x * x, axis=-1, keepdims=True)
    shift = mod_ref[pl.ds(row, 1), 0:D_MODEL]
    scale = mod_ref[pl.ds(row, 1), D_MODEL:2 * D_MODEL]
    y = (x * lax.rsqrt(ms + EPS)) * ng_ref[...]
    return (y * (1.0 + scale) + shift).astype(BF16)


def _head_norm(p, gain):
    ms = jnp.mean(p * p, axis=-1, keepdims=True)
    return (p * lax.rsqrt(ms + EPS)) * gain


def _rope(y, cos, sin, swap_lo):
    partner = jnp.where(swap_lo, pltpu.roll(y, 96, 1), pltpu.roll(y, 32, 1))
    return y * cos + partner * sin


def _inproj_kernel(*refs, rope, tiles_per_batch):
    if rope:
        (x_ref, mod_ref, ng_ref, w_ref, qg_ref, kg_ref, cos_ref, sin_ref,
         q_ref, k_ref, v_ref, up_ref, uf_ref, g_ref, h_ref) = refs
    else:
        (x_ref, mod_ref, ng_ref, w_ref, qg_ref, kg_ref,
         q_ref, k_ref, v_ref, up_ref, uf_ref, g_ref, h_ref) = refs
    tm = x_ref.shape[0]
    row = CTX_MOD_ROW if tiles_per_batch is None else pl.program_id(0) // tiles_per_batch
    h_ref[...] = _modulated_norm(x_ref, mod_ref, ng_ref, row)

    if rope:
        cos = cos_ref[...]
        sin = sin_ref[...]
        lane = lax.broadcasted_iota(jnp.int32, (tm, HEAD_DIM), 1)
        swap_lo = (lane & (AXIS_ROT // 2)) == 0

    def finish_head(p, gain):
        y = _head_norm(p, gain)
        if rope:
            y = _rope(y, cos, sin, swap_lo)
        return y.astype(BF16)

    q_gain = qg_ref[...] * (HEAD_DIM ** -0.5)
    chunk = GQA_GROUP * HEAD_DIM
    for c0 in range(0, ATTN_W, chunk):
        p = _dot(h_ref[...], w_ref[:, c0:c0 + chunk])
        for hh in range(GQA_GROUP):
            lo = hh * HEAD_DIM
            q_ref[:, c0 + lo:c0 + lo + HEAD_DIM] = finish_head(p[:, lo:lo + HEAD_DIM], q_gain)

    p = _dot(h_ref[...], w_ref[:, OFF_K:OFF_POOL])
    k_gain = kg_ref[...]
    for hh in range(N_KV_HEADS):
        lo = hh * HEAD_DIM
        k_ref[:, lo:lo + HEAD_DIM] = finish_head(p[:, lo:lo + HEAD_DIM], k_gain)
    v_ref[...] = p[:, KV_W:2 * KV_W].astype(BF16)

    up_ref[...] = _dot(h_ref[...], w_ref[:, OFF_POOL:OFF_FOURIER])
    uf_ref[...] = _dot(h_ref[...], w_ref[:, OFF_FOURIER:OFF_GATE]).astype(BF16)
    for c0 in range(0, D_MODEL, chunk):
        g_ref[:, c0:c0 + chunk] = _dot(h_ref[...], w_ref[:, OFF_GATE + c0:OFF_GATE + c0 + chunk])


def _inproj(x2d, mod, norm_g, w_in, q_g, k_g, rope_tabs, *, tm, rows_per_batch):
    rows = x2d.shape[0]
    rope = rope_tabs is not None
    tiles_per_batch = None if rows_per_batch is None else rows_per_batch // tm
    row_spec = lambda w: pl.BlockSpec((tm, w), lambda i: (i, 0))
    in_specs = [row_spec(D_MODEL), _resident((MOD_ROWS, 3 * D_MODEL)), _resident((1, D_MODEL)),
                _resident((D_MODEL, IN_W)), _resident((1, HEAD_DIM)), _resident((1, HEAD_DIM))]
    args = [x2d, mod, norm_g, w_in, q_g, k_g]
    if rope:
        tabs_per_seq = rows_per_batch // tm
        tab_spec = pl.BlockSpec((tm, HEAD_DIM), lambda i: (i % tabs_per_seq, 0))
        in_specs += [tab_spec, tab_spec]
        args += list(rope_tabs)
    widths = (ATTN_W, KV_W, KV_W, POOL_W, FOURIER_W, D_MODEL)
    dtypes = (BF16, BF16, BF16, F32, BF16, F32)
    return pl.pallas_call(
        functools.partial(_inproj_kernel, rope=rope, tiles_per_batch=tiles_per_batch),
        grid=(rows // tm,),
        in_specs=in_specs,
        out_specs=[row_spec(w) for w in widths],
        out_shape=[jax.ShapeDtypeStruct((rows, w), dt) for w, dt in zip(widths, dtypes)],
        scratch_shapes=[pltpu.VMEM((tm, D_MODEL), BF16)],
        compiler_params=_params(1),
        name="inproj_rope" if rope else "inproj_ctx",
    )(*args)


def _ctx_kv_kernel(x_ref, mod_ref, ng_ref, w_ref, kg_ref, k_ref, v_ref):
    h = _modulated_norm(x_ref, mod_ref, ng_ref, CTX_MOD_ROW)
    p = _dot(h, w_ref[...])
    k_gain = kg_ref[...]
    for hh in range(N_KV_HEADS):
        lo = hh * HEAD_DIM
        k_ref[:, lo:lo + HEAD_DIM] = _head_norm(p[:, lo:lo + HEAD_DIM], k_gain).astype(BF16)
    v_ref[...] = p[:, KV_W:2 * KV_W].astype(BF16)


def _ctx_kv(x2d, mod, norm_g, w_kv, k_g, *, tm):
    rows = x2d.shape[0]
    row_spec = lambda w: pl.BlockSpec((tm, w), lambda i: (i, 0))
    return pl.pallas_call(
        _ctx_kv_kernel,
        grid=(rows // tm,),
        in_specs=[row_spec(D_MODEL), _resident((MOD_ROWS, 3 * D_MODEL)), _resident((1, D_MODEL)),
                  _resident((D_MODEL, 2 * KV_W)), _resident((1, HEAD_DIM))],
        out_specs=[row_spec(KV_W), row_spec(KV_W)],
        out_shape=[jax.ShapeDtypeStruct((rows, KV_W), BF16)] * 2,
        compiler_params=_params(1),
        name="ctx_kv",
    )(x2d, mod, norm_g, w_kv, k_g)


def _attn_kernel(*refs, n_sources):
    q_ref, g_ref = refs[0], refs[1]
    kv_refs = refs[2:2 + 2 * n_sources]
    o_ref = refs[2 + 2 * n_sources]
    for hh in range(GQA_GROUP):
        cols = slice(hh * HEAD_DIM, (hh + 1) * HEAD_DIM)
        qh = q_ref[:, cols]
        scores = [_dot_nt(qh, kv_refs[2 * s][...]) for s in range(n_sources)]
        m = functools.reduce(jnp.maximum, [jnp.max(s, axis=-1, keepdims=True) for s in scores])
        probs = [jnp.exp(s - m) for s in scores]
        denom = functools.reduce(jnp.add, [jnp.sum(p, axis=-1, keepdims=True) for p in probs])
        acc = functools.reduce(
            jnp.add, [_dot(p.astype(BF16), kv_refs[2 * s + 1][...]) for s, p in enumerate(probs)])
        o_ref[:, cols] = ((acc / denom) * _silu(g_ref[:, cols])).astype(BF16)


def _attention(q, g, sources, *, batch, q_len, tq):
    tiles = q_len // tq
    q_spec = pl.BlockSpec((tq, GQA_GROUP * HEAD_DIM), lambda b, h, i: (b * tiles + i, h))
    in_specs = [q_spec, q_spec]
    args = [q, g]
    for k, v, length in sources:
        kv_spec = pl.BlockSpec((length, HEAD_DIM), lambda b, h, i: (b, h))
        in_specs += [kv_spec, kv_spec]
        args += [k, v]
    return pl.pallas_call(
        functools.partial(_attn_kernel, n_sources=len(sources)),
        grid=(batch, N_KV_HEADS, tiles),
        in_specs=in_specs,
        out_specs=q_spec,
        out_shape=jax.ShapeDtypeStruct((batch * q_len, ATTN_W), BF16),
        compiler_params=_params(3),
        name="attention",
    )(*args)


def _pool_kernel(up_ref, g_ref, pw_ref, ps_ref, o_ref, pad_ref, *, n, chunk):
    zeros = jnp.zeros((POOL_HALO, POOL_W), F32)
    pad_ref[0:POOL_HALO, :] = zeros
    pad_ref[POOL_HALO + n:2 * POOL_HALO + n, :] = zeros
    pad_ref[POOL_HALO:POOL_HALO + n, :] = up_ref[...]
    for gi, win in enumerate(POOL_WINDOWS):
        cols = slice(gi * GROUP_C, (gi + 1) * GROUP_C)
        w = pw_ref[gi].astype(BF16)
        before, after = win // 2, win - win // 2 - 1
        for r0 in range(0, n, chunk):
            base = POOL_HALO + r0
            acc = pad_ref[base - before:base - before + chunk, cols]
            for d in range(-before + 1, after + 1):
                acc = acc + pad_ref[base + d:base + d + chunk, cols]
            t = r0 + lax.broadcasted_iota(jnp.int32, (chunk, GROUP_C), 0)
            cnt = jnp.minimum(t + after, n - 1) - jnp.maximum(t - before, 0) + 1
            pooled = acc / cnt.astype(F32) - pad_ref[base:base + chunk, cols]
            y = _dot(pooled.astype(BF16), w) * ps_ref[:, cols]
            o_ref[r0:r0 + chunk, cols] = (y * _silu(g_ref[r0:r0 + chunk, cols])).astype(BF16)


def _pool(up, g, pool_w, pool_scale, *, batch, n):
    seq_spec = lambda col: pl.BlockSpec((n, POOL_W), lambda b: (b, col))
    return pl.pallas_call(
        functools.partial(_pool_kernel, n=n, chunk=min(n, 256)),
        grid=(batch,),
        in_specs=[seq_spec(0), seq_spec(ATTN_W // POOL_W),
                  _resident((MIX_GROUPS, GROUP_C, GROUP_C)), _resident((1, POOL_W))],
        out_specs=seq_spec(0),
        out_shape=jax.ShapeDtypeStruct((batch * n, POOL_W), BF16),
        scratch_shapes=[pltpu.VMEM((n + 2 * POOL_HALO, POOL_W), F32)],
        compiler_params=_params(1),
        name="pool",
    )(up, g, pool_w, pool_scale)


def _fourier_kernel(u_ref, g_ref, fw_ref, cc_ref, sc_ref, cn_ref, sn_ref, o_ref, ua_ref, ub_ref,
                    *, n, chunk):
    for gi in range(MIX_GROUPS):
        cols = slice(gi * GROUP_C, (gi + 1) * GROUP_C)
        w = fw_ref[gi].astype(BF16)
        a = _dot(cc_ref[...], w).astype(BF16)
        b = _dot(sc_ref[...], w).astype(BF16)
        ug = u_ref[:, cols]
        ua_ref[:, cols] = _dot(ug, a).astype(BF16)
        ub_ref[:, cols] = _dot(ug, b).astype(BF16)
    for r0 in range(0, n, chunk):
        rows = slice(r0, r0 + chunk)
        y = _dot(cn_ref[rows, :], ua_ref[...]) - _dot(sn_ref[rows, :], ub_ref[...])
        o_ref[rows, :] = (y * _silu(g_ref[rows, :])).astype(BF16)


def _fourier(uf, g, fourier_w, tabs, *, batch, n):
    cc, sc, cn, sn = tabs
    seq_spec = lambda col: pl.BlockSpec((n, FOURIER_W), lambda b: (b, col))
    return pl.pallas_call(
        functools.partial(_fourier_kernel, n=n, chunk=min(n, 512)),
        grid=(batch,),
        in_specs=[seq_spec(0), seq_spec((ATTN_W + POOL_W) // FOURIER_W),
                  _resident((MIX_GROUPS, GROUP_C, GROUP_C)),
                  _resident((GROUP_C, GROUP_C)), _resident((GROUP_C, GROUP_C)),
                  _resident((n, n)), _resident((n, n))],
        out_specs=seq_spec(0),
        out_shape=jax.ShapeDtypeStruct((batch * n, FOURIER_W), BF16),
        scratch_shapes=[pltpu.VMEM((n, FOURIER_W), BF16), pltpu.VMEM((n, FOURIER_W), BF16)],
        compiler_params=_params(1),
        name="fourier",
    )(uf, g, fourier_w, cc, sc, cn, sn)


def _outproj_kernel(*refs, tiles_per_batch, final):
    if final:
        att_ref, pool_ref, four_ref, w_ref, x_ref, mod_ref, fg_ref, o_ref = refs
    else:
        att_ref, pool_ref, four_ref, w_ref, x_ref, mod_ref, o_ref = refs
    row = CTX_MOD_ROW if tiles_per_batch is None else pl.program_id(0) // tiles_per_batch
    gate = mod_ref[pl.ds(row, 1), 2 * D_MODEL:3 * D_MODEL]
    out = (_dot(att_ref[...], w_ref[0:ATTN_W, :])
           + _dot(pool_ref[...], w_ref[ATTN_W:ATTN_W + POOL_W, :])
           + _dot(four_ref[...], w_ref[ATTN_W + POOL_W:D_MODEL, :]))
    y = x_ref[...] + gate * out
    if final:
        ms = jnp.mean(y * y, axis=-1, keepdims=True)
        y = (y * lax.rsqrt(ms + EPS)) * fg_ref[...]
    o_ref[...] = y


def _outproj(att, pool, four, w_out, x2d, mod, final_g, *, tm, rows_per_batch):
    rows = x2d.shape[0]
    final = final_g is not None
    tiles_per_batch = None if rows_per_batch is None else rows_per_batch // tm
    row_spec = lambda w: pl.BlockSpec((tm, w), lambda i: (i, 0))
    in_specs = [row_spec(ATTN_W), row_spec(POOL_W), row_spec(FOURIER_W),
                _resident((D_MODEL, D_MODEL)), row_spec(D_MODEL), _resident((MOD_ROWS, 3 * D_MODEL))]
    args = [att, pool, four, w_out, x2d, mod]
    if final:
        in_specs.append(_resident((1, D_MODEL)))
        args.append(final_g)
    return pl.pallas_call(
        functools.partial(_outproj_kernel, tiles_per_batch=tiles_per_batch, final=final),
        grid=(rows // tm,),
        in_specs=in_specs,
        out_specs=row_spec(D_MODEL),
        out_shape=jax.ShapeDtypeStruct((rows, D_MODEL), F32),
        compiler_params=_params(1),
        name="outproj_final" if final else "outproj",
    )(*args)


def _rope_tables(n):
    pos = jnp.arange(n, dtype=jnp.int32)
    row = (pos // GRID_W).astype(F32)
    col = (pos % GRID_W).astype(F32)
    inv = ROPE_THETA ** (-jnp.arange(0, AXIS_ROT, 2, dtype=F32) / AXIS_ROT)
    ang_r = row[:, None] * inv
    ang_c = col[:, None] * inv
    cos = jnp.concatenate([jnp.cos(ang_r), jnp.cos(ang_r), jnp.cos(ang_c), jnp.cos(ang_c)], axis=1)
    sin = jnp.concatenate([-jnp.sin(ang_r), jnp.sin(ang_r), -jnp.sin(ang_c), jnp.sin(ang_c)], axis=1)
    return cos, sin


def _dft_cos_sin(n):
    k = jnp.arange(n, dtype=jnp.int32)
    ang = ((k[:, None] * k[None, :]) % n).astype(F32) * (2.0 * math.pi / n)
    return jnp.cos(ang), jnp.sin(ang)


def _fourier_tables(n):
    cc, sc = _dft_cos_sin(GROUP_C)
    cn, sn = _dft_cos_sin(n)
    norm = 1.0 / math.sqrt(n * GROUP_C)
    return (cc * norm).astype(BF16), (sc * norm).astype(BF16), cn.astype(BF16), sn.astype(BF16)


def kernel(x, c, ctx, c_ctx, ada_w, ada_b, norm_g, w_in, q_norm_g, k_norm_g, pool_w, pool_scale,
           fourier_w, w_out, final_norm_g):
    B, N, _ = x.shape
    C = ctx.shape[1]
    tm = 512
    tm_ctx = 256

    cond = jnp.concatenate([c, c_ctx[None, :], jnp.zeros((MOD_ROWS - B - 1, D_MODEL), F32)], axis=0)
    mod = _adaln(cond, ada_w, ada_b)

    rope_tabs = _rope_tables(N)
    four_tabs = _fourier_tables(N)
    four_tabs_ctx = _fourier_tables(C)
    w_in_b = w_in.astype(BF16)
    w_out_b = w_out.astype(BF16)

    xl = x.reshape(B * N, D_MODEL)
    xc = ctx.reshape(B * C, D_MODEL)
    for l in range(DEPTH):
        last = l == DEPTH - 1
        ng = norm_g[l][None, :]
        qg = q_norm_g[l][None, :]
        kg = k_norm_g[l][None, :]
        ps = pool_scale[l][None, :]

        if last:
            kc, vc = _ctx_kv(xc, mod[l], ng, w_in_b[l][:, OFF_K:OFF_POOL], kg, tm=tm_ctx)
        else:
            qc, kc, vc, upc, ufc, gc = _inproj(xc, mod[l], ng, w_in_b[l], qg, kg, None,
                                               tm=tm_ctx, rows_per_batch=None)
        q, k, v, up, uf, g = _inproj(xl, mod[l], ng, w_in_b[l], qg, kg, rope_tabs,
                                     tm=tm, rows_per_batch=N)

        att = _attention(q, g, [(kc, vc, C), (k, v, N)], batch=B, q_len=N, tq=256)
        pooled = _pool(up, g, pool_w[l], ps, batch=B, n=N)
        four = _fourier(uf, g, fourier_w[l], four_tabs, batch=B, n=N)
        xl_new = _outproj(att, pooled, four, w_out_b[l], xl, mod[l],
                          final_norm_g[None, :] if last else None, tm=tm, rows_per_batch=N)

        if not last:
            attc = _attention(qc, gc, [(kc, vc, C)], batch=B, q_len=C, tq=C)
            pooledc = _pool(upc, gc, pool_w[l], ps, batch=B, n=C)
            fourc = _fourier(ufc, gc, fourier_w[l], four_tabs_ctx, batch=B, n=C)
            xc = _outproj(attc, pooledc, fourc, w_out_b[l], xc, mod[l], None,
                          tm=tm_ctx, rows_per_batch=None)
        xl = xl_new
    return xl.reshape(B, N, D_MODEL)
```

```python
import functools
import math

import jax
import jax.numpy as jnp
from jax import lax
from jax.experimental import pallas as pl
from jax.experimental.pallas import tpu as pltpu

D_MODEL = 2048
DEPTH = 2
GRID_W = 64
HEAD_DIM = 128
ATTN_W = 1024
N_HEADS = 8
N_KV_HEADS = 2
GQA_GROUP = 4
KV_W = 256
POOL_W = 512
POOL_WINDOWS = (2, 4, 8, 16)
FOURIER_W = 512
MIX_GROUPS = 4
GROUP_C = 128
OFF_K = ATTN_W
OFF_V = OFF_K + KV_W
OFF_POOL = OFF_V + KV_W
OFF_FOURIER = OFF_POOL + POOL_W
OFF_GATE = OFF_FOURIER + FOURIER_W
IN_W = OFF_GATE + D_MODEL
ROPE_THETA = 10000.0
AXIS_ROT = HEAD_DIM // 2
EPS = 1e-6

MOD_ROWS = 16
CTX_MOD_ROW = 8
POOL_HALO = 8
KEY_CHUNK = 256
EXP_CHUNK = 128
V7X_VMEM_BYTES = 64 * 1024 * 1024
VMEM_LIMIT = V7X_VMEM_BYTES - 8 * 1024 * 1024

BF16 = jnp.bfloat16
F32 = jnp.float32


def _silu(x):
    return x * jax.nn.sigmoid(x)


def _dot(a, b):
    return jnp.dot(a, b, preferred_element_type=F32)


def _dot_nt(a, b):
    return lax.dot_general(a, b, (((1,), (1,)), ((), ())), preferred_element_type=F32)


def _resident(shape):
    zeros = (0,) * len(shape)
    return pl.BlockSpec(shape, lambda *_: zeros, pipeline_mode=pl.Buffered(1))


def _layer(shape, l, tail=None):
    index = (l,) + (tail or (0,) * len(shape))
    return pl.BlockSpec((None,) + tuple(shape), lambda *_: index, pipeline_mode=pl.Buffered(1))


def _params(n_grid):
    return pltpu.CompilerParams(dimension_semantics=("arbitrary",) * n_grid,
                                vmem_limit_bytes=VMEM_LIMIT)


def _adaln_kernel(c_ref, w_ref, b_ref, o_ref):
    a = _silu(c_ref[...]).astype(BF16)
    o_ref[...] = _dot(a, w_ref[...].astype(BF16)) + b_ref[...]


def _adaln(cond, ada_w, ada_b):
    tn = 1024
    return pl.pallas_call(
        _adaln_kernel,
        grid=(DEPTH, 3 * D_MODEL // tn),
        in_specs=[
            pl.BlockSpec((MOD_ROWS, D_MODEL), lambda l, j: (0, 0)),
            pl.BlockSpec((None, D_MODEL, tn), lambda l, j: (l, 0, j)),
            pl.BlockSpec((None, 1, tn), lambda l, j: (l, 0, j)),
        ],
        out_specs=pl.BlockSpec((None, MOD_ROWS, tn), lambda l, j: (l, 0, j)),
        out_shape=jax.ShapeDtypeStruct((DEPTH, MOD_ROWS, 3 * D_MODEL), F32),
        compiler_params=_params(2),
        name="adaln",
    )(cond, ada_w, ada_b.reshape(DEPTH, 1, 3 * D_MODEL))


def _modulated_norm(x_ref, mod_ref, ng_ref, row):
    x = x_ref[...]
    ms = jnp.mean(x * x, axis=-1, keepdims=True)
    shift = mod_ref[pl.ds(row, 1), 0:D_MODEL]
    scale = mod_ref[pl.ds(row, 1), D_MODEL:2 * D_MODEL]
    y = (x * lax.rsqrt(ms + EPS)) * ng_ref[...]
    return (y * (1.0 + scale) + shift).astype(BF16)


def _head_norm(p, gain):
    ms = jnp.mean(p * p, axis=-1, keepdims=True)
    return (p * lax.rsqrt(ms + EPS)) * gain


def _rope(y, cos, sin, swap_lo):
    partner = jnp.where(swap_lo, pltpu.roll(y, 96, 1), pltpu.roll(y, 32, 1))
    return y * cos + partner * sin


def _inproj_kernel(*refs, rope, tiles_per_batch):
    if rope:
        (x_ref, mod_ref, ng_ref, w_ref, qg_ref, kg_ref, cos_ref, sin_ref,
         q_ref, k_ref, vt_ref, up_ref, uf_ref, g_ref, h_ref) = refs
    else:
        (x_ref, mod_ref, ng_ref, w_ref, qg_ref, kg_ref,
         q_ref, k_ref, vt_ref, up_ref, uf_ref, g_ref, h_ref) = refs
    tm = x_ref.shape[0]
    row = CTX_MOD_ROW if tiles_per_batch is None else pl.program_id(0) // tiles_per_batch
    h_ref[...] = _modulated_norm(x_ref, mod_ref, ng_ref, row)

    if rope:
        cos = cos_ref[...]
        sin = sin_ref[...]
        lane = lax.broadcasted_iota(jnp.int32, (tm, HEAD_DIM), 1)
        swap_lo = (lane & (AXIS_ROT // 2)) == 0

    def finish_head(p, gain):
        y = _head_norm(p, gain)
        if rope:
            y = _rope(y, cos, sin, swap_lo)
        return y.astype(BF16)

    q_gain = qg_ref[...] * (HEAD_DIM ** -0.5 * math.log2(math.e))
    chunk = GQA_GROUP * HEAD_DIM
    for c0 in range(0, ATTN_W, chunk):
        p = _dot(h_ref[...], w_ref[:, c0:c0 + chunk])
        for hh in range(GQA_GROUP):
            lo = hh * HEAD_DIM
            q_ref[:, c0 + lo:c0 + lo + HEAD_DIM] = finish_head(p[:, lo:lo + HEAD_DIM], q_gain)

    p = _dot(h_ref[...], w_ref[:, OFF_K:OFF_POOL])
    k_gain = kg_ref[...]
    for hh in range(N_KV_HEADS):
        lo = hh * HEAD_DIM
        k_ref[:, lo:lo + HEAD_DIM] = finish_head(p[:, lo:lo + HEAD_DIM], k_gain)
    vt_ref[...] = p[:, KV_W:2 * KV_W].T.astype(BF16)

    up_ref[...] = _dot(h_ref[...], w_ref[:, OFF_POOL:OFF_FOURIER])
    uf_ref[...] = _dot(h_ref[...], w_ref[:, OFF_FOURIER:OFF_GATE]).astype(BF16)
    for c0 in range(0, D_MODEL, chunk):
        g_ref[:, c0:c0 + chunk] = _dot(h_ref[...], w_ref[:, OFF_GATE + c0:OFF_GATE + c0 + chunk])


def _vt_spec(tm, tiles_per_seq):
    return pl.BlockSpec((None, KV_W, tm), lambda i: (i // tiles_per_seq, 0, i % tiles_per_seq))


def _inproj(x2d, mod, norm_g, w_in, q_g, k_g, rope_tabs, l, *, tm, seq_len, per_batch_mod):
    rows = x2d.shape[0]
    rope = rope_tabs is not None
    tiles_per_seq = seq_len // tm
    row_spec = lambda w: pl.BlockSpec((tm, w), lambda i: (i, 0))
    in_specs = [row_spec(D_MODEL), _layer((MOD_ROWS, 3 * D_MODEL), l), _layer((1, D_MODEL), l),
                _layer((D_MODEL, IN_W), l), _layer((1, HEAD_DIM), l), _layer((1, HEAD_DIM), l)]
    args = [x2d, mod, norm_g, w_in, q_g, k_g]
    if rope:
        tab_spec = pl.BlockSpec((tm, HEAD_DIM), lambda i: (i % tiles_per_seq, 0))
        in_specs += [tab_spec, tab_spec]
        args += list(rope_tabs)
    row_out = lambda w, dt: jax.ShapeDtypeStruct((rows, w), dt)
    return pl.pallas_call(
        functools.partial(_inproj_kernel, rope=rope,
                          tiles_per_batch=tiles_per_seq if per_batch_mod else None),
        grid=(rows // tm,),
        in_specs=in_specs,
        out_specs=[row_spec(ATTN_W), row_spec(KV_W), _vt_spec(tm, tiles_per_seq),
                   row_spec(POOL_W), row_spec(FOURIER_W), row_spec(D_MODEL)],
        out_shape=[row_out(ATTN_W, BF16), row_out(KV_W, BF16),
                   jax.ShapeDtypeStruct((rows // seq_len, KV_W, seq_len), BF16),
                   row_out(POOL_W, F32), row_out(FOURIER_W, BF16), row_out(D_MODEL, F32)],
        scratch_shapes=[pltpu.VMEM((tm, D_MODEL), BF16)],
        compiler_params=_params(1),
        name="inproj_rope" if rope else "inproj_ctx",
    )(*args)


def _ctx_kv_kernel(x_ref, mod_ref, ng_ref, w_ref, kg_ref, k_ref, vt_ref):
    h = _modulated_norm(x_ref, mod_ref, ng_ref, CTX_MOD_ROW)
    p = _dot(h, w_ref[...])
    k_gain = kg_ref[...]
    for hh in range(N_KV_HEADS):
        lo = hh * HEAD_DIM
        k_ref[:, lo:lo + HEAD_DIM] = _head_norm(p[:, lo:lo + HEAD_DIM], k_gain).astype(BF16)
    vt_ref[...] = p[:, KV_W:2 * KV_W].T.astype(BF16)


def _ctx_kv(x2d, mod, norm_g, w_in, k_g, l, *, tm, seq_len):
    rows = x2d.shape[0]
    row_spec = lambda w: pl.BlockSpec((tm, w), lambda i: (i, 0))
    kv_cols = (0, OFF_K // (2 * KV_W))
    return pl.pallas_call(
        _ctx_kv_kernel,
        grid=(rows // tm,),
        in_specs=[row_spec(D_MODEL), _layer((MOD_ROWS, 3 * D_MODEL), l), _layer((1, D_MODEL), l),
                  _layer((D_MODEL, 2 * KV_W), l, kv_cols), _layer((1, HEAD_DIM), l)],
        out_specs=[row_spec(KV_W), _vt_spec(tm, seq_len // tm)],
        out_shape=[jax.ShapeDtypeStruct((rows, KV_W), BF16),
                   jax.ShapeDtypeStruct((rows // seq_len, KV_W, seq_len), BF16)],
        compiler_params=_params(1),
        name="ctx_kv",
    )(x2d, mod, norm_g, w_in, k_g)


def _attn_kernel(*refs, lengths, tq):
    n_sources = len(lengths)
    q_ref, g_ref = refs[0], refs[1]
    k_refs = refs[2:2 + 2 * n_sources:2]
    vt_refs = refs[3:3 + 2 * n_sources:2]
    o_ref = refs[2 + 2 * n_sources]
    s_ref, p_ref = refs[3 + 2 * n_sources:]
    offsets = [sum(lengths[:i]) for i in range(n_sources)]
    total = sum(lengths)
    n_tiles = q_ref.shape[0] // tq
    key_chunks = [(k_ref, off, r0) for k_ref, off, length in zip(k_refs, offsets, lengths)
                  for r0 in range(0, length, KEY_CHUNK)]
    head_cols = [slice(hh * HEAD_DIM, (hh + 1) * HEAD_DIM) for hh in range(GQA_GROUP)]

    def q_rows(tile):
        return pl.ds(pl.multiple_of(tile * tq, tq), tq)

    def score_chunk(tile, hh, slot, chunk, col_max):
        k_ref, off, r0 = key_chunks[chunk]
        s = _dot_nt(k_ref[r0:r0 + KEY_CHUNK, :], q_ref[q_rows(tile), head_cols[hh]])
        s_ref[slot, off + r0:off + r0 + KEY_CHUNK, :] = s
        cm = jnp.max(s, axis=0, keepdims=True)
        return cm if col_max is None else jnp.maximum(col_max, cm)

    def prob_chunk(slot, chunk, col_max, denom):
        for r0 in range(chunk * KEY_CHUNK, (chunk + 1) * KEY_CHUNK, EXP_CHUNK):
            p = jnp.exp2(s_ref[slot, r0:r0 + EXP_CHUNK, :] - col_max)
            ps = jnp.sum(p, axis=0, keepdims=True)
            denom = ps if denom is None else denom + ps
            p_ref[slot, r0:r0 + EXP_CHUNK, :] = p.astype(BF16)
        return denom

    def finish(tile, hh, slot, denom):
        acc = None
        for vt_ref, off, length in zip(vt_refs, offsets, lengths):
            part = _dot(vt_ref[...], p_ref[slot, off:off + length, :])
            acc = part if acc is None else acc + part
        out = (acc / denom).T
        gate = _silu(g_ref[q_rows(tile), head_cols[hh]])
        o_ref[q_rows(tile), head_cols[hh]] = (out * gate).astype(BF16)

    n_chunks = total // KEY_CHUNK
    first_max = None
    for chunk in range(n_chunks):
        first_max = score_chunk(0, 0, 0, chunk, first_max)

    def tile_body(tile, cur_max):
        for hh in range(GQA_GROUP):
            slot, next_slot = hh % 2, (hh + 1) % 2
            next_hh = (hh + 1) % GQA_GROUP
            next_tile = tile if next_hh else jnp.minimum(tile + 1, n_tiles - 1)
            next_max, denom = None, None
            for chunk in range(n_chunks):
                next_max = score_chunk(next_tile, next_hh, next_slot, chunk, next_max)
                denom = prob_chunk(slot, chunk, cur_max, denom)
            finish(tile, hh, slot, denom)
            cur_max = next_max
        return cur_max

    lax.fori_loop(0, n_tiles, tile_body, first_max)


def _attention(q, g, sources, *, batch, q_len, tq):
    q_spec = pl.BlockSpec((q_len, GQA_GROUP * HEAD_DIM), lambda b, h: (b, h))
    in_specs = [q_spec, q_spec]
    args = [q, g]
    for k, vt, length in sources:
        in_specs += [pl.BlockSpec((length, HEAD_DIM), lambda b, h: (b, h)),
                     pl.BlockSpec((None, HEAD_DIM, length), lambda b, h: (b, h, 0))]
        args += [k, vt]
    lengths = tuple(length for _, _, length in sources)
    total = sum(lengths)
    return pl.pallas_call(
        functools.partial(_attn_kernel, lengths=lengths, tq=tq),
        grid=(batch, N_KV_HEADS),
        in_specs=in_specs,
        out_specs=q_spec,
        out_shape=jax.ShapeDtypeStruct((batch * q_len, ATTN_W), BF16),
        scratch_shapes=[pltpu.VMEM((2, total, tq), F32), pltpu.VMEM((2, total, tq), BF16)],
        compiler_params=_params(2),
        name="attention",
    )(*args)


def _pool_kernel(up_ref, g_ref, pw_ref, ps_ref, o_ref, pad_ref, *, n, chunk):
    zeros = jnp.zeros((POOL_HALO, POOL_W), F32)
    pad_ref[0:POOL_HALO, :] = zeros
    pad_ref[POOL_HALO + n:2 * POOL_HALO + n, :] = zeros
    pad_ref[POOL_HALO:POOL_HALO + n, :] = up_ref[...]
    for gi, win in enumerate(POOL_WINDOWS):
        cols = slice(gi * GROUP_C, (gi + 1) * GROUP_C)
        w = pw_ref[gi].astype(BF16)
        before, after = win // 2, win - win // 2 - 1
        for r0 in range(0, n, chunk):
            base = POOL_HALO + r0
            acc = pad_ref[base - before:base - before + chunk, cols]
            for d in range(-before + 1, after + 1):
                acc = acc + pad_ref[base + d:base + d + chunk, cols]
            t = r0 + lax.broadcasted_iota(jnp.int32, (chunk, GROUP_C), 0)
            cnt = jnp.minimum(t + after, n - 1) - jnp.maximum(t - before, 0) + 1
            pooled = acc / cnt.astype(F32) - pad_ref[base:base + chunk, cols]
            y = _dot(pooled.astype(BF16), w) * ps_ref[:, cols]
            o_ref[r0:r0 + chunk, cols] = (y * _silu(g_ref[r0:r0 + chunk, cols])).astype(BF16)


def _pool(up, g, pool_w, pool_scale, l, *, batch, n):
    seq_spec = lambda col: pl.BlockSpec((n, POOL_W), lambda b: (b, col))
    return pl.pallas_call(
        functools.partial(_pool_kernel, n=n, chunk=min(n, 256)),
        grid=(batch,),
        in_specs=[seq_spec(0), seq_spec(ATTN_W // POOL_W),
                  _layer((MIX_GROUPS, GROUP_C, GROUP_C), l), _layer((1, POOL_W), l)],
        out_specs=seq_spec(0),
        out_shape=jax.ShapeDtypeStruct((batch * n, POOL_W), BF16),
        scratch_shapes=[pltpu.VMEM((n + 2 * POOL_HALO, POOL_W), F32)],
        compiler_params=_params(1),
        name="pool",
    )(up, g, pool_w, pool_scale)


def _fourier_kernel(u_ref, g_ref, fw_ref, cc_ref, sc_ref, cn_ref, sn_ref, o_ref, ua_ref, ub_ref,
                    *, n, chunk):
    for gi in range(MIX_GROUPS):
        cols = slice(gi * GROUP_C, (gi + 1) * GROUP_C)
        w = fw_ref[gi].astype(BF16)
        a = _dot(cc_ref[...], w).astype(BF16)
        b = _dot(sc_ref[...], w).astype(BF16)
        ug = u_ref[:, cols]
        ua_ref[:, cols] = _dot(ug, a).astype(BF16)
        ub_ref[:, cols] = _dot(ug, b).astype(BF16)
    for r0 in range(0, n, chunk):
        rows = slice(r0, r0 + chunk)
        y = _dot(cn_ref[rows, :], ua_ref[...]) - _dot(sn_ref[rows, :], ub_ref[...])
        o_ref[rows, :] = (y * _silu(g_ref[rows, :])).astype(BF16)


def _fourier(uf, g, fourier_w, tabs, l, *, batch, n):
    cc, sc, cn, sn = tabs
    seq_spec = lambda col: pl.BlockSpec((n, FOURIER_W), lambda b: (b, col))
    return pl.pallas_call(
        functools.partial(_fourier_kernel, n=n, chunk=min(n, 512)),
        grid=(batch,),
        in_specs=[seq_spec(0), seq_spec((ATTN_W + POOL_W) // FOURIER_W),
                  _layer((MIX_GROUPS, GROUP_C, GROUP_C), l),
                  _resident((GROUP_C, GROUP_C)), _resident((GROUP_C, GROUP_C)),
                  _resident((n, n)), _resident((n, n))],
        out_specs=seq_spec(0),
        out_shape=jax.ShapeDtypeStruct((batch * n, FOURIER_W), BF16),
        scratch_shapes=[pltpu.VMEM((n, FOURIER_W), BF16), pltpu.VMEM((n, FOURIER_W), BF16)],
        compiler_params=_params(1),
        name="fourier",
    )(uf, g, fourier_w, cc, sc, cn, sn)


def _outproj_kernel(*refs, tiles_per_batch, final):
    if final:
        att_ref, pool_ref, four_ref, w_ref, x_ref, mod_ref, fg_ref, o_ref = refs
    else:
        att_ref, pool_ref, four_ref, w_ref, x_ref, mod_ref, o_ref = refs
    row = CTX_MOD_ROW if tiles_per_batch is None else pl.program_id(0) // tiles_per_batch
    gate = mod_ref[pl.ds(row, 1), 2 * D_MODEL:3 * D_MODEL]
    out = (_dot(att_ref[...], w_ref[0:ATTN_W, :])
           + _dot(pool_ref[...], w_ref[ATTN_W:ATTN_W + POOL_W, :])
           + _dot(four_ref[...], w_ref[ATTN_W + POOL_W:D_MODEL, :]))
    y = x_ref[...] + gate * out
    if final:
        ms = jnp.mean(y * y, axis=-1, keepdims=True)
        y = (y * lax.rsqrt(ms + EPS)) * fg_ref[...]
    o_ref[...] = y


def _outproj(att, pool, four, w_out, x2d, mod, final_g, l, *, tm, rows_per_batch):
    rows = x2d.shape[0]
    final = final_g is not None
    tiles_per_batch = None if rows_per_batch is None else rows_per_batch // tm
    row_spec = lambda w: pl.BlockSpec((tm, w), lambda i: (i, 0))
    in_specs = [row_spec(ATTN_W), row_spec(POOL_W), row_spec(FOURIER_W),
                _layer((D_MODEL, D_MODEL), l), row_spec(D_MODEL), _layer((MOD_ROWS, 3 * D_MODEL), l)]
    args = [att, pool, four, w_out, x2d, mod]
    if final:
        in_specs.append(_resident((1, D_MODEL)))
        args.append(final_g)
    return pl.pallas_call(
        functools.partial(_outproj_kernel, tiles_per_batch=tiles_per_batch, final=final),
        grid=(rows // tm,),
        in_specs=in_specs,
        out_specs=row_spec(D_MODEL),
        out_shape=jax.ShapeDtypeStruct((rows, D_MODEL), F32),
        compiler_params=_params(1),
        name="outproj_final" if final else "outproj",
    )(*args)


def _rope_tables(n):
    pos = jnp.arange(n, dtype=jnp.int32)
    row = (pos // GRID_W).astype(F32)
    col = (pos % GRID_W).astype(F32)
    inv = ROPE_THETA ** (-jnp.arange(0, AXIS_ROT, 2, dtype=F32) / AXIS_ROT)
    ang_r = row[:, None] * inv
    ang_c = col[:, None] * inv
    cos = jnp.concatenate([jnp.cos(ang_r), jnp.cos(ang_r), jnp.cos(ang_c), jnp.cos(ang_c)], axis=1)
    sin = jnp.concatenate([-jnp.sin(ang_r), jnp.sin(ang_r), -jnp.sin(ang_c), jnp.sin(ang_c)], axis=1)
    return cos, sin


def _dft_cos_sin(n):
    k = jnp.arange(n, dtype=jnp.int32)
    ang = ((k[:, None] * k[None, :]) % n).astype(F32) * (2.0 * math.pi / n)
    return jnp.cos(ang), jnp.sin(ang)


def _fourier_tables(n):
    cc, sc = _dft_cos_sin(GROUP_C)
    cn, sn = _dft_cos_sin(n)
    norm = 1.0 / math.sqrt(n * GROUP_C)
    return (cc * norm).astype(BF16), (sc * norm).astype(BF16), cn.astype(BF16), sn.astype(BF16)


def kernel(x, c, ctx, c_ctx, ada_w, ada_b, norm_g, w_in, q_norm_g, k_norm_g, pool_w, pool_scale,
           fourier_w, w_out, final_norm_g):
    B, N, _ = x.shape
    C = ctx.shape[1]
    tm = 512
    tm_ctx = 256

    cond = jnp.concatenate([c, c_ctx[None, :], jnp.zeros((MOD_ROWS - B - 1, D_MODEL), F32)], axis=0)
    mod = _adaln(cond, ada_w, ada_b)

    rope_tabs = _rope_tables(N)
    four_tabs = _fourier_tables(N)
    four_tabs_ctx = _fourier_tables(C)
    w_in_b = w_in.astype(BF16)
    w_out_b = w_out.astype(BF16)
    ng = norm_g[:, None, :]
    qg = q_norm_g[:, None, :]
    kg = k_norm_g[:, None, :]
    ps = pool_scale[:, None, :]

    xl = x.reshape(B * N, D_MODEL)
    xc = ctx.reshape(B * C, D_MODEL)
    for l in range(DEPTH):
        last = l == DEPTH - 1
        if last:
            kc, vct = _ctx_kv(xc, mod, ng, w_in_b, kg, l, tm=tm_ctx, seq_len=C)
        else:
            qc, kc, vct, upc, ufc, gc = _inproj(xc, mod, ng, w_in_b, qg, kg, None, l,
                                                tm=tm_ctx, seq_len=C, per_batch_mod=False)
        q, k, vt, up, uf, g = _inproj(xl, mod, ng, w_in_b, qg, kg, rope_tabs, l,
                                      tm=tm, seq_len=N, per_batch_mod=True)

        att = _attention(q, g, [(kc, vct, C), (k, vt, N)], batch=B, q_len=N, tq=256)
        pooled = _pool(up, g, pool_w, ps, l, batch=B, n=N)
        four = _fourier(uf, g, fourier_w, four_tabs, l, batch=B, n=N)
        xl_new = _outproj(att, pooled, four, w_out_b, xl, mod,
                          final_norm_g[None, :] if last else None, l, tm=tm, rows_per_batch=N)

        if not last:
            attc = _attention(qc, gc, [(kc, vct, C)], batch=B, q_len=C, tq=C)
            pooledc = _pool(upc, gc, pool_w, ps, l, batch=B, n=C)
            fourc = _fourier(ufc, gc, fourier_w, four_tabs_ctx, l, batch=B, n=C)
            xc = _outproj(attc, pooledc, fourc, w_out_b, xc, mod, None, l,
                          tm=tm_ctx, rows_per_batch=None)
        xl = xl_new
    return xl.reshape(B, N, D_MODEL)
```

```python
import functools
import math

import jax
import jax.numpy as jnp
from jax import lax
from jax.experimental import pallas as pl
from jax.experimental.pallas import tpu as pltpu

D_MODEL = 2048
DEPTH = 2
GRID_W = 64
HEAD_DIM = 128
ATTN_W = 1024
N_HEADS = 8
N_KV_HEADS = 2
GQA_GROUP = 4
KV_W = 256
POOL_W = 512
POOL_WINDOWS = (2, 4, 8, 16)
FOURIER_W = 512
MIX_GROUPS = 4
GROUP_C = 128
OFF_K = ATTN_W
OFF_V = OFF_K + KV_W
OFF_POOL = OFF_V + KV_W
OFF_FOURIER = OFF_POOL + POOL_W
OFF_GATE = OFF_FOURIER + FOURIER_W
IN_W = OFF_GATE + D_MODEL
ROPE_THETA = 10000.0
AXIS_ROT = HEAD_DIM // 2
EPS = 1e-6

MOD_ROWS = 16
CTX_MOD_ROW = 8
SUBLANES = 8
POOL_HALO = 8
KEY_CHUNK = 256
EXP_CHUNK = 128
V7X_VMEM_BYTES = 64 * 1024 * 1024
VMEM_LIMIT = V7X_VMEM_BYTES - 8 * 1024 * 1024

BF16 = jnp.bfloat16
F32 = jnp.float32


def _silu(x):
    return x * jax.nn.sigmoid(x)


def _dot(a, b):
    return jnp.dot(a, b, preferred_element_type=F32)


def _dot_nt(a, b):
    return lax.dot_general(a, b, (((1,), (1,)), ((), ())), preferred_element_type=F32)


def _resident(shape):
    zeros = (0,) * len(shape)
    return pl.BlockSpec(shape, lambda *_: zeros, pipeline_mode=pl.Buffered(1))


def _layer(shape, l, tail=None):
    index = (l,) + (tail or (0,) * len(shape))
    return pl.BlockSpec((None,) + tuple(shape), lambda *_: index, pipeline_mode=pl.Buffered(1))


def _params(n_grid):
    return pltpu.CompilerParams(dimension_semantics=("arbitrary",) * n_grid,
                                vmem_limit_bytes=VMEM_LIMIT)


def _adaln_kernel(c_ref, w_ref, b_ref, o_ref):
    a = _silu(c_ref[...]).astype(BF16)
    o_ref[...] = _dot(a, w_ref[...].astype(BF16)) + b_ref[...]


def _adaln(cond, ada_w, ada_b):
    tn = 1024
    return pl.pallas_call(
        _adaln_kernel,
        grid=(DEPTH, 3 * D_MODEL // tn),
        in_specs=[
            pl.BlockSpec((MOD_ROWS, D_MODEL), lambda l, j: (0, 0)),
            pl.BlockSpec((None, D_MODEL, tn), lambda l, j: (l, 0, j)),
            pl.BlockSpec((None, 1, tn), lambda l, j: (l, 0, j)),
        ],
        out_specs=pl.BlockSpec((None, MOD_ROWS, tn), lambda l, j: (l, 0, j)),
        out_shape=jax.ShapeDtypeStruct((DEPTH, MOD_ROWS, 3 * D_MODEL), F32),
        compiler_params=_params(2),
        name="adaln",
    )(cond, ada_w, ada_b.reshape(DEPTH, 1, 3 * D_MODEL))


def _modulated_norm(x_ref, mod_ref, ng_ref, row):
    x = x_ref[...]
    ms = jnp.mean(x * x, axis=-1, keepdims=True)
    shift = mod_ref[pl.ds(row, 1), 0:D_MODEL]
    scale = mod_ref[pl.ds(row, 1), D_MODEL:2 * D_MODEL]
    y = (x * lax.rsqrt(ms + EPS)) * ng_ref[...]
    return (y * (1.0 + scale) + shift).astype(BF16)


def _head_norm(p, gain):
    ms = jnp.mean(p * p, axis=-1, keepdims=True)
    return (p * lax.rsqrt(ms + EPS)) * gain


def _rope(y, cos, sin, swap_lo):
    partner = jnp.where(swap_lo, pltpu.roll(y, 96, 1), pltpu.roll(y, 32, 1))
    return y * cos + partner * sin


def _inproj_kernel(*refs, rope, tiles_per_batch):
    if rope:
        (x_ref, mod_ref, ng_ref, w_ref, qg_ref, kg_ref, cos_ref, sin_ref,
         q_ref, k_ref, vt_ref, up_ref, uf_ref, g_ref, h_ref) = refs
    else:
        (x_ref, mod_ref, ng_ref, w_ref, qg_ref, kg_ref,
         q_ref, k_ref, vt_ref, up_ref, uf_ref, g_ref, h_ref) = refs
    tm = x_ref.shape[0]
    row = CTX_MOD_ROW if tiles_per_batch is None else pl.program_id(0) // tiles_per_batch
    h_ref[...] = _modulated_norm(x_ref, mod_ref, ng_ref, row)

    if rope:
        cos = cos_ref[...]
        sin = sin_ref[...]
        lane = lax.broadcasted_iota(jnp.int32, (tm, HEAD_DIM), 1)
        swap_lo = (lane & (AXIS_ROT // 2)) == 0

    def finish_head(p, gain):
        y = _head_norm(p, gain)
        if rope:
            y = _rope(y, cos, sin, swap_lo)
        return y.astype(BF16)

    q_gain = qg_ref[...] * (HEAD_DIM ** -0.5 * math.log2(math.e))
    chunk = GQA_GROUP * HEAD_DIM
    for c0 in range(0, ATTN_W, chunk):
        p = _dot(h_ref[...], w_ref[:, c0:c0 + chunk])
        for hh in range(GQA_GROUP):
            lo = hh * HEAD_DIM
            q_ref[:, c0 + lo:c0 + lo + HEAD_DIM] = finish_head(p[:, lo:lo + HEAD_DIM], q_gain)

    p = _dot(h_ref[...], w_ref[:, OFF_K:OFF_POOL])
    k_gain = kg_ref[...]
    for hh in range(N_KV_HEADS):
        lo = hh * HEAD_DIM
        k_ref[:, lo:lo + HEAD_DIM] = finish_head(p[:, lo:lo + HEAD_DIM], k_gain)
    vt_ref[...] = p[:, KV_W:2 * KV_W].T.astype(BF16)

    up_ref[...] = _dot(h_ref[...], w_ref[:, OFF_POOL:OFF_FOURIER])
    uf_ref[...] = _dot(h_ref[...], w_ref[:, OFF_FOURIER:OFF_GATE]).astype(BF16)
    for c0 in range(0, D_MODEL, chunk):
        g_ref[:, c0:c0 + chunk] = _dot(h_ref[...], w_ref[:, OFF_GATE + c0:OFF_GATE + c0 + chunk])


def _vt_spec(tm, tiles_per_seq):
    return pl.BlockSpec((None, KV_W, tm), lambda i: (i // tiles_per_seq, 0, i % tiles_per_seq))


def _inproj(x2d, mod, norm_g, w_in, q_g, k_g, rope_tabs, l, *, tm, seq_len, per_batch_mod):
    rows = x2d.shape[0]
    rope = rope_tabs is not None
    tiles_per_seq = seq_len // tm
    row_spec = lambda w: pl.BlockSpec((tm, w), lambda i: (i, 0))
    in_specs = [row_spec(D_MODEL), _layer((MOD_ROWS, 3 * D_MODEL), l), _layer((1, D_MODEL), l),
                _layer((D_MODEL, IN_W), l), _layer((1, HEAD_DIM), l), _layer((1, HEAD_DIM), l)]
    args = [x2d, mod, norm_g, w_in, q_g, k_g]
    if rope:
        tab_spec = pl.BlockSpec((tm, HEAD_DIM), lambda i: (i % tiles_per_seq, 0))
        in_specs += [tab_spec, tab_spec]
        args += list(rope_tabs)
    row_out = lambda w, dt: jax.ShapeDtypeStruct((rows, w), dt)
    return pl.pallas_call(
        functools.partial(_inproj_kernel, rope=rope,
                          tiles_per_batch=tiles_per_seq if per_batch_mod else None),
        grid=(rows // tm,),
        in_specs=in_specs,
        out_specs=[row_spec(ATTN_W), row_spec(KV_W), _vt_spec(tm, tiles_per_seq),
                   row_spec(POOL_W), row_spec(FOURIER_W), row_spec(D_MODEL)],
        out_shape=[row_out(ATTN_W, BF16), row_out(KV_W, BF16),
                   jax.ShapeDtypeStruct((rows // seq_len, KV_W, seq_len), BF16),
                   row_out(POOL_W, F32), row_out(FOURIER_W, BF16), row_out(D_MODEL, F32)],
        scratch_shapes=[pltpu.VMEM((tm, D_MODEL), BF16)],
        compiler_params=_params(1),
        name="inproj_rope" if rope else "inproj_ctx",
    )(*args)


def _ctx_kv_kernel(x_ref, mod_ref, ng_ref, w_ref, kg_ref, k_ref, vt_ref):
    h = _modulated_norm(x_ref, mod_ref, ng_ref, CTX_MOD_ROW)
    p = _dot(h, w_ref[...])
    k_gain = kg_ref[...]
    for hh in range(N_KV_HEADS):
        lo = hh * HEAD_DIM
        k_ref[:, lo:lo + HEAD_DIM] = _head_norm(p[:, lo:lo + HEAD_DIM], k_gain).astype(BF16)
    vt_ref[...] = p[:, KV_W:2 * KV_W].T.astype(BF16)


def _ctx_kv(x2d, mod, norm_g, w_in, k_g, l, *, tm, seq_len):
    rows = x2d.shape[0]
    row_spec = lambda w: pl.BlockSpec((tm, w), lambda i: (i, 0))
    kv_cols = (0, OFF_K // (2 * KV_W))
    return pl.pallas_call(
        _ctx_kv_kernel,
        grid=(rows // tm,),
        in_specs=[row_spec(D_MODEL), _layer((MOD_ROWS, 3 * D_MODEL), l), _layer((1, D_MODEL), l),
                  _layer((D_MODEL, 2 * KV_W), l, kv_cols), _layer((1, HEAD_DIM), l)],
        out_specs=[row_spec(KV_W), _vt_spec(tm, seq_len // tm)],
        out_shape=[jax.ShapeDtypeStruct((rows, KV_W), BF16),
                   jax.ShapeDtypeStruct((rows // seq_len, KV_W, seq_len), BF16)],
        compiler_params=_params(1),
        name="ctx_kv",
    )(x2d, mod, norm_g, w_in, k_g)


def _attn_kernel(*refs, lengths, tq):
    n_sources = len(lengths)
    q_ref, g_ref = refs[0], refs[1]
    k_refs = refs[2:2 + 2 * n_sources:2]
    vt_refs = refs[3:3 + 2 * n_sources:2]
    o_ref = refs[2 + 2 * n_sources]
    s_ref, p_ref = refs[3 + 2 * n_sources:]
    offsets = [sum(lengths[:i]) for i in range(n_sources)]
    total = sum(lengths)
    n_tiles = q_ref.shape[0] // tq
    key_chunks = [(k_ref, off, r0) for k_ref, off, length in zip(k_refs, offsets, lengths)
                  for r0 in range(0, length, KEY_CHUNK)]
    head_cols = [slice(hh * HEAD_DIM, (hh + 1) * HEAD_DIM) for hh in range(GQA_GROUP)]

    def q_rows(tile):
        return pl.ds(pl.multiple_of(tile * tq, tq), tq)

    def fold_rows(x, op):
        return op(x.reshape(x.shape[0] // SUBLANES, SUBLANES, tq), axis=0)

    def score_chunk(tile, hh, slot, chunk, col_max):
        k_ref, off, r0 = key_chunks[chunk]
        s = _dot_nt(k_ref[r0:r0 + KEY_CHUNK, :], q_ref[q_rows(tile), head_cols[hh]])
        s_ref[slot, off + r0:off + r0 + KEY_CHUNK, :] = s
        cm = fold_rows(s, jnp.max)
        return cm if col_max is None else jnp.maximum(col_max, cm)

    def prob_chunk(slot, chunk, col_max, denom):
        for r0 in range(chunk * KEY_CHUNK, (chunk + 1) * KEY_CHUNK, EXP_CHUNK):
            p = jnp.exp2(s_ref[slot, r0:r0 + EXP_CHUNK, :] - col_max)
            ps = fold_rows(p, jnp.sum)
            denom = ps if denom is None else denom + ps
            p_ref[slot, r0:r0 + EXP_CHUNK, :] = p.astype(BF16)
        return denom

    def finish(tile, hh, slot, denom):
        acc = None
        for vt_ref, off, length in zip(vt_refs, offsets, lengths):
            part = _dot(vt_ref[...], p_ref[slot, off:off + length, :])
            acc = part if acc is None else acc + part
        out = (acc / jnp.sum(denom, axis=0, keepdims=True)).T
        gate = _silu(g_ref[q_rows(tile), head_cols[hh]])
        o_ref[q_rows(tile), head_cols[hh]] = (out * gate).astype(BF16)

    n_chunks = total // KEY_CHUNK
    first_max = None
    for chunk in range(n_chunks):
        first_max = score_chunk(0, 0, 0, chunk, first_max)
    first_max = jnp.max(first_max, axis=0, keepdims=True)

    def tile_body(tile, cur_max):
        for hh in range(GQA_GROUP):
            slot, next_slot = hh % 2, (hh + 1) % 2
            next_hh = (hh + 1) % GQA_GROUP
            next_tile = tile if next_hh else jnp.minimum(tile + 1, n_tiles - 1)
            next_max, denom = None, None
            for chunk in range(n_chunks):
                next_max = score_chunk(next_tile, next_hh, next_slot, chunk, next_max)
                denom = prob_chunk(slot, chunk, cur_max, denom)
            finish(tile, hh, slot, denom)
            cur_max = jnp.max(next_max, axis=0, keepdims=True)
        return cur_max

    lax.fori_loop(0, n_tiles, tile_body, first_max)


def _attention(q, g, sources, *, batch, q_len, tq):
    q_spec = pl.BlockSpec((q_len, GQA_GROUP * HEAD_DIM), lambda b, h: (b, h))
    in_specs = [q_spec, q_spec]
    args = [q, g]
    for k, vt, length in sources:
        in_specs += [pl.BlockSpec((length, HEAD_DIM), lambda b, h: (b, h)),
                     pl.BlockSpec((None, HEAD_DIM, length), lambda b, h: (b, h, 0))]
        args += [k, vt]
    lengths = tuple(length for _, _, length in sources)
    total = sum(lengths)
    return pl.pallas_call(
        functools.partial(_attn_kernel, lengths=lengths, tq=tq),
        grid=(batch, N_KV_HEADS),
        in_specs=in_specs,
        out_specs=q_spec,
        out_shape=jax.ShapeDtypeStruct((batch * q_len, ATTN_W), BF16),
        scratch_shapes=[pltpu.VMEM((2, total, tq), F32), pltpu.VMEM((2, total, tq), BF16)],
        compiler_params=_params(2),
        name="attention",
    )(*args)


def _pool_kernel(up_ref, g_ref, pw_ref, ps_ref, o_ref, pad_ref, s2_ref, s4_ref, s8_ref,
                 *, n, chunk):
    halo = POOL_HALO
    pad_ref[0:halo, :] = jnp.zeros((halo, POOL_W), F32)
    pad_ref[halo + n:4 * halo + n, :] = jnp.zeros((3 * halo, POOL_W), F32)
    pad_ref[halo:halo + n, :] = up_ref[...]

    def forward_sum(dst_ref, src_ref, shift, length, lane0):
        for r0 in range(0, length, chunk):
            rows = min(chunk, length - r0)
            dst_ref[r0:r0 + rows, :] = (src_ref[r0:r0 + rows, lane0:]
                                        + src_ref[r0 + shift:r0 + shift + rows, lane0:])

    forward_sum(s2_ref, pad_ref, 1, n + 3 * halo, 0)
    forward_sum(s4_ref, s2_ref, 2, n + 2 * halo, GROUP_C)
    forward_sum(s8_ref, s4_ref, 4, n + halo, GROUP_C)

    def window_sum(gi, r0):
        if gi == 0:
            return s2_ref[halo - 1 + r0:halo - 1 + r0 + chunk, 0:GROUP_C]
        if gi == 1:
            return s4_ref[halo - 2 + r0:halo - 2 + r0 + chunk, 0:GROUP_C]
        if gi == 2:
            return s8_ref[halo - 4 + r0:halo - 4 + r0 + chunk, 0:GROUP_C]
        return (s8_ref[r0:r0 + chunk, GROUP_C:] + s8_ref[halo + r0:halo + r0 + chunk, GROUP_C:])

    for gi, win in enumerate(POOL_WINDOWS):
        cols = slice(gi * GROUP_C, (gi + 1) * GROUP_C)
        w = pw_ref[gi].astype(BF16)
        before, after = win // 2, win - win // 2 - 1
        for r0 in range(0, n, chunk):
            acc = window_sum(gi, r0)
            if r0 >= before and r0 + chunk - 1 + after <= n - 1:
                mean = acc * (1.0 / win)
            else:
                t = r0 + lax.broadcasted_iota(jnp.int32, (chunk, GROUP_C), 0)
                cnt = jnp.minimum(t + after, n - 1) - jnp.maximum(t - before, 0) + 1
                mean = acc / cnt.astype(F32)
            pooled = mean - pad_ref[halo + r0:halo + r0 + chunk, cols]
            y = _dot(pooled.astype(BF16), w) * ps_ref[:, cols]
            o_ref[r0:r0 + chunk, cols] = (y * _silu(g_ref[r0:r0 + chunk, cols])).astype(BF16)


def _pool(up, g, pool_w, pool_scale, l, *, batch, n):
    seq_spec = lambda col: pl.BlockSpec((n, POOL_W), lambda b: (b, col))
    return pl.pallas_call(
        functools.partial(_pool_kernel, n=n, chunk=min(n, 256)),
        grid=(batch,),
        in_specs=[seq_spec(0), seq_spec(ATTN_W // POOL_W),
                  _layer((MIX_GROUPS, GROUP_C, GROUP_C), l), _layer((1, POOL_W), l)],
        out_specs=seq_spec(0),
        out_shape=jax.ShapeDtypeStruct((batch * n, POOL_W), BF16),
        scratch_shapes=[pltpu.VMEM((n + 4 * POOL_HALO, POOL_W), F32),
                        pltpu.VMEM((n + 3 * POOL_HALO, POOL_W), F32),
                        pltpu.VMEM((n + 2 * POOL_HALO, POOL_W - GROUP_C), F32),
                        pltpu.VMEM((n + POOL_HALO, POOL_W - 2 * GROUP_C), F32)],
        compiler_params=_params(1),
        name="pool",
    )(up, g, pool_w, pool_scale)


def _fourier_kernel(u_ref, g_ref, fw_ref, cc_ref, sc_ref, cn_ref, sn_ref, o_ref, ua_ref, ub_ref,
                    *, n, chunk):
    for gi in range(MIX_GROUPS):
        cols = slice(gi * GROUP_C, (gi + 1) * GROUP_C)
        w = fw_ref[gi].astype(BF16)
        a = _dot(cc_ref[...], w).astype(BF16)
        b = _dot(sc_ref[...], w).astype(BF16)
        ug = u_ref[:, cols]
        ua_ref[:, cols] = _dot(ug, a).astype(BF16)
        ub_ref[:, cols] = _dot(ug, b).astype(BF16)
    for r0 in range(0, n, chunk):
        rows = slice(r0, r0 + chunk)
        y = _dot(cn_ref[rows, :], ua_ref[...]) - _dot(sn_ref[rows, :], ub_ref[...])
        o_ref[rows, :] = (y * _silu(g_ref[rows, :])).astype(BF16)


def _fourier(uf, g, fourier_w, tabs, l, *, batch, n):
    cc, sc, cn, sn = tabs
    seq_spec = lambda col: pl.BlockSpec((n, FOURIER_W), lambda b: (b, col))
    return pl.pallas_call(
        functools.partial(_fourier_kernel, n=n, chunk=min(n, 512)),
        grid=(batch,),
        in_specs=[seq_spec(0), seq_spec((ATTN_W + POOL_W) // FOURIER_W),
                  _layer((MIX_GROUPS, GROUP_C, GROUP_C), l),
                  _resident((GROUP_C, GROUP_C)), _resident((GROUP_C, GROUP_C)),
                  _resident((n, n)), _resident((n, n))],
        out_specs=seq_spec(0),
        out_shape=jax.ShapeDtypeStruct((batch * n, FOURIER_W), BF16),
        scratch_shapes=[pltpu.VMEM((n, FOURIER_W), BF16), pltpu.VMEM((n, FOURIER_W), BF16)],
        compiler_params=_params(1),
        name="fourier",
    )(uf, g, fourier_w, cc, sc, cn, sn)


RADIX = 8
COMBINE_ROWS = 16
SQRT_HALF = math.sqrt(0.5)


def _radix8_real(yr, yi):
    sr = [yr[n] + yr[n + 4] for n in range(4)]
    dr = [yr[n] - yr[n + 4] for n in range(4)]
    si = {n: yi[n] + yi[n + 4] for n in (1, 3)}
    di = {n: yi[n] - yi[n + 4] for n in (1, 2, 3)}
    t0r = dr[0]
    t1r, t1i = (dr[1] + di[1]) * SQRT_HALF, (di[1] - dr[1]) * SQRT_HALF
    t2r = di[2]
    t3r, t3i = (di[3] - dr[3]) * SQRT_HALF, (dr[3] + di[3]) * -SQRT_HALF
    out = [None] * RADIX
    ea, eb = sr[0] + sr[2], sr[1] + sr[3]
    ec, ed = sr[0] - sr[2], si[1] - si[3]
    out[0], out[4] = ea + eb, ea - eb
    out[2], out[6] = ec + ed, ec - ed
    oa, ob = t0r + t2r, t1r + t3r
    oc, od = t0r - t2r, t1i - t3i
    out[1], out[5] = oa + ob, oa - ob
    out[3], out[7] = oc + od, oc - od
    return out


def _fourier_radix_kernel(u_ref, g_ref, fw_ref, cc_ref, sc_ref, cs_ref, tw_ref, o_ref,
                          z_ref, er_ref, ei_ref, *, n1):
    width = FOURIER_W
    for gi in range(MIX_GROUPS):
        w = fw_ref[gi].astype(BF16)
        wab = jnp.concatenate([_dot(cc_ref[...], w), _dot(sc_ref[...], w)], axis=1).astype(BF16)
        for n2 in range(RADIX):
            ab = _dot(u_ref[:, n2 * width + gi * GROUP_C:n2 * width + (gi + 1) * GROUP_C], wab)
            lo = 2 * n2 * width + gi * GROUP_C
            z_ref[:, lo:lo + GROUP_C] = ab[:, :GROUP_C].astype(BF16)
            z_ref[:, lo + width:lo + width + GROUP_C] = ab[:, GROUP_C:].astype(BF16)

    for n2 in range(RADIX):
        prod = _dot(cs_ref[...], z_ref[:, 2 * n2 * width:2 * (n2 + 1) * width])
        er_ref[n2] = prod[:n1, :width] - prod[n1:, width:]
        ei_ref[n2] = -(prod[n1:, :width] + prod[:n1, width:])

    for gi in range(MIX_GROUPS):
        lanes = slice(gi * GROUP_C, (gi + 1) * GROUP_C)

        def combine(step, carry, lanes=lanes):
            r0 = pl.multiple_of(step * COMBINE_ROWS, COMBINE_ROWS)
            rows = pl.ds(r0, COMBINE_ROWS)
            yr, yi = [er_ref[0, rows, lanes]], [ei_ref[0, rows, lanes]]
            for n2 in range(1, RADIX):
                er, ei = er_ref[n2, rows, lanes], ei_ref[n2, rows, lanes]
                tc = tw_ref[0, rows, n2 * GROUP_C:(n2 + 1) * GROUP_C]
                ts = tw_ref[1, rows, n2 * GROUP_C:(n2 + 1) * GROUP_C]
                yr.append(tc * er + ts * ei)
                yi.append(tc * ei - ts * er)
            for k2, val in enumerate(_radix8_real(yr, yi)):
                out_rows = pl.ds(k2 * n1 + r0, COMBINE_ROWS)
                o_ref[out_rows, lanes] = (val * _silu(g_ref[out_rows, lanes])).astype(BF16)
            return carry

        lax.fori_loop(0, n1 // COMBINE_ROWS, combine, 0)


def _fourier_radix(uf, g, fourier_w, tabs, l, *, batch, n):
    cc, sc, cs, tw = tabs
    n1 = n // RADIX
    seq_spec = lambda col: pl.BlockSpec((n, FOURIER_W), lambda b: (b, col))
    return pl.pallas_call(
        functools.partial(_fourier_radix_kernel, n1=n1),
        grid=(batch,),
        in_specs=[pl.BlockSpec((n1, RADIX * FOURIER_W), lambda b: (b, 0)),
                  seq_spec((ATTN_W + POOL_W) // FOURIER_W),
                  _layer((MIX_GROUPS, GROUP_C, GROUP_C), l),
                  _resident((GROUP_C, GROUP_C)), _resident((GROUP_C, GROUP_C)),
                  _resident((2 * n1, n1)), _resident((2, n1, RADIX * GROUP_C))],
        out_specs=seq_spec(0),
        out_shape=jax.ShapeDtypeStruct((batch * n, FOURIER_W), BF16),
        scratch_shapes=[pltpu.VMEM((n1, 2 * RADIX * FOURIER_W), BF16),
                        pltpu.VMEM((RADIX, n1, FOURIER_W), F32),
                        pltpu.VMEM((RADIX, n1, FOURIER_W), F32)],
        compiler_params=_params(1),
        name="fourier_radix",
    )(uf.reshape(batch * n1, RADIX * FOURIER_W), g, fourier_w, cc, sc, cs, tw)


def _outproj_kernel(*refs, tiles_per_batch, final):
    if final:
        att_ref, pool_ref, four_ref, w_ref, x_ref, mod_ref, fg_ref, o_ref = refs
    else:
        att_ref, pool_ref, four_ref, w_ref, x_ref, mod_ref, o_ref = refs
    row = CTX_MOD_ROW if tiles_per_batch is None else pl.program_id(0) // tiles_per_batch
    gate = mod_ref[pl.ds(row, 1), 2 * D_MODEL:3 * D_MODEL]
    out = (_dot(att_ref[...], w_ref[0:ATTN_W, :])
           + _dot(pool_ref[...], w_ref[ATTN_W:ATTN_W + POOL_W, :])
           + _dot(four_ref[...], w_ref[ATTN_W + POOL_W:D_MODEL, :]))
    y = x_ref[...] + gate * out
    if final:
        ms = jnp.mean(y * y, axis=-1, keepdims=True)
        y = (y * lax.rsqrt(ms + EPS)) * fg_ref[...]
    o_ref[...] = y


def _outproj(att, pool, four, w_out, x2d, mod, final_g, l, *, tm, rows_per_batch):
    rows = x2d.shape[0]
    final = final_g is not None
    tiles_per_batch = None if rows_per_batch is None else rows_per_batch // tm
    row_spec = lambda w: pl.BlockSpec((tm, w), lambda i: (i, 0))
    in_specs = [row_spec(ATTN_W), row_spec(POOL_W), row_spec(FOURIER_W),
                _layer((D_MODEL, D_MODEL), l), row_spec(D_MODEL), _layer((MOD_ROWS, 3 * D_MODEL), l)]
    args = [att, pool, four, w_out, x2d, mod]
    if final:
        in_specs.append(_resident((1, D_MODEL)))
        args.append(final_g)
    return pl.pallas_call(
        functools.partial(_outproj_kernel, tiles_per_batch=tiles_per_batch, final=final),
        grid=(rows // tm,),
        in_specs=in_specs,
        out_specs=row_spec(D_MODEL),
        out_shape=jax.ShapeDtypeStruct((rows, D_MODEL), F32),
        compiler_params=_params(1),
        name="outproj_final" if final else "outproj",
    )(*args)


def _rope_tables(n):
    pos = jnp.arange(n, dtype=jnp.int32)
    row = (pos // GRID_W).astype(F32)
    col = (pos % GRID_W).astype(F32)
    inv = ROPE_THETA ** (-jnp.arange(0, AXIS_ROT, 2, dtype=F32) / AXIS_ROT)
    ang_r = row[:, None] * inv
    ang_c = col[:, None] * inv
    cos = jnp.concatenate([jnp.cos(ang_r), jnp.cos(ang_r), jnp.cos(ang_c), jnp.cos(ang_c)], axis=1)
    sin = jnp.concatenate([-jnp.sin(ang_r), jnp.sin(ang_r), -jnp.sin(ang_c), jnp.sin(ang_c)], axis=1)
    return cos, sin


def _dft_cos_sin(n):
    k = jnp.arange(n, dtype=jnp.int32)
    ang = ((k[:, None] * k[None, :]) % n).astype(F32) * (2.0 * math.pi / n)
    return jnp.cos(ang), jnp.sin(ang)


def _fourier_tables(n):
    cc, sc = _dft_cos_sin(GROUP_C)
    cn, sn = _dft_cos_sin(n)
    norm = 1.0 / math.sqrt(n * GROUP_C)
    return (cc * norm).astype(BF16), (sc * norm).astype(BF16), cn.astype(BF16), sn.astype(BF16)


def _fourier_radix_tables(n):
    n1 = n // RADIX
    cc, sc = _dft_cos_sin(GROUP_C)
    c1, s1 = _dft_cos_sin(n1)
    norm = 1.0 / math.sqrt(n * GROUP_C)
    k1 = jnp.arange(n1, dtype=jnp.int32)[:, None]
    n2 = jnp.arange(RADIX, dtype=jnp.int32)[None, :]
    ang = (k1 * n2).astype(F32) * (2.0 * math.pi / n)
    twiddle = jnp.stack([jnp.repeat(jnp.cos(ang), GROUP_C, axis=1),
                         jnp.repeat(jnp.sin(ang), GROUP_C, axis=1)])
    return ((cc * norm).astype(BF16), (sc * norm).astype(BF16),
            jnp.concatenate([c1, s1], axis=0).astype(BF16), twiddle)


def kernel(x, c, ctx, c_ctx, ada_w, ada_b, norm_g, w_in, q_norm_g, k_norm_g, pool_w, pool_scale,
           fourier_w, w_out, final_norm_g):
    B, N, _ = x.shape
    C = ctx.shape[1]
    tm = 512
    tm_ctx = 256

    cond = jnp.concatenate([c, c_ctx[None, :], jnp.zeros((MOD_ROWS - B - 1, D_MODEL), F32)], axis=0)
    mod = _adaln(cond, ada_w, ada_b)

    rope_tabs = _rope_tables(N)
    four_tabs = _fourier_radix_tables(N)
    four_tabs_ctx = _fourier_tables(C)
    w_in_b = w_in.astype(BF16)
    w_out_b = w_out.astype(BF16)
    ng = norm_g[:, None, :]
    qg = q_norm_g[:, None, :]
    kg = k_norm_g[:, None, :]
    ps = pool_scale[:, None, :]

    xl = x.reshape(B * N, D_MODEL)
    xc = ctx.reshape(B * C, D_MODEL)
    for l in range(DEPTH):
        last = l == DEPTH - 1
        if last:
            kc, vct = _ctx_kv(xc, mod, ng, w_in_b, kg, l, tm=tm_ctx, seq_len=C)
        else:
            qc, kc, vct, upc, ufc, gc = _inproj(xc, mod, ng, w_in_b, qg, kg, None, l,
                                                tm=tm_ctx, seq_len=C, per_batch_mod=False)
        q, k, vt, up, uf, g = _inproj(xl, mod, ng, w_in_b, qg, kg, rope_tabs, l,
                                      tm=tm, seq_len=N, per_batch_mod=True)

        att = _attention(q, g, [(kc, vct, C), (k, vt, N)], batch=B, q_len=N, tq=256)
        pooled = _pool(up, g, pool_w, ps, l, batch=B, n=N)
        four = _fourier_radix(uf, g, fourier_w, four_tabs, l, batch=B, n=N)
        xl_new = _outproj(att, pooled, four, w_out_b, xl, mod,
                          final_norm_g[None, :] if last else None, l, tm=tm, rows_per_batch=N)

        if not last:
            attc = _attention(qc, gc, [(kc, vct, C)], batch=B, q_len=C, tq=C)
            pooledc = _pool(upc, gc, pool_w, ps, l, batch=B, n=C)
            fourc = _fourier(ufc, gc, fourier_w, four_tabs_ctx, l, batch=B, n=C)
            xc = _outproj(attc, pooledc, fourc, w_out_b, xc, mod, None, l,
                          tm=tm_ctx, rows_per_batch=None)
        xl = xl_new
    return xl.reshape(B, N, D_MODEL)
```

```python
import functools
import math

import jax
import jax.numpy as jnp
from jax import lax
from jax.experimental import pallas as pl
from jax.experimental.pallas import tpu as pltpu

D_MODEL = 2048
DEPTH = 2
GRID_W = 64
HEAD_DIM = 128
ATTN_W = 1024
N_HEADS = 8
N_KV_HEADS = 2
GQA_GROUP = 4
KV_W = 256
POOL_W = 512
POOL_WINDOWS = (2, 4, 8, 16)
FOURIER_W = 512
MIX_GROUPS = 4
GROUP_C = 128
OFF_K = ATTN_W
OFF_V = OFF_K + KV_W
OFF_POOL = OFF_V + KV_W
OFF_FOURIER = OFF_POOL + POOL_W
OFF_GATE = OFF_FOURIER + FOURIER_W
IN_W = OFF_GATE + D_MODEL
ROPE_THETA = 10000.0
AXIS_ROT = HEAD_DIM // 2
EPS = 1e-6

MOD_ROWS = 16
CTX_MOD_ROW = 8
SUBLANES = 8
POOL_HALO = 8
KEY_CHUNK = 256
EXP_CHUNK = 128
V7X_VMEM_BYTES = 64 * 1024 * 1024
VMEM_LIMIT = V7X_VMEM_BYTES - 8 * 1024 * 1024

BF16 = jnp.bfloat16
F32 = jnp.float32


def _silu(x):
    return x * jax.nn.sigmoid(x)


def _dot(a, b):
    return jnp.dot(a, b, preferred_element_type=F32)


def _dot_nt(a, b):
    return lax.dot_general(a, b, (((1,), (1,)), ((), ())), preferred_element_type=F32)


def _resident(shape):
    zeros = (0,) * len(shape)
    return pl.BlockSpec(shape, lambda *_: zeros, pipeline_mode=pl.Buffered(1))


def _layer(shape, l, tail=None):
    index = (l,) + (tail or (0,) * len(shape))
    return pl.BlockSpec((None,) + tuple(shape), lambda *_: index, pipeline_mode=pl.Buffered(1))


def _params(n_grid):
    return pltpu.CompilerParams(dimension_semantics=("arbitrary",) * n_grid,
                                vmem_limit_bytes=VMEM_LIMIT)


def _adaln_kernel(c_ref, w_ref, b_ref, o_ref):
    a = _silu(c_ref[...]).astype(BF16)
    o_ref[...] = _dot(a, w_ref[...].astype(BF16)) + b_ref[...]


def _adaln(cond, ada_w, ada_b):
    tn = 1024
    return pl.pallas_call(
        _adaln_kernel,
        grid=(DEPTH, 3 * D_MODEL // tn),
        in_specs=[
            pl.BlockSpec((MOD_ROWS, D_MODEL), lambda l, j: (0, 0)),
            pl.BlockSpec((None, D_MODEL, tn), lambda l, j: (l, 0, j)),
            pl.BlockSpec((None, 1, tn), lambda l, j: (l, 0, j)),
        ],
        out_specs=pl.BlockSpec((None, MOD_ROWS, tn), lambda l, j: (l, 0, j)),
        out_shape=jax.ShapeDtypeStruct((DEPTH, MOD_ROWS, 3 * D_MODEL), F32),
        compiler_params=_params(2),
        name="adaln",
    )(cond, ada_w, ada_b.reshape(DEPTH, 1, 3 * D_MODEL))


def _modulated_norm(x_ref, mod_ref, ng_ref, row):
    x = x_ref[...]
    ms = jnp.mean(x * x, axis=-1, keepdims=True)
    shift = mod_ref[pl.ds(row, 1), 0:D_MODEL]
    scale = mod_ref[pl.ds(row, 1), D_MODEL:2 * D_MODEL]
    y = (x * lax.rsqrt(ms + EPS)) * ng_ref[...]
    return (y * (1.0 + scale) + shift).astype(BF16)


def _head_norm(p, gain):
    ms = jnp.mean(p * p, axis=-1, keepdims=True)
    return (p * lax.rsqrt(ms + EPS)) * gain


def _rope(y, cos, sin, swap_lo):
    partner = jnp.where(swap_lo, pltpu.roll(y, 96, 1), pltpu.roll(y, 32, 1))
    return y * cos + partner * sin


def _inproj_kernel(*refs, rope, tiles_per_batch):
    if rope:
        (x_ref, mod_ref, ng_ref, w_ref, qg_ref, kg_ref, cos_ref, sin_ref,
         q_ref, k_ref, vt_ref, up_ref, uf_ref, g_ref, h_ref) = refs
    else:
        (x_ref, mod_ref, ng_ref, w_ref, qg_ref, kg_ref,
         q_ref, k_ref, vt_ref, up_ref, uf_ref, g_ref, h_ref) = refs
    tm = x_ref.shape[0]
    row = CTX_MOD_ROW if tiles_per_batch is None else pl.program_id(0) // tiles_per_batch
    h_ref[...] = _modulated_norm(x_ref, mod_ref, ng_ref, row)

    if rope:
        cos = cos_ref[...]
        sin = sin_ref[...]
        lane = lax.broadcasted_iota(jnp.int32, (tm, HEAD_DIM), 1)
        swap_lo = (lane & (AXIS_ROT // 2)) == 0

    def finish_head(p, gain):
        y = _head_norm(p, gain)
        if rope:
            y = _rope(y, cos, sin, swap_lo)
        return y.astype(BF16)

    q_gain = qg_ref[...] * (HEAD_DIM ** -0.5 * math.log2(math.e))
    chunk = GQA_GROUP * HEAD_DIM
    for c0 in range(0, ATTN_W, chunk):
        p = _dot(h_ref[...], w_ref[:, c0:c0 + chunk])
        for hh in range(GQA_GROUP):
            lo = hh * HEAD_DIM
            q_ref[:, c0 + lo:c0 + lo + HEAD_DIM] = finish_head(p[:, lo:lo + HEAD_DIM], q_gain)

    p = _dot(h_ref[...], w_ref[:, OFF_K:OFF_POOL])
    k_gain = kg_ref[...]
    for hh in range(N_KV_HEADS):
        lo = hh * HEAD_DIM
        k_ref[:, lo:lo + HEAD_DIM] = finish_head(p[:, lo:lo + HEAD_DIM], k_gain)
    vt_ref[...] = p[:, KV_W:2 * KV_W].T.astype(BF16)

    up_ref[...] = _dot(h_ref[...], w_ref[:, OFF_POOL:OFF_FOURIER])
    uf_ref[...] = _dot(h_ref[...], w_ref[:, OFF_FOURIER:OFF_GATE])
    for c0 in range(0, D_MODEL, chunk):
        g_ref[:, c0:c0 + chunk] = _dot(h_ref[...], w_ref[:, OFF_GATE + c0:OFF_GATE + c0 + chunk])


def _vt_spec(tm, tiles_per_seq):
    return pl.BlockSpec((None, KV_W, tm), lambda i: (i // tiles_per_seq, 0, i % tiles_per_seq))


def _inproj(x2d, mod, norm_g, w_in, q_g, k_g, rope_tabs, l, *, tm, seq_len, per_batch_mod):
    rows = x2d.shape[0]
    rope = rope_tabs is not None
    tiles_per_seq = seq_len // tm
    row_spec = lambda w: pl.BlockSpec((tm, w), lambda i: (i, 0))
    in_specs = [row_spec(D_MODEL), _layer((MOD_ROWS, 3 * D_MODEL), l), _layer((1, D_MODEL), l),
                _layer((D_MODEL, IN_W), l), _layer((1, HEAD_DIM), l), _layer((1, HEAD_DIM), l)]
    args = [x2d, mod, norm_g, w_in, q_g, k_g]
    if rope:
        tab_spec = pl.BlockSpec((tm, HEAD_DIM), lambda i: (i % tiles_per_seq, 0))
        in_specs += [tab_spec, tab_spec]
        args += list(rope_tabs)
    row_out = lambda w, dt: jax.ShapeDtypeStruct((rows, w), dt)
    return pl.pallas_call(
        functools.partial(_inproj_kernel, rope=rope,
                          tiles_per_batch=tiles_per_seq if per_batch_mod else None),
        grid=(rows // tm,),
        in_specs=in_specs,
        out_specs=[row_spec(ATTN_W), row_spec(KV_W), _vt_spec(tm, tiles_per_seq),
                   row_spec(POOL_W), row_spec(FOURIER_W), row_spec(D_MODEL)],
        out_shape=[row_out(ATTN_W, BF16), row_out(KV_W, BF16),
                   jax.ShapeDtypeStruct((rows // seq_len, KV_W, seq_len), BF16),
                   row_out(POOL_W, F32), row_out(FOURIER_W, F32), row_out(D_MODEL, F32)],
        scratch_shapes=[pltpu.VMEM((tm, D_MODEL), BF16)],
        compiler_params=_params(1),
        name="inproj_rope" if rope else "inproj_ctx",
    )(*args)


def _ctx_kv_kernel(x_ref, mod_ref, ng_ref, w_ref, kg_ref, k_ref, vt_ref):
    h = _modulated_norm(x_ref, mod_ref, ng_ref, CTX_MOD_ROW)
    p = _dot(h, w_ref[...])
    k_gain = kg_ref[...]
    for hh in range(N_KV_HEADS):
        lo = hh * HEAD_DIM
        k_ref[:, lo:lo + HEAD_DIM] = _head_norm(p[:, lo:lo + HEAD_DIM], k_gain).astype(BF16)
    vt_ref[...] = p[:, KV_W:2 * KV_W].T.astype(BF16)


def _ctx_kv(x2d, mod, norm_g, w_in, k_g, l, *, tm, seq_len):
    rows = x2d.shape[0]
    row_spec = lambda w: pl.BlockSpec((tm, w), lambda i: (i, 0))
    kv_cols = (0, OFF_K // (2 * KV_W))
    return pl.pallas_call(
        _ctx_kv_kernel,
        grid=(rows // tm,),
        in_specs=[row_spec(D_MODEL), _layer((MOD_ROWS, 3 * D_MODEL), l), _layer((1, D_MODEL), l),
                  _layer((D_MODEL, 2 * KV_W), l, kv_cols), _layer((1, HEAD_DIM), l)],
        out_specs=[row_spec(KV_W), _vt_spec(tm, seq_len // tm)],
        out_shape=[jax.ShapeDtypeStruct((rows, KV_W), BF16),
                   jax.ShapeDtypeStruct((rows // seq_len, KV_W, seq_len), BF16)],
        compiler_params=_params(1),
        name="ctx_kv",
    )(x2d, mod, norm_g, w_in, k_g)


def _attn_kernel(*refs, lengths, tq):
    n_sources = len(lengths)
    q_ref, g_ref = refs[0], refs[1]
    k_refs = refs[2:2 + 2 * n_sources:2]
    vt_refs = refs[3:3 + 2 * n_sources:2]
    o_ref = refs[2 + 2 * n_sources]
    s_ref, p_ref = refs[3 + 2 * n_sources:]
    offsets = [sum(lengths[:i]) for i in range(n_sources)]
    total = sum(lengths)
    n_tiles = q_ref.shape[0] // tq
    key_chunks = [(k_ref, off, r0) for k_ref, off, length in zip(k_refs, offsets, lengths)
                  for r0 in range(0, length, KEY_CHUNK)]
    head_cols = [slice(hh * HEAD_DIM, (hh + 1) * HEAD_DIM) for hh in range(GQA_GROUP)]

    def q_rows(tile):
        return pl.ds(pl.multiple_of(tile * tq, tq), tq)

    def fold_rows(x, op):
        return op(x.reshape(x.shape[0] // SUBLANES, SUBLANES, tq), axis=0)

    def score_chunk(tile, hh, slot, chunk, col_max):
        k_ref, off, r0 = key_chunks[chunk]
        s = _dot_nt(k_ref[r0:r0 + KEY_CHUNK, :], q_ref[q_rows(tile), head_cols[hh]])
        s_ref[slot, off + r0:off + r0 + KEY_CHUNK, :] = s
        cm = fold_rows(s, jnp.max)
        return cm if col_max is None else jnp.maximum(col_max, cm)

    def prob_chunk(slot, chunk, col_max, denom):
        for r0 in range(chunk * KEY_CHUNK, (chunk + 1) * KEY_CHUNK, EXP_CHUNK):
            p = jnp.exp2(s_ref[slot, r0:r0 + EXP_CHUNK, :] - col_max)
            ps = fold_rows(p, jnp.sum)
            denom = ps if denom is None else denom + ps
            p_ref[slot, r0:r0 + EXP_CHUNK, :] = p.astype(BF16)
        return denom

    def finish(tile, hh, slot, denom):
        acc = None
        for vt_ref, off, length in zip(vt_refs, offsets, lengths):
            part = _dot(vt_ref[...], p_ref[slot, off:off + length, :])
            acc = part if acc is None else acc + part
        out = (acc / jnp.sum(denom, axis=0, keepdims=True)).T
        gate = _silu(g_ref[q_rows(tile), head_cols[hh]])
        o_ref[q_rows(tile), head_cols[hh]] = (out * gate).astype(BF16)

    n_chunks = total // KEY_CHUNK
    first_max = None
    for chunk in range(n_chunks):
        first_max = score_chunk(0, 0, 0, chunk, first_max)
    first_max = jnp.max(first_max, axis=0, keepdims=True)

    def tile_body(tile, cur_max):
        for hh in range(GQA_GROUP):
            slot, next_slot = hh % 2, (hh + 1) % 2
            next_hh = (hh + 1) % GQA_GROUP
            next_tile = tile if next_hh else jnp.minimum(tile + 1, n_tiles - 1)
            next_max, denom = None, None
            for chunk in range(n_chunks):
                next_max = score_chunk(next_tile, next_hh, next_slot, chunk, next_max)
                denom = prob_chunk(slot, chunk, cur_max, denom)
            finish(tile, hh, slot, denom)
            cur_max = jnp.max(next_max, axis=0, keepdims=True)
        return cur_max

    lax.fori_loop(0, n_tiles, tile_body, first_max)


def _attention(q, g, sources, *, batch, q_len, tq):
    q_spec = pl.BlockSpec((q_len, GQA_GROUP * HEAD_DIM), lambda b, h: (b, h))
    in_specs = [q_spec, q_spec]
    args = [q, g]
    for k, vt, length in sources:
        in_specs += [pl.BlockSpec((length, HEAD_DIM), lambda b, h: (b, h)),
                     pl.BlockSpec((None, HEAD_DIM, length), lambda b, h: (b, h, 0))]
        args += [k, vt]
    lengths = tuple(length for _, _, length in sources)
    total = sum(lengths)
    return pl.pallas_call(
        functools.partial(_attn_kernel, lengths=lengths, tq=tq),
        grid=(batch, N_KV_HEADS),
        in_specs=in_specs,
        out_specs=q_spec,
        out_shape=jax.ShapeDtypeStruct((batch * q_len, ATTN_W), BF16),
        scratch_shapes=[pltpu.VMEM((2, total, tq), F32), pltpu.VMEM((2, total, tq), BF16)],
        compiler_params=_params(2),
        name="attention",
    )(*args)


def _pool_kernel(up_ref, g_ref, pw_ref, ps_ref, o_ref, pad_ref, s2_ref, s4_ref, s8_ref,
                 *, n, chunk):
    halo = POOL_HALO
    pad_ref[0:halo, :] = jnp.zeros((halo, POOL_W), F32)
    pad_ref[halo + n:4 * halo + n, :] = jnp.zeros((3 * halo, POOL_W), F32)
    pad_ref[halo:halo + n, :] = up_ref[...]

    def forward_sum(dst_ref, src_ref, shift, length, lane0):
        for r0 in range(0, length, chunk):
            rows = min(chunk, length - r0)
            dst_ref[r0:r0 + rows, :] = (src_ref[r0:r0 + rows, lane0:]
                                        + src_ref[r0 + shift:r0 + shift + rows, lane0:])

    forward_sum(s2_ref, pad_ref, 1, n + 3 * halo, 0)
    forward_sum(s4_ref, s2_ref, 2, n + 2 * halo, GROUP_C)
    forward_sum(s8_ref, s4_ref, 4, n + halo, GROUP_C)

    def window_sum(gi, r0):
        if gi == 0:
            return s2_ref[halo - 1 + r0:halo - 1 + r0 + chunk, 0:GROUP_C]
        if gi == 1:
            return s4_ref[halo - 2 + r0:halo - 2 + r0 + chunk, 0:GROUP_C]
        if gi == 2:
            return s8_ref[halo - 4 + r0:halo - 4 + r0 + chunk, 0:GROUP_C]
        return (s8_ref[r0:r0 + chunk, GROUP_C:] + s8_ref[halo + r0:halo + r0 + chunk, GROUP_C:])

    for gi, win in enumerate(POOL_WINDOWS):
        cols = slice(gi * GROUP_C, (gi + 1) * GROUP_C)
        w = pw_ref[gi].astype(BF16)
        before, after = win // 2, win - win // 2 - 1
        for r0 in range(0, n, chunk):
            acc = window_sum(gi, r0)
            if r0 >= before and r0 + chunk - 1 + after <= n - 1:
                mean = acc * (1.0 / win)
            else:
                t = r0 + lax.broadcasted_iota(jnp.int32, (chunk, GROUP_C), 0)
                cnt = jnp.minimum(t + after, n - 1) - jnp.maximum(t - before, 0) + 1
                mean = acc / cnt.astype(F32)
            pooled = mean - pad_ref[halo + r0:halo + r0 + chunk, cols]
            y = _dot(pooled.astype(BF16), w) * ps_ref[:, cols]
            o_ref[r0:r0 + chunk, cols] = (y * _silu(g_ref[r0:r0 + chunk, cols])).astype(BF16)


def _pool(up, g, pool_w, pool_scale, l, *, batch, n):
    seq_spec = lambda col: pl.BlockSpec((n, POOL_W), lambda b: (b, col))
    return pl.pallas_call(
        functools.partial(_pool_kernel, n=n, chunk=min(n, 256)),
        grid=(batch,),
        in_specs=[seq_spec(0), seq_spec(ATTN_W // POOL_W),
                  _layer((MIX_GROUPS, GROUP_C, GROUP_C), l), _layer((1, POOL_W), l)],
        out_specs=seq_spec(0),
        out_shape=jax.ShapeDtypeStruct((batch * n, POOL_W), BF16),
        scratch_shapes=[pltpu.VMEM((n + 4 * POOL_HALO, POOL_W), F32),
                        pltpu.VMEM((n + 3 * POOL_HALO, POOL_W), F32),
                        pltpu.VMEM((n + 2 * POOL_HALO, POOL_W - GROUP_C), F32),
                        pltpu.VMEM((n + POOL_HALO, POOL_W - 2 * GROUP_C), F32)],
        compiler_params=_params(1),
        name="pool",
    )(up, g, pool_w, pool_scale)


def _fourier_kernel(u_ref, g_ref, fw_ref, cc_ref, sc_ref, cn_ref, sn_ref, o_ref, ua_ref, ub_ref,
                    *, n, chunk):
    for gi in range(MIX_GROUPS):
        cols = slice(gi * GROUP_C, (gi + 1) * GROUP_C)
        w = fw_ref[gi].astype(BF16)
        a = _dot(cc_ref[...], w).astype(BF16)
        b = _dot(sc_ref[...], w).astype(BF16)
        ug = u_ref[:, cols].astype(BF16)
        ua_ref[:, cols] = _dot(ug, a).astype(BF16)
        ub_ref[:, cols] = _dot(ug, b).astype(BF16)
    for r0 in range(0, n, chunk):
        rows = slice(r0, r0 + chunk)
        y = _dot(cn_ref[rows, :], ua_ref[...]) - _dot(sn_ref[rows, :], ub_ref[...])
        o_ref[rows, :] = (y * _silu(g_ref[rows, :])).astype(BF16)


def _fourier(uf, g, fourier_w, tabs, l, *, batch, n):
    cc, sc, cn, sn = tabs
    seq_spec = lambda col: pl.BlockSpec((n, FOURIER_W), lambda b: (b, col))
    return pl.pallas_call(
        functools.partial(_fourier_kernel, n=n, chunk=min(n, 512)),
        grid=(batch,),
        in_specs=[seq_spec(0), seq_spec((ATTN_W + POOL_W) // FOURIER_W),
                  _layer((MIX_GROUPS, GROUP_C, GROUP_C), l),
                  _resident((GROUP_C, GROUP_C)), _resident((GROUP_C, GROUP_C)),
                  _resident((n, n)), _resident((n, n))],
        out_specs=seq_spec(0),
        out_shape=jax.ShapeDtypeStruct((batch * n, FOURIER_W), BF16),
        scratch_shapes=[pltpu.VMEM((n, FOURIER_W), BF16), pltpu.VMEM((n, FOURIER_W), BF16)],
        compiler_params=_params(1),
        name="fourier",
    )(uf, g, fourier_w, cc, sc, cn, sn)


RADIX = 8
COMBINE_ROWS = 16
SQRT_HALF = math.sqrt(0.5)


def _radix8_real(yr, yi):
    sr = [yr[n] + yr[n + 4] for n in range(4)]
    dr = [yr[n] - yr[n + 4] for n in range(4)]
    si = {n: yi[n] + yi[n + 4] for n in (1, 3)}
    di = {n: yi[n] - yi[n + 4] for n in (1, 2, 3)}
    t0r = dr[0]
    t1r, t1i = (dr[1] + di[1]) * SQRT_HALF, (di[1] - dr[1]) * SQRT_HALF
    t2r = di[2]
    t3r, t3i = (di[3] - dr[3]) * SQRT_HALF, (dr[3] + di[3]) * -SQRT_HALF
    out = [None] * RADIX
    ea, eb = sr[0] + sr[2], sr[1] + sr[3]
    ec, ed = sr[0] - sr[2], si[1] - si[3]
    out[0], out[4] = ea + eb, ea - eb
    out[2], out[6] = ec + ed, ec - ed
    oa, ob = t0r + t2r, t1r + t3r
    oc, od = t0r - t2r, t1i - t3i
    out[1], out[5] = oa + ob, oa - ob
    out[3], out[7] = oc + od, oc - od
    return out


def _fourier_radix_kernel(*refs, n1):
    u_refs = refs[:MIX_GROUPS]
    g_ref, fw_ref, cc_ref, sc_ref, cs_ref, tw_ref, o_ref = refs[MIX_GROUPS:MIX_GROUPS + 7]
    scratch = refs[MIX_GROUPS + 7:]
    z_refs, er_refs, ei_refs = (scratch[i * MIX_GROUPS:(i + 1) * MIX_GROUPS] for i in range(3))
    pair = 2 * GROUP_C

    def matmul_steps(gi):
        wab = []

        def channel_part(n2):
            if not wab:
                w = fw_ref[gi].astype(BF16)
                wab.append(jnp.concatenate([_dot(cc_ref[...], w), _dot(sc_ref[...], w)],
                                           axis=1).astype(BF16))
            rows_n2 = u_refs[gi][pl.ds(n2, n1, stride=RADIX), :].astype(BF16)
            z_refs[gi][:, n2 * pair:(n2 + 1) * pair] = _dot(rows_n2, wab[0]).astype(BF16)

        def position_part(n2):
            prod = _dot(cs_ref[...], z_refs[gi][:, n2 * pair:(n2 + 1) * pair])
            er_refs[gi][n2] = prod[:n1, :GROUP_C] - prod[n1:, GROUP_C:]
            ei_refs[gi][n2] = -(prod[n1:, :GROUP_C] + prod[:n1, GROUP_C:])

        return ([functools.partial(channel_part, n2) for n2 in range(RADIX)]
                + [functools.partial(position_part, n2) for n2 in range(RADIX)])

    def combine_steps(gi):
        lanes = slice(gi * GROUP_C, (gi + 1) * GROUP_C)

        def combine(r0):
            rows = slice(r0, r0 + COMBINE_ROWS)
            yr, yi = [er_refs[gi][0, rows, :]], [ei_refs[gi][0, rows, :]]
            for n2 in range(1, RADIX):
                er, ei = er_refs[gi][n2, rows, :], ei_refs[gi][n2, rows, :]
                tc = tw_ref[0, rows, n2 * GROUP_C:(n2 + 1) * GROUP_C]
                ts = tw_ref[1, rows, n2 * GROUP_C:(n2 + 1) * GROUP_C]
                yr.append(tc * er + ts * ei)
                yi.append(tc * ei - ts * er)
            for k2, val in enumerate(_radix8_real(yr, yi)):
                out_rows = slice(k2 * n1 + r0, k2 * n1 + r0 + COMBINE_ROWS)
                o_ref[out_rows, lanes] = (val * _silu(g_ref[out_rows, lanes])).astype(BF16)

        return [functools.partial(combine, r0) for r0 in range(0, n1, COMBINE_ROWS)]

    for step in matmul_steps(0):
        step()
    for gi in range(MIX_GROUPS):
        mxu = matmul_steps(gi + 1) if gi + 1 < MIX_GROUPS else []
        vpu = combine_steps(gi)
        for i in range(max(len(mxu), len(vpu))):
            if i < len(mxu):
                mxu[i]()
            if i < len(vpu):
                vpu[i]()


def _fourier_radix(uf, g, fourier_w, tabs, l, *, batch, n):
    cc, sc, cs, tw = tabs
    n1 = n // RADIX
    seq_spec = lambda col: pl.BlockSpec((n, FOURIER_W), lambda b: (b, col))
    group_specs = [pl.BlockSpec((n, GROUP_C), lambda b, gi=gi: (b, gi)) for gi in range(MIX_GROUPS)]
    return pl.pallas_call(
        functools.partial(_fourier_radix_kernel, n1=n1),
        grid=(batch,),
        in_specs=group_specs + [
            seq_spec((ATTN_W + POOL_W) // FOURIER_W),
            _layer((MIX_GROUPS, GROUP_C, GROUP_C), l),
            _resident((GROUP_C, GROUP_C)), _resident((GROUP_C, GROUP_C)),
            _resident((2 * n1, n1)), _resident((2, n1, RADIX * GROUP_C))],
        out_specs=seq_spec(0),
        out_shape=jax.ShapeDtypeStruct((batch * n, FOURIER_W), BF16),
        scratch_shapes=([pltpu.VMEM((n1, 2 * RADIX * GROUP_C), BF16)] * MIX_GROUPS
                        + [pltpu.VMEM((RADIX, n1, GROUP_C), F32)] * (2 * MIX_GROUPS)),
        compiler_params=_params(1),
        name="fourier_radix",
    )(*([uf] * MIX_GROUPS), g, fourier_w, cc, sc, cs, tw)


def _outproj_kernel(*refs, tiles_per_batch, final):
    if final:
        att_ref, pool_ref, four_ref, w_ref, x_ref, mod_ref, fg_ref, o_ref = refs
    else:
        att_ref, pool_ref, four_ref, w_ref, x_ref, mod_ref, o_ref = refs
    row = CTX_MOD_ROW if tiles_per_batch is None else pl.program_id(0) // tiles_per_batch
    gate = mod_ref[pl.ds(row, 1), 2 * D_MODEL:3 * D_MODEL]
    out = (_dot(att_ref[...], w_ref[0:ATTN_W, :])
           + _dot(pool_ref[...], w_ref[ATTN_W:ATTN_W + POOL_W, :])
           + _dot(four_ref[...], w_ref[ATTN_W + POOL_W:D_MODEL, :]))
    y = x_ref[...] + gate * out
    if final:
        ms = jnp.mean(y * y, axis=-1, keepdims=True)
        y = (y * lax.rsqrt(ms + EPS)) * fg_ref[...]
    o_ref[...] = y


def _outproj(att, pool, four, w_out, x2d, mod, final_g, l, *, tm, rows_per_batch):
    rows = x2d.shape[0]
    final = final_g is not None
    tiles_per_batch = None if rows_per_batch is None else rows_per_batch // tm
    row_spec = lambda w: pl.BlockSpec((tm, w), lambda i: (i, 0))
    in_specs = [row_spec(ATTN_W), row_spec(POOL_W), row_spec(FOURIER_W),
                _layer((D_MODEL, D_MODEL), l), row_spec(D_MODEL), _layer((MOD_ROWS, 3 * D_MODEL), l)]
    args = [att, pool, four, w_out, x2d, mod]
    if final:
        in_specs.append(_resident((1, D_MODEL)))
        args.append(final_g)
    return pl.pallas_call(
        functools.partial(_outproj_kernel, tiles_per_batch=tiles_per_batch, final=final),
        grid=(rows // tm,),
        in_specs=in_specs,
        out_specs=row_spec(D_MODEL),
        out_shape=jax.ShapeDtypeStruct((rows, D_MODEL), F32),
        compiler_params=_params(1),
        name="outproj_final" if final else "outproj",
    )(*args)


def _rope_tables(n):
    pos = jnp.arange(n, dtype=jnp.int32)
    row = (pos // GRID_W).astype(F32)
    col = (pos % GRID_W).astype(F32)
    inv = ROPE_THETA ** (-jnp.arange(0, AXIS_ROT, 2, dtype=F32) / AXIS_ROT)
    ang_r = row[:, None] * inv
    ang_c = col[:, None] * inv
    cos = jnp.concatenate([jnp.cos(ang_r), jnp.cos(ang_r), jnp.cos(ang_c), jnp.cos(ang_c)], axis=1)
    sin = jnp.concatenate([-jnp.sin(ang_r), jnp.sin(ang_r), -jnp.sin(ang_c), jnp.sin(ang_c)], axis=1)
    return cos, sin


def _dft_cos_sin(n):
    k = jnp.arange(n, dtype=jnp.int32)
    ang = ((k[:, None] * k[None, :]) % n).astype(F32) * (2.0 * math.pi / n)
    return jnp.cos(ang), jnp.sin(ang)


def _fourier_tables(n):
    cc, sc = _dft_cos_sin(GROUP_C)
    cn, sn = _dft_cos_sin(n)
    norm = 1.0 / math.sqrt(n * GROUP_C)
    return (cc * norm).astype(BF16), (sc * norm).astype(BF16), cn.astype(BF16), sn.astype(BF16)


def _fourier_radix_tables(n):
    n1 = n // RADIX
    cc, sc = _dft_cos_sin(GROUP_C)
    c1, s1 = _dft_cos_sin(n1)
    norm = 1.0 / math.sqrt(n * GROUP_C)
    k1 = jnp.arange(n1, dtype=jnp.int32)[:, None]
    n2 = jnp.arange(RADIX, dtype=jnp.int32)[None, :]
    ang = (k1 * n2).astype(F32) * (2.0 * math.pi / n)
    twiddle = jnp.stack([jnp.repeat(jnp.cos(ang), GROUP_C, axis=1),
                         jnp.repeat(jnp.sin(ang), GROUP_C, axis=1)])
    return ((cc * norm).astype(BF16), (sc * norm).astype(BF16),
            jnp.concatenate([c1, s1], axis=0).astype(BF16), twiddle)


def kernel(x, c, ctx, c_ctx, ada_w, ada_b, norm_g, w_in, q_norm_g, k_norm_g, pool_w, pool_scale,
           fourier_w, w_out, final_norm_g):
    B, N, _ = x.shape
    C = ctx.shape[1]
    tm = 512
    tm_ctx = 256

    cond = jnp.concatenate([c, c_ctx[None, :], jnp.zeros((MOD_ROWS - B - 1, D_MODEL), F32)], axis=0)
    mod = _adaln(cond, ada_w, ada_b)

    rope_tabs = _rope_tables(N)
    four_tabs = _fourier_radix_tables(N)
    four_tabs_ctx = _fourier_tables(C)
    w_in_b = w_in.astype(BF16)
    w_out_b = w_out.astype(BF16)
    ng = norm_g[:, None, :]
    qg = q_norm_g[:, None, :]
    kg = k_norm_g[:, None, :]
    ps = pool_scale[:, None, :]

    xl = x.reshape(B * N, D_MODEL)
    xc = ctx.reshape(B * C, D_MODEL)
    for l in range(DEPTH):
        last = l == DEPTH - 1
        if last:
            kc, vct = _ctx_kv(xc, mod, ng, w_in_b, kg, l, tm=tm_ctx, seq_len=C)
        else:
            qc, kc, vct, upc, ufc, gc = _inproj(xc, mod, ng, w_in_b, qg, kg, None, l,
                                                tm=tm_ctx, seq_len=C, per_batch_mod=False)
        q, k, vt, up, uf, g = _inproj(xl, mod, ng, w_in_b, qg, kg, rope_tabs, l,
                                      tm=tm, seq_len=N, per_batch_mod=True)

        att = _attention(q, g, [(kc, vct, C), (k, vt, N)], batch=B, q_len=N, tq=256)
        pooled = _pool(up, g, pool_w, ps, l, batch=B, n=N)
        four = _fourier_radix(uf, g, fourier_w, four_tabs, l, batch=B, n=N)
        xl_new = _outproj(att, pooled, four, w_out_b, xl, mod,
                          final_norm_g[None, :] if last else None, l, tm=tm, rows_per_batch=N)

        if not last:
            attc = _attention(qc, gc, [(kc, vct, C)], batch=B, q_len=C, tq=C)
            pooledc = _pool(upc, gc, pool_w, ps, l, batch=B, n=C)
            fourc = _fourier(ufc, gc, fourier_w, four_tabs_ctx, l, batch=B, n=C)
            xc = _outproj(attc, pooledc, fourc, w_out_b, xc, mod, None, l,
                          tm=tm_ctx, rows_per_batch=None)
        xl = xl_new
    return xl.reshape(B, N, D_MODEL)
```

```python
import functools
import math

import jax
import jax.numpy as jnp
from jax import lax
from jax.experimental import pallas as pl
from jax.experimental.pallas import tpu as pltpu

D_MODEL = 2048
DEPTH = 2
GRID_W = 64
HEAD_DIM = 128
ATTN_W = 1024
N_HEADS = 8
N_KV_HEADS = 2
GQA_GROUP = 4
KV_W = 256
POOL_W = 512
POOL_WINDOWS = (2, 4, 8, 16)
FOURIER_W = 512
MIX_GROUPS = 4
GROUP_C = 128
OFF_K = ATTN_W
OFF_V = OFF_K + KV_W
OFF_POOL = OFF_V + KV_W
OFF_FOURIER = OFF_POOL + POOL_W
OFF_GATE = OFF_FOURIER + FOURIER_W
IN_W = OFF_GATE + D_MODEL
ROPE_THETA = 10000.0
AXIS_ROT = HEAD_DIM // 2
EPS = 1e-6

MOD_ROWS = 16
CTX_MOD_ROW = 8
SUBLANES = 8
POOL_HALO = 8
KEY_CHUNK = 256
EXP_CHUNK = 256
ONES_ROWS = 16
V7X_VMEM_BYTES = 64 * 1024 * 1024
VMEM_LIMIT = V7X_VMEM_BYTES - 8 * 1024 * 1024

BF16 = jnp.bfloat16
F32 = jnp.float32


def _silu(x):
    return x * jax.nn.sigmoid(x)


def _dot(a, b):
    return jnp.dot(a, b, preferred_element_type=F32)


def _dot_nt(a, b):
    return lax.dot_general(a, b, (((1,), (1,)), ((), ())), preferred_element_type=F32)


def _resident(shape):
    zeros = (0,) * len(shape)
    return pl.BlockSpec(shape, lambda *_: zeros, pipeline_mode=pl.Buffered(1))


def _layer(shape, l, tail=None):
    index = (l,) + (tail or (0,) * len(shape))
    return pl.BlockSpec((None,) + tuple(shape), lambda *_: index, pipeline_mode=pl.Buffered(1))


def _params(n_grid):
    return pltpu.CompilerParams(dimension_semantics=("arbitrary",) * n_grid,
                                vmem_limit_bytes=VMEM_LIMIT)


def _adaln_kernel(c_ref, w_ref, b_ref, o_ref):
    a = _silu(c_ref[...]).astype(BF16)
    o_ref[...] = _dot(a, w_ref[...].astype(BF16)) + b_ref[...]


def _adaln(cond, ada_w, ada_b):
    tn = 1024
    return pl.pallas_call(
        _adaln_kernel,
        grid=(DEPTH, 3 * D_MODEL // tn),
        in_specs=[
            pl.BlockSpec((MOD_ROWS, D_MODEL), lambda l, j: (0, 0)),
            pl.BlockSpec((None, D_MODEL, tn), lambda l, j: (l, 0, j)),
            pl.BlockSpec((None, 1, tn), lambda l, j: (l, 0, j)),
        ],
        out_specs=pl.BlockSpec((None, MOD_ROWS, tn), lambda l, j: (l, 0, j)),
        out_shape=jax.ShapeDtypeStruct((DEPTH, MOD_ROWS, 3 * D_MODEL), F32),
        compiler_params=_params(2),
        name="adaln",
    )(cond, ada_w, ada_b.reshape(DEPTH, 1, 3 * D_MODEL))


def _modulated_norm(x_ref, mod_ref, ng_ref, row):
    x = x_ref[...]
    ms = jnp.mean(x * x, axis=-1, keepdims=True)
    shift = mod_ref[pl.ds(row, 1), 0:D_MODEL]
    scale = mod_ref[pl.ds(row, 1), D_MODEL:2 * D_MODEL]
    y = (x * lax.rsqrt(ms + EPS)) * ng_ref[...]
    return (y * (1.0 + scale) + shift).astype(BF16)


def _head_norm(p, gain):
    ms = jnp.mean(p * p, axis=-1, keepdims=True)
    return (p * lax.rsqrt(ms + EPS)) * gain


def _rope(y, cos, sin, swap_lo):
    partner = jnp.where(swap_lo, pltpu.roll(y, 96, 1), pltpu.roll(y, 32, 1))
    return y * cos + partner * sin


def _inproj_kernel(*refs, rope, tiles_per_batch):
    if rope:
        (x_ref, mod_ref, ng_ref, w_ref, qg_ref, kg_ref, cos_ref, sin_ref,
         q_ref, k_ref, vt_ref, up_ref, uf_ref, g_ref, h_ref) = refs
    else:
        (x_ref, mod_ref, ng_ref, w_ref, qg_ref, kg_ref,
         q_ref, k_ref, vt_ref, up_ref, uf_ref, g_ref, h_ref) = refs
    tm = x_ref.shape[0]
    row = CTX_MOD_ROW if tiles_per_batch is None else pl.program_id(0) // tiles_per_batch
    h_ref[...] = _modulated_norm(x_ref, mod_ref, ng_ref, row)

    if rope:
        cos = cos_ref[...]
        sin = sin_ref[...]
        lane = lax.broadcasted_iota(jnp.int32, (tm, HEAD_DIM), 1)
        swap_lo = (lane & (AXIS_ROT // 2)) == 0

    def finish_head(p, gain):
        y = _head_norm(p, gain)
        if rope:
            y = _rope(y, cos, sin, swap_lo)
        return y.astype(BF16)

    q_gain = qg_ref[...] * (HEAD_DIM ** -0.5 * math.log2(math.e))
    chunk = GQA_GROUP * HEAD_DIM
    for c0 in range(0, ATTN_W, chunk):
        p = _dot(h_ref[...], w_ref[:, c0:c0 + chunk])
        for hh in range(GQA_GROUP):
            lo = hh * HEAD_DIM
            q_ref[:, c0 + lo:c0 + lo + HEAD_DIM] = finish_head(p[:, lo:lo + HEAD_DIM], q_gain)

    p = _dot(h_ref[...], w_ref[:, OFF_K:OFF_POOL])
    k_gain = kg_ref[...]
    for hh in range(N_KV_HEADS):
        lo = hh * HEAD_DIM
        k_ref[:, lo:lo + HEAD_DIM] = finish_head(p[:, lo:lo + HEAD_DIM], k_gain)
    vt_ref[...] = p[:, KV_W:2 * KV_W].T.astype(BF16)

    up_ref[...] = _dot(h_ref[...], w_ref[:, OFF_POOL:OFF_FOURIER])
    uf_ref[...] = _dot(h_ref[...], w_ref[:, OFF_FOURIER:OFF_GATE])
    for c0 in range(0, D_MODEL, chunk):
        g_ref[:, c0:c0 + chunk] = _dot(h_ref[...], w_ref[:, OFF_GATE + c0:OFF_GATE + c0 + chunk])


def _vt_spec(tm, tiles_per_seq):
    return pl.BlockSpec((None, KV_W, tm), lambda i: (i // tiles_per_seq, 0, i % tiles_per_seq))


def _inproj(x2d, mod, norm_g, w_in, q_g, k_g, rope_tabs, l, *, tm, seq_len, per_batch_mod):
    rows = x2d.shape[0]
    rope = rope_tabs is not None
    tiles_per_seq = seq_len // tm
    row_spec = lambda w: pl.BlockSpec((tm, w), lambda i: (i, 0))
    in_specs = [row_spec(D_MODEL), _layer((MOD_ROWS, 3 * D_MODEL), l), _layer((1, D_MODEL), l),
                _layer((D_MODEL, IN_W), l), _layer((1, HEAD_DIM), l), _layer((1, HEAD_DIM), l)]
    args = [x2d, mod, norm_g, w_in, q_g, k_g]
    if rope:
        tab_spec = pl.BlockSpec((tm, HEAD_DIM), lambda i: (i % tiles_per_seq, 0))
        in_specs += [tab_spec, tab_spec]
        args += list(rope_tabs)
    row_out = lambda w, dt: jax.ShapeDtypeStruct((rows, w), dt)
    return pl.pallas_call(
        functools.partial(_inproj_kernel, rope=rope,
                          tiles_per_batch=tiles_per_seq if per_batch_mod else None),
        grid=(rows // tm,),
        in_specs=in_specs,
        out_specs=[row_spec(ATTN_W), row_spec(KV_W), _vt_spec(tm, tiles_per_seq),
                   row_spec(POOL_W), row_spec(FOURIER_W), row_spec(D_MODEL)],
        out_shape=[row_out(ATTN_W, BF16), row_out(KV_W, BF16),
                   jax.ShapeDtypeStruct((rows // seq_len, KV_W, seq_len), BF16),
                   row_out(POOL_W, F32), row_out(FOURIER_W, F32), row_out(D_MODEL, F32)],
        scratch_shapes=[pltpu.VMEM((tm, D_MODEL), BF16)],
        compiler_params=_params(1),
        name="inproj_rope" if rope else "inproj_ctx",
    )(*args)


def _ctx_kv_kernel(x_ref, mod_ref, ng_ref, w_ref, kg_ref, k_ref, vt_ref):
    h = _modulated_norm(x_ref, mod_ref, ng_ref, CTX_MOD_ROW)
    p = _dot(h, w_ref[...])
    k_gain = kg_ref[...]
    for hh in range(N_KV_HEADS):
        lo = hh * HEAD_DIM
        k_ref[:, lo:lo + HEAD_DIM] = _head_norm(p[:, lo:lo + HEAD_DIM], k_gain).astype(BF16)
    vt_ref[...] = p[:, KV_W:2 * KV_W].T.astype(BF16)


def _ctx_kv(x2d, mod, norm_g, w_in, k_g, l, *, tm, seq_len):
    rows = x2d.shape[0]
    row_spec = lambda w: pl.BlockSpec((tm, w), lambda i: (i, 0))
    kv_cols = (0, OFF_K // (2 * KV_W))
    return pl.pallas_call(
        _ctx_kv_kernel,
        grid=(rows // tm,),
        in_specs=[row_spec(D_MODEL), _layer((MOD_ROWS, 3 * D_MODEL), l), _layer((1, D_MODEL), l),
                  _layer((D_MODEL, 2 * KV_W), l, kv_cols), _layer((1, HEAD_DIM), l)],
        out_specs=[row_spec(KV_W), _vt_spec(tm, seq_len // tm)],
        out_shape=[jax.ShapeDtypeStruct((rows, KV_W), BF16),
                   jax.ShapeDtypeStruct((rows // seq_len, KV_W, seq_len), BF16)],
        compiler_params=_params(1),
        name="ctx_kv",
    )(x2d, mod, norm_g, w_in, k_g)


def _attn_kernel(*refs, lengths, tq):
    n_sources = len(lengths)
    q_ref, g_ref = refs[0], refs[1]
    k_refs = refs[2:2 + 2 * n_sources:2]
    vt_refs = refs[3:3 + 2 * n_sources:2]
    o_ref = refs[2 + 2 * n_sources]
    s_ref, p_ref, vta_ref = refs[3 + 2 * n_sources:]
    offsets = [sum(lengths[:i]) for i in range(n_sources)]
    total = sum(lengths)
    n_tiles = q_ref.shape[0] // tq
    key_chunks = [(k_ref, off, r0) for k_ref, off, length in zip(k_refs, offsets, lengths)
                  for r0 in range(0, length, KEY_CHUNK)]
    head_cols = [slice(hh * HEAD_DIM, (hh + 1) * HEAD_DIM) for hh in range(GQA_GROUP)]

    def q_rows(tile):
        return pl.ds(pl.multiple_of(tile * tq, tq), tq)

    def fold_rows(x, op):
        return op(x.reshape(x.shape[0] // SUBLANES, SUBLANES, tq), axis=0)

    def score_chunk(tile, hh, slot, chunk, col_max):
        k_ref, off, r0 = key_chunks[chunk]
        s = _dot_nt(k_ref[r0:r0 + KEY_CHUNK, :], q_ref[q_rows(tile), head_cols[hh]])
        s_ref[slot, off + r0:off + r0 + KEY_CHUNK, :] = s
        cm = fold_rows(s, jnp.max)
        return cm if col_max is None else jnp.maximum(col_max, cm)

    def prob_chunk(slot, chunk, col_max):
        for r0 in range(chunk * KEY_CHUNK, (chunk + 1) * KEY_CHUNK, EXP_CHUNK):
            p = jnp.exp2(s_ref[slot, r0:r0 + EXP_CHUNK, :] - col_max)
            p_ref[slot, r0:r0 + EXP_CHUNK, :] = p.astype(BF16)

    def finish(tile, hh, slot):
        acc = _dot(vta_ref[...], p_ref[slot])
        out = (acc[:HEAD_DIM] / acc[HEAD_DIM:HEAD_DIM + 1]).T
        gate = _silu(g_ref[q_rows(tile), head_cols[hh]])
        o_ref[q_rows(tile), head_cols[hh]] = (out * gate).astype(BF16)

    for vt_ref, off, length in zip(vt_refs, offsets, lengths):
        vta_ref[0:HEAD_DIM, off:off + length] = vt_ref[...]
    vta_ref[HEAD_DIM:HEAD_DIM + ONES_ROWS, :] = jnp.ones((ONES_ROWS, total), BF16)

    n_chunks = total // KEY_CHUNK
    first_max = None
    for chunk in range(n_chunks):
        first_max = score_chunk(0, 0, 0, chunk, first_max)
    first_max = jnp.max(first_max, axis=0, keepdims=True)

    def tile_body(tile, cur_max):
        for hh in range(GQA_GROUP):
            slot, next_slot = hh % 2, (hh + 1) % 2
            next_hh = (hh + 1) % GQA_GROUP
            next_tile = tile if next_hh else jnp.minimum(tile + 1, n_tiles - 1)
            next_max = None
            for chunk in range(n_chunks):
                next_max = score_chunk(next_tile, next_hh, next_slot, chunk, next_max)
                prob_chunk(slot, chunk, cur_max)
            finish(tile, hh, slot)
            cur_max = jnp.max(next_max, axis=0, keepdims=True)
        return cur_max

    lax.fori_loop(0, n_tiles, tile_body, first_max)


def _attention(q, g, sources, *, batch, q_len, tq):
    q_spec = pl.BlockSpec((q_len, GQA_GROUP * HEAD_DIM), lambda b, h: (b, h))
    in_specs = [q_spec, q_spec]
    args = [q, g]
    for k, vt, length in sources:
        in_specs += [pl.BlockSpec((length, HEAD_DIM), lambda b, h: (b, h)),
                     pl.BlockSpec((None, HEAD_DIM, length), lambda b, h: (b, h, 0))]
        args += [k, vt]
    lengths = tuple(length for _, _, length in sources)
    total = sum(lengths)
    return pl.pallas_call(
        functools.partial(_attn_kernel, lengths=lengths, tq=tq),
        grid=(batch, N_KV_HEADS),
        in_specs=in_specs,
        out_specs=q_spec,
        out_shape=jax.ShapeDtypeStruct((batch * q_len, ATTN_W), BF16),
        scratch_shapes=[pltpu.VMEM((2, total, tq), F32), pltpu.VMEM((2, total, tq), BF16),
                        pltpu.VMEM((HEAD_DIM + ONES_ROWS, total), BF16)],
        compiler_params=_params(2),
        name="attention",
    )(*args)


POOL_CHUNK = 256


def _pool_tile(up_ref, prev_ref, next_ref, g_ref, pw_ref, ps_ref, dst_ref,
               pad_ref, s2_ref, s4_ref, s8_ref, *, tile_in_seq, tiles_per_seq, seq_len):
    tm = up_ref.shape[0]
    halo = POOL_HALO
    chunk = min(tm, POOL_CHUNK)
    pad_ref[0:halo, :] = jnp.where(tile_in_seq > 0, prev_ref[...], 0.0)
    pad_ref[halo:halo + tm, :] = up_ref[...]
    pad_ref[halo + tm:2 * halo + tm, :] = jnp.where(tile_in_seq < tiles_per_seq - 1,
                                                    next_ref[...], 0.0)
    pad_ref[2 * halo + tm:4 * halo + tm, :] = jnp.zeros((2 * halo, POOL_W), F32)

    def forward_sum(dst, src, shift, length, lane0):
        for r0 in range(0, length, chunk):
            rows = min(chunk, length - r0)
            dst[r0:r0 + rows, :] = (src[r0:r0 + rows, lane0:]
                                    + src[r0 + shift:r0 + shift + rows, lane0:])

    forward_sum(s2_ref, pad_ref, 1, tm + 3 * halo, 0)
    forward_sum(s4_ref, s2_ref, 2, tm + 2 * halo, GROUP_C)
    forward_sum(s8_ref, s4_ref, 4, tm + halo, GROUP_C)

    def window_sum(gi, r0):
        if gi == 0:
            return s2_ref[halo - 1 + r0:halo - 1 + r0 + chunk, 0:GROUP_C]
        if gi == 1:
            return s4_ref[halo - 2 + r0:halo - 2 + r0 + chunk, 0:GROUP_C]
        if gi == 2:
            return s8_ref[halo - 4 + r0:halo - 4 + r0 + chunk, 0:GROUP_C]
        return (s8_ref[r0:r0 + chunk, GROUP_C:] + s8_ref[halo + r0:halo + r0 + chunk, GROUP_C:])

    def edge_inverse_count(win, local_row):
        before, after = win // 2, win - win // 2 - 1
        t = tile_in_seq * tm + local_row + lax.broadcasted_iota(jnp.int32, (halo, GROUP_C), 0)
        cnt = jnp.minimum(t + after, seq_len - 1) - jnp.maximum(t - before, 0) + 1
        return 1.0 / cnt.astype(F32)

    for gi, win in enumerate(POOL_WINDOWS):
        cols = slice(gi * GROUP_C, (gi + 1) * GROUP_C)
        w = pw_ref[gi].astype(BF16)
        for r0 in range(0, tm, chunk):
            pieces, inner = [], chunk
            if r0 == 0:
                pieces.append(edge_inverse_count(win, 0))
                inner -= halo
            tail = [edge_inverse_count(win, tm - halo)] if r0 + chunk == tm else []
            inner -= halo * len(tail)
            pieces.append(jnp.full((inner, GROUP_C), 1.0 / win, F32))
            inv_cnt = jnp.concatenate(pieces + tail, axis=0)
            pooled = window_sum(gi, r0) * inv_cnt - pad_ref[halo + r0:halo + r0 + chunk, cols]
            y = _dot(pooled.astype(BF16), w) * ps_ref[:, cols]
            dst_ref[r0:r0 + chunk, cols] = (y * _silu(g_ref[r0:r0 + chunk, cols])).astype(BF16)


def _pool_scratch(tm):
    return [pltpu.VMEM((tm + 4 * POOL_HALO, POOL_W), F32),
            pltpu.VMEM((tm + 3 * POOL_HALO, POOL_W), F32),
            pltpu.VMEM((tm + 2 * POOL_HALO, POOL_W - GROUP_C), F32),
            pltpu.VMEM((tm + POOL_HALO, POOL_W - 2 * GROUP_C), F32),
            pltpu.VMEM((tm, POOL_W), BF16)]


def _fourier_kernel(u_ref, g_ref, fw_ref, cc_ref, sc_ref, cn_ref, sn_ref, o_ref, ua_ref, ub_ref,
                    *, n, chunk):
    for gi in range(MIX_GROUPS):
        cols = slice(gi * GROUP_C, (gi + 1) * GROUP_C)
        w = fw_ref[gi].astype(BF16)
        a = _dot(cc_ref[...], w).astype(BF16)
        b = _dot(sc_ref[...], w).astype(BF16)
        ug = u_ref[:, cols].astype(BF16)
        ua_ref[:, cols] = _dot(ug, a).astype(BF16)
        ub_ref[:, cols] = _dot(ug, b).astype(BF16)
    for r0 in range(0, n, chunk):
        rows = slice(r0, r0 + chunk)
        y = _dot(cn_ref[rows, :], ua_ref[...]) - _dot(sn_ref[rows, :], ub_ref[...])
        o_ref[rows, :] = (y * _silu(g_ref[rows, :])).astype(BF16)


def _fourier(uf, g, fourier_w, tabs, l, *, batch, n):
    cc, sc, cn, sn = tabs
    seq_spec = lambda col: pl.BlockSpec((n, FOURIER_W), lambda b: (b, col))
    return pl.pallas_call(
        functools.partial(_fourier_kernel, n=n, chunk=min(n, 512)),
        grid=(batch,),
        in_specs=[seq_spec(0), seq_spec((ATTN_W + POOL_W) // FOURIER_W),
                  _layer((MIX_GROUPS, GROUP_C, GROUP_C), l),
                  _resident((GROUP_C, GROUP_C)), _resident((GROUP_C, GROUP_C)),
                  _resident((n, n)), _resident((n, n))],
        out_specs=seq_spec(0),
        out_shape=jax.ShapeDtypeStruct((batch * n, FOURIER_W), BF16),
        scratch_shapes=[pltpu.VMEM((n, FOURIER_W), BF16), pltpu.VMEM((n, FOURIER_W), BF16)],
        compiler_params=_params(1),
        name="fourier",
    )(uf, g, fourier_w, cc, sc, cn, sn)


RADIX = 8
COMBINE_ROWS = 16
SQRT_HALF = math.sqrt(0.5)


def _radix8_real(yr, yi):
    sr = [yr[n] + yr[n + 4] for n in range(4)]
    dr = [yr[n] - yr[n + 4] for n in range(4)]
    si = {n: yi[n] + yi[n + 4] for n in (1, 3)}
    di = {n: yi[n] - yi[n + 4] for n in (1, 2, 3)}
    t0r = dr[0]
    t1r, t1i = (dr[1] + di[1]) * SQRT_HALF, (di[1] - dr[1]) * SQRT_HALF
    t2r = di[2]
    t3r, t3i = (di[3] - dr[3]) * SQRT_HALF, (dr[3] + di[3]) * -SQRT_HALF
    out = [None] * RADIX
    ea, eb = sr[0] + sr[2], sr[1] + sr[3]
    ec, ed = sr[0] - sr[2], si[1] - si[3]
    out[0], out[4] = ea + eb, ea - eb
    out[2], out[6] = ec + ed, ec - ed
    oa, ob = t0r + t2r, t1r + t3r
    oc, od = t0r - t2r, t1i - t3i
    out[1], out[5] = oa + ob, oa - ob
    out[3], out[7] = oc + od, oc - od
    return out


def _fourier_radix_kernel(*refs, n1):
    u_refs = refs[:MIX_GROUPS]
    g_ref, fw_ref, cc_ref, sc_ref, cs_ref, tw_ref, o_ref = refs[MIX_GROUPS:MIX_GROUPS + 7]
    scratch = refs[MIX_GROUPS + 7:]
    z_refs, er_refs, ei_refs = (scratch[i * MIX_GROUPS:(i + 1) * MIX_GROUPS] for i in range(3))
    pair = 2 * GROUP_C

    def matmul_steps(gi):
        wab = []

        def channel_part(n2):
            if not wab:
                w = fw_ref[gi].astype(BF16)
                wab.append(jnp.concatenate([_dot(cc_ref[...], w), _dot(sc_ref[...], w)],
                                           axis=1).astype(BF16))
            rows_n2 = u_refs[gi][pl.ds(n2, n1, stride=RADIX), :].astype(BF16)
            z_refs[gi][:, n2 * pair:(n2 + 1) * pair] = _dot(rows_n2, wab[0]).astype(BF16)

        def position_part(n2):
            prod = _dot(cs_ref[...], z_refs[gi][:, n2 * pair:(n2 + 1) * pair])
            er_refs[gi][n2] = prod[:n1, :GROUP_C] - prod[n1:, GROUP_C:]
            ei_refs[gi][n2] = -(prod[n1:, :GROUP_C] + prod[:n1, GROUP_C:])

        return ([functools.partial(channel_part, n2) for n2 in range(RADIX)]
                + [functools.partial(position_part, n2) for n2 in range(RADIX)])

    def combine_steps(gi):
        lanes = slice(gi * GROUP_C, (gi + 1) * GROUP_C)

        def combine(r0):
            rows = slice(r0, r0 + COMBINE_ROWS)
            yr, yi = [er_refs[gi][0, rows, :]], [ei_refs[gi][0, rows, :]]
            for n2 in range(1, RADIX):
                er, ei = er_refs[gi][n2, rows, :], ei_refs[gi][n2, rows, :]
                tc = tw_ref[0, rows, n2 * GROUP_C:(n2 + 1) * GROUP_C]
                ts = tw_ref[1, rows, n2 * GROUP_C:(n2 + 1) * GROUP_C]
                yr.append(tc * er + ts * ei)
                yi.append(tc * ei - ts * er)
            for k2, val in enumerate(_radix8_real(yr, yi)):
                out_rows = slice(k2 * n1 + r0, k2 * n1 + r0 + COMBINE_ROWS)
                o_ref[out_rows, lanes] = (val * _silu(g_ref[out_rows, lanes])).astype(BF16)

        return [functools.partial(combine, r0) for r0 in range(0, n1, COMBINE_ROWS)]

    for step in matmul_steps(0):
        step()
    for gi in range(MIX_GROUPS):
        mxu = matmul_steps(gi + 1) if gi + 1 < MIX_GROUPS else []
        vpu = combine_steps(gi)
        for i in range(max(len(mxu), len(vpu))):
            if i < len(mxu):
                mxu[i]()
            if i < len(vpu):
                vpu[i]()


def _fourier_radix(uf, g, fourier_w, tabs, l, *, batch, n):
    cc, sc, cs, tw = tabs
    n1 = n // RADIX
    seq_spec = lambda col: pl.BlockSpec((n, FOURIER_W), lambda b: (b, col))
    group_specs = [pl.BlockSpec((n, GROUP_C), lambda b, gi=gi: (b, gi)) for gi in range(MIX_GROUPS)]
    return pl.pallas_call(
        functools.partial(_fourier_radix_kernel, n1=n1),
        grid=(batch,),
        in_specs=group_specs + [
            seq_spec((ATTN_W + POOL_W) // FOURIER_W),
            _layer((MIX_GROUPS, GROUP_C, GROUP_C), l),
            _resident((GROUP_C, GROUP_C)), _resident((GROUP_C, GROUP_C)),
            _resident((2 * n1, n1)), _resident((2, n1, RADIX * GROUP_C))],
        out_specs=seq_spec(0),
        out_shape=jax.ShapeDtypeStruct((batch * n, FOURIER_W), BF16),
        scratch_shapes=([pltpu.VMEM((n1, 2 * RADIX * GROUP_C), BF16)] * MIX_GROUPS
                        + [pltpu.VMEM((RADIX, n1, GROUP_C), F32)] * (2 * MIX_GROUPS)),
        compiler_params=_params(1),
        name="fourier_radix",
    )(*([uf] * MIX_GROUPS), g, fourier_w, cc, sc, cs, tw)


def _outproj_kernel(*refs, tiles_per_seq, seq_len, per_batch_mod, final):
    (att_ref, up_ref, prev_ref, next_ref, gp_ref, four_ref, w_ref, pw_ref, ps_ref,
     x_ref, mod_ref) = refs[:11]
    fg_ref = refs[11] if final else None
    o_ref, pad_ref, s2_ref, s4_ref, s8_ref, pooled_ref = refs[-6:]
    step = pl.program_id(0)
    _pool_tile(up_ref, prev_ref, next_ref, gp_ref, pw_ref, ps_ref, pooled_ref,
               pad_ref, s2_ref, s4_ref, s8_ref, tile_in_seq=step % tiles_per_seq,
               tiles_per_seq=tiles_per_seq, seq_len=seq_len)
    row = step // tiles_per_seq if per_batch_mod else CTX_MOD_ROW
    gate = mod_ref[pl.ds(row, 1), 2 * D_MODEL:3 * D_MODEL]
    out = (_dot(att_ref[...], w_ref[0:ATTN_W, :])
           + _dot(four_ref[...], w_ref[ATTN_W + POOL_W:D_MODEL, :])
           + _dot(pooled_ref[...], w_ref[ATTN_W:ATTN_W + POOL_W, :]))
    y = x_ref[...] + gate * out
    if final:
        ms = jnp.mean(y * y, axis=-1, keepdims=True)
        y = (y * lax.rsqrt(ms + EPS)) * fg_ref[...]
    o_ref[...] = y


def _outproj(att, up, g, four, w_out, pool_w, pool_scale, x2d, mod, final_g, l,
             *, tm, seq_len, per_batch_mod):
    rows = x2d.shape[0]
    final = final_g is not None
    tiles_per_seq = seq_len // tm
    halo_blocks = tm // POOL_HALO
    row_spec = lambda w: pl.BlockSpec((tm, w), lambda i: (i, 0))
    halo_spec = lambda index: pl.BlockSpec((POOL_HALO, POOL_W), lambda i: (index(i), 0))
    in_specs = [row_spec(ATTN_W), row_spec(POOL_W),
                halo_spec(lambda i: jnp.maximum(i * halo_blocks - 1, 0)),
                halo_spec(lambda i: jnp.minimum((i + 1) * halo_blocks, rows // POOL_HALO - 1)),
                pl.BlockSpec((tm, POOL_W), lambda i: (i, ATTN_W // POOL_W)), row_spec(FOURIER_W),
                _layer((D_MODEL, D_MODEL), l), _layer((MIX_GROUPS, GROUP_C, GROUP_C), l),
                _layer((1, POOL_W), l), row_spec(D_MODEL), _layer((MOD_ROWS, 3 * D_MODEL), l)]
    args = [att, up, up, up, g, four, w_out, pool_w, pool_scale, x2d, mod]
    if final:
        in_specs.append(_resident((1, D_MODEL)))
        args.append(final_g)
    return pl.pallas_call(
        functools.partial(_outproj_kernel, tiles_per_seq=tiles_per_seq, seq_len=seq_len,
                          per_batch_mod=per_batch_mod, final=final),
        grid=(rows // tm,),
        in_specs=in_specs,
        out_specs=row_spec(D_MODEL),
        out_shape=jax.ShapeDtypeStruct((rows, D_MODEL), F32),
        scratch_shapes=_pool_scratch(tm),
        compiler_params=_params(1),
        name="outproj_final" if final else "outproj",
    )(*args)


def _rope_tables(n):
    pos = jnp.arange(n, dtype=jnp.int32)
    row = (pos // GRID_W).astype(F32)
    col = (pos % GRID_W).astype(F32)
    inv = ROPE_THETA ** (-jnp.arange(0, AXIS_ROT, 2, dtype=F32) / AXIS_ROT)
    ang_r = row[:, None] * inv
    ang_c = col[:, None] * inv
    cos = jnp.concatenate([jnp.cos(ang_r), jnp.cos(ang_r), jnp.cos(ang_c), jnp.cos(ang_c)], axis=1)
    sin = jnp.concatenate([-jnp.sin(ang_r), jnp.sin(ang_r), -jnp.sin(ang_c), jnp.sin(ang_c)], axis=1)
    return cos, sin


def _dft_cos_sin(n):
    k = jnp.arange(n, dtype=jnp.int32)
    ang = ((k[:, None] * k[None, :]) % n).astype(F32) * (2.0 * math.pi / n)
    return jnp.cos(ang), jnp.sin(ang)


def _fourier_tables(n):
    cc, sc = _dft_cos_sin(GROUP_C)
    cn, sn = _dft_cos_sin(n)
    norm = 1.0 / math.sqrt(n * GROUP_C)
    return (cc * norm).astype(BF16), (sc * norm).astype(BF16), cn.astype(BF16), sn.astype(BF16)


def _fourier_radix_tables(n):
    n1 = n // RADIX
    cc, sc = _dft_cos_sin(GROUP_C)
    c1, s1 = _dft_cos_sin(n1)
    norm = 1.0 / math.sqrt(n * GROUP_C)
    k1 = jnp.arange(n1, dtype=jnp.int32)[:, None]
    n2 = jnp.arange(RADIX, dtype=jnp.int32)[None, :]
    ang = (k1 * n2).astype(F32) * (2.0 * math.pi / n)
    twiddle = jnp.stack([jnp.repeat(jnp.cos(ang), GROUP_C, axis=1),
                         jnp.repeat(jnp.sin(ang), GROUP_C, axis=1)])
    return ((cc * norm).astype(BF16), (sc * norm).astype(BF16),
            jnp.concatenate([c1, s1], axis=0).astype(BF16), twiddle)


def kernel(x, c, ctx, c_ctx, ada_w, ada_b, norm_g, w_in, q_norm_g, k_norm_g, pool_w, pool_scale,
           fourier_w, w_out, final_norm_g):
    B, N, _ = x.shape
    C = ctx.shape[1]
    tm = 512
    tm_ctx = 256

    cond = jnp.concatenate([c, c_ctx[None, :], jnp.zeros((MOD_ROWS - B - 1, D_MODEL), F32)], axis=0)
    mod = _adaln(cond, ada_w, ada_b)

    rope_tabs = _rope_tables(N)
    four_tabs = _fourier_radix_tables(N)
    four_tabs_ctx = _fourier_tables(C)
    w_in_b = w_in.astype(BF16)
    w_out_b = w_out.astype(BF16)
    ng = norm_g[:, None, :]
    qg = q_norm_g[:, None, :]
    kg = k_norm_g[:, None, :]
    ps = pool_scale[:, None, :]

    xl = x.reshape(B * N, D_MODEL)
    xc = ctx.reshape(B * C, D_MODEL)
    for l in range(DEPTH):
        last = l == DEPTH - 1
        if last:
            kc, vct = _ctx_kv(xc, mod, ng, w_in_b, kg, l, tm=tm_ctx, seq_len=C)
        else:
            qc, kc, vct, upc, ufc, gc = _inproj(xc, mod, ng, w_in_b, qg, kg, None, l,
                                                tm=tm_ctx, seq_len=C, per_batch_mod=False)
        q, k, vt, up, uf, g = _inproj(xl, mod, ng, w_in_b, qg, kg, rope_tabs, l,
                                      tm=tm, seq_len=N, per_batch_mod=True)

        att = _attention(q, g, [(kc, vct, C), (k, vt, N)], batch=B, q_len=N, tq=256)
        four = _fourier_radix(uf, g, fourier_w, four_tabs, l, batch=B, n=N)
        xl_new = _outproj(att, up, g, four, w_out_b, pool_w, ps, xl, mod,
                          final_norm_g[None, :] if last else None, l,
                          tm=tm, seq_len=N, per_batch_mod=True)

        if not last:
            attc = _attention(qc, gc, [(kc, vct, C)], batch=B, q_len=C, tq=C)
            fourc = _fourier(ufc, gc, fourier_w, four_tabs_ctx, l, batch=B, n=C)
            xc = _outproj(attc, upc, gc, fourc, w_out_b, pool_w, ps, xc, mod, None, l,
                          tm=tm_ctx, seq_len=C, per_batch_mod=False)
        xl = xl_new
    return xl.reshape(B, N, D_MODEL)
```

```python
import functools
import math

import jax
import jax.numpy as jnp
from jax import lax
from jax.experimental import pallas as pl
from jax.experimental.pallas import tpu as pltpu

D_MODEL = 2048
DEPTH = 2
GRID_W = 64
HEAD_DIM = 128
ATTN_W = 1024
N_HEADS = 8
N_KV_HEADS = 2
GQA_GROUP = 4
KV_W = 256
POOL_W = 512
POOL_WINDOWS = (2, 4, 8, 16)
FOURIER_W = 512
MIX_GROUPS = 4
GROUP_C = 128
OFF_K = ATTN_W
OFF_V = OFF_K + KV_W
OFF_POOL = OFF_V + KV_W
OFF_FOURIER = OFF_POOL + POOL_W
OFF_GATE = OFF_FOURIER + FOURIER_W
IN_W = OFF_GATE + D_MODEL
ROPE_THETA = 10000.0
AXIS_ROT = HEAD_DIM // 2
EPS = 1e-6

MOD_ROWS = 16
CTX_MOD_ROW = 8
SUBLANES = 8
POOL_HALO = 8
KEY_CHUNK = 256
EXP_CHUNK = 256
ONES_ROWS = 16
V7X_VMEM_BYTES = 64 * 1024 * 1024
VMEM_LIMIT = V7X_VMEM_BYTES - 8 * 1024 * 1024

BF16 = jnp.bfloat16
F32 = jnp.float32


def _silu(x):
    return x * jax.nn.sigmoid(x)


def _dot(a, b):
    return jnp.dot(a, b, preferred_element_type=F32)


def _dot_nt(a, b):
    return lax.dot_general(a, b, (((1,), (1,)), ((), ())), preferred_element_type=F32)


def _resident(shape):
    zeros = (0,) * len(shape)
    return pl.BlockSpec(shape, lambda *_: zeros, pipeline_mode=pl.Buffered(1))


def _layer(shape, l, tail=None):
    index = (l,) + (tail or (0,) * len(shape))
    return pl.BlockSpec((None,) + tuple(shape), lambda *_: index, pipeline_mode=pl.Buffered(1))


def _params(n_grid):
    return pltpu.CompilerParams(dimension_semantics=("arbitrary",) * n_grid,
                                vmem_limit_bytes=VMEM_LIMIT)


def _adaln_kernel(c_ref, w_ref, b_ref, o_ref):
    a = _silu(c_ref[...]).astype(BF16)
    o_ref[...] = _dot(a, w_ref[...].astype(BF16)) + b_ref[...]


def _adaln(cond, ada_w, ada_b):
    tn = 1024
    return pl.pallas_call(
        _adaln_kernel,
        grid=(DEPTH, 3 * D_MODEL // tn),
        in_specs=[
            pl.BlockSpec((MOD_ROWS, D_MODEL), lambda l, j: (0, 0)),
            pl.BlockSpec((None, D_MODEL, tn), lambda l, j: (l, 0, j)),
            pl.BlockSpec((None, 1, tn), lambda l, j: (l, 0, j)),
        ],
        out_specs=pl.BlockSpec((None, MOD_ROWS, tn), lambda l, j: (l, 0, j)),
        out_shape=jax.ShapeDtypeStruct((DEPTH, MOD_ROWS, 3 * D_MODEL), F32),
        compiler_params=_params(2),
        name="adaln",
    )(cond, ada_w, ada_b.reshape(DEPTH, 1, 3 * D_MODEL))


def _modulated_norm(x_ref, mod_ref, ng_ref, row):
    x = x_ref[...]
    ms = jnp.mean(x * x, axis=-1, keepdims=True)
    shift = mod_ref[pl.ds(row, 1), 0:D_MODEL]
    scale = mod_ref[pl.ds(row, 1), D_MODEL:2 * D_MODEL]
    y = (x * lax.rsqrt(ms + EPS)) * ng_ref[...]
    return (y * (1.0 + scale) + shift).astype(BF16)


def _head_norm(p, gain):
    ms = jnp.mean(p * p, axis=-1, keepdims=True)
    return (p * lax.rsqrt(ms + EPS)) * gain


def _rope(y, cos, sin, swap_lo):
    partner = jnp.where(swap_lo, pltpu.roll(y, 96, 1), pltpu.roll(y, 32, 1))
    return y * cos + partner * sin


def _inproj_kernel(*refs, rope, tiles_per_batch, n_convert):
    n_in = 8 if rope else 6
    x_ref, mod_ref, ng_ref, w_ref, qg_ref, kg_ref = refs[:6]
    cos_ref, sin_ref = refs[6:8] if rope else (None, None)
    f32_weight_refs = refs[n_in:n_in + n_convert]
    q_ref, k_ref, vt_ref, up_ref, uf_ref, g_ref = refs[n_in + n_convert:n_in + n_convert + 6]
    bf16_weight_refs = refs[n_in + n_convert + 6:n_in + 2 * n_convert + 6]
    h_ref = refs[-1]
    for src_ref, dst_ref in zip(f32_weight_refs, bf16_weight_refs):
        dst_ref[...] = src_ref[...].astype(BF16)
    tm = x_ref.shape[0]
    row = CTX_MOD_ROW if tiles_per_batch is None else pl.program_id(0) // tiles_per_batch
    h_ref[...] = _modulated_norm(x_ref, mod_ref, ng_ref, row)

    if rope:
        cos = cos_ref[...]
        sin = sin_ref[...]
        lane = lax.broadcasted_iota(jnp.int32, (tm, HEAD_DIM), 1)
        swap_lo = (lane & (AXIS_ROT // 2)) == 0

    def finish_head(p, gain):
        y = _head_norm(p, gain)
        if rope:
            y = _rope(y, cos, sin, swap_lo)
        return y.astype(BF16)

    q_gain = qg_ref[...] * (HEAD_DIM ** -0.5 * math.log2(math.e))
    chunk = GQA_GROUP * HEAD_DIM
    for c0 in range(0, ATTN_W, chunk):
        p = _dot(h_ref[...], w_ref[:, c0:c0 + chunk])
        for hh in range(GQA_GROUP):
            lo = hh * HEAD_DIM
            q_ref[:, c0 + lo:c0 + lo + HEAD_DIM] = finish_head(p[:, lo:lo + HEAD_DIM], q_gain)

    p = _dot(h_ref[...], w_ref[:, OFF_K:OFF_POOL])
    k_gain = kg_ref[...]
    for hh in range(N_KV_HEADS):
        lo = hh * HEAD_DIM
        k_ref[:, lo:lo + HEAD_DIM] = finish_head(p[:, lo:lo + HEAD_DIM], k_gain)
    vt_ref[...] = p[:, KV_W:2 * KV_W].T.astype(BF16)

    up_ref[...] = _dot(h_ref[...], w_ref[:, OFF_POOL:OFF_FOURIER])
    uf_ref[...] = _dot(h_ref[...], w_ref[:, OFF_FOURIER:OFF_GATE])
    for c0 in range(0, D_MODEL, chunk):
        g_ref[:, c0:c0 + chunk] = _dot(h_ref[...], w_ref[:, OFF_GATE + c0:OFF_GATE + c0 + chunk])


def _vt_spec(tm, tiles_per_seq):
    return pl.BlockSpec((None, KV_W, tm), lambda i: (i // tiles_per_seq, 0, i % tiles_per_seq))


def _inproj(x2d, mod, norm_g, w_in, q_g, k_g, rope_tabs, l, *, tm, seq_len, per_batch_mod,
            cast_next=()):
    rows = x2d.shape[0]
    steps = rows // tm
    rope = rope_tabs is not None
    tiles_per_seq = seq_len // tm
    row_spec = lambda w: pl.BlockSpec((tm, w), lambda i: (i, 0))
    in_specs = [row_spec(D_MODEL), _layer((MOD_ROWS, 3 * D_MODEL), l), _layer((1, D_MODEL), l),
                _layer((D_MODEL, IN_W), 0), _layer((1, HEAD_DIM), l), _layer((1, HEAD_DIM), l)]
    args = [x2d, mod, norm_g, w_in, q_g, k_g]
    if rope:
        tab_spec = pl.BlockSpec((tm, HEAD_DIM), lambda i: (i % tiles_per_seq, 0))
        in_specs += [tab_spec, tab_spec]
        args += list(rope_tabs)
    row_out = lambda w, dt: jax.ShapeDtypeStruct((rows, w), dt)
    out_specs = [row_spec(ATTN_W), row_spec(KV_W), _vt_spec(tm, tiles_per_seq),
                 row_spec(POOL_W), row_spec(FOURIER_W), row_spec(D_MODEL)]
    out_shape = [row_out(ATTN_W, BF16), row_out(KV_W, BF16),
                 jax.ShapeDtypeStruct((rows // seq_len, KV_W, seq_len), BF16),
                 row_out(POOL_W, F32), row_out(FOURIER_W, F32), row_out(D_MODEL, F32)]
    for param in cast_next:
        _, p_rows, p_cols = param.shape
        slab = (None, p_rows // steps, p_cols)
        in_specs.append(pl.BlockSpec(slab, lambda i: (l + 1, i, 0)))
        out_specs.append(pl.BlockSpec(slab, lambda i: (0, i, 0)))
        out_shape.append(jax.ShapeDtypeStruct((1, p_rows, p_cols), BF16))
        args.append(param)
    outs = pl.pallas_call(
        functools.partial(_inproj_kernel, rope=rope, n_convert=len(cast_next),
                          tiles_per_batch=tiles_per_seq if per_batch_mod else None),
        grid=(steps,),
        in_specs=in_specs,
        out_specs=out_specs,
        out_shape=out_shape,
        scratch_shapes=[pltpu.VMEM((tm, D_MODEL), BF16)],
        compiler_params=_params(1),
        name="inproj_rope" if rope else "inproj_ctx",
    )(*args)
    return outs[:6], outs[6:]


def _ctx_kv_kernel(x_ref, mod_ref, ng_ref, w_ref, kg_ref, k_ref, vt_ref):
    h = _modulated_norm(x_ref, mod_ref, ng_ref, CTX_MOD_ROW)
    p = _dot(h, w_ref[...])
    k_gain = kg_ref[...]
    for hh in range(N_KV_HEADS):
        lo = hh * HEAD_DIM
        k_ref[:, lo:lo + HEAD_DIM] = _head_norm(p[:, lo:lo + HEAD_DIM], k_gain).astype(BF16)
    vt_ref[...] = p[:, KV_W:2 * KV_W].T.astype(BF16)


def _ctx_kv(x2d, mod, norm_g, w_in, k_g, l, *, tm, seq_len):
    rows = x2d.shape[0]
    row_spec = lambda w: pl.BlockSpec((tm, w), lambda i: (i, 0))
    kv_cols = (0, OFF_K // (2 * KV_W))
    return pl.pallas_call(
        _ctx_kv_kernel,
        grid=(rows // tm,),
        in_specs=[row_spec(D_MODEL), _layer((MOD_ROWS, 3 * D_MODEL), l), _layer((1, D_MODEL), l),
                  _layer((D_MODEL, 2 * KV_W), 0, kv_cols), _layer((1, HEAD_DIM), l)],
        out_specs=[row_spec(KV_W), _vt_spec(tm, seq_len // tm)],
        out_shape=[jax.ShapeDtypeStruct((rows, KV_W), BF16),
                   jax.ShapeDtypeStruct((rows // seq_len, KV_W, seq_len), BF16)],
        compiler_params=_params(1),
        name="ctx_kv",
    )(x2d, mod, norm_g, w_in, k_g)


def _attn_kernel(*refs, lengths, tq):
    n_sources = len(lengths)
    q_ref, g_ref = refs[0], refs[1]
    k_refs = refs[2:2 + 2 * n_sources:2]
    vt_refs = refs[3:3 + 2 * n_sources:2]
    o_ref = refs[2 + 2 * n_sources]
    s_ref, p_ref, vta_ref = refs[3 + 2 * n_sources:]
    offsets = [sum(lengths[:i]) for i in range(n_sources)]
    total = sum(lengths)
    n_tiles = q_ref.shape[0] // tq
    key_chunks = [(k_ref, off, r0) for k_ref, off, length in zip(k_refs, offsets, lengths)
                  for r0 in range(0, length, KEY_CHUNK)]
    head_cols = [slice(hh * HEAD_DIM, (hh + 1) * HEAD_DIM) for hh in range(GQA_GROUP)]

    def q_rows(tile):
        return pl.ds(pl.multiple_of(tile * tq, tq), tq)

    def fold_rows(x, op):
        return op(x.reshape(x.shape[0] // SUBLANES, SUBLANES, tq), axis=0)

    def score_chunk(tile, hh, slot, chunk, col_max):
        k_ref, off, r0 = key_chunks[chunk]
        s = _dot_nt(k_ref[r0:r0 + KEY_CHUNK, :], q_ref[q_rows(tile), head_cols[hh]])
        s_ref[slot, off + r0:off + r0 + KEY_CHUNK, :] = s
        cm = fold_rows(s, jnp.max)
        return cm if col_max is None else jnp.maximum(col_max, cm)

    def prob_chunk(slot, chunk, col_max):
        for r0 in range(chunk * KEY_CHUNK, (chunk + 1) * KEY_CHUNK, EXP_CHUNK):
            p = jnp.exp2(s_ref[slot, r0:r0 + EXP_CHUNK, :] - col_max)
            p_ref[slot, r0:r0 + EXP_CHUNK, :] = p.astype(BF16)

    def finish(tile, hh, slot):
        acc = _dot(vta_ref[...], p_ref[slot])
        out = (acc[:HEAD_DIM] / acc[HEAD_DIM:HEAD_DIM + 1]).T
        gate = _silu(g_ref[q_rows(tile), head_cols[hh]])
        o_ref[q_rows(tile), head_cols[hh]] = (out * gate).astype(BF16)

    for vt_ref, off, length in zip(vt_refs, offsets, lengths):
        vta_ref[0:HEAD_DIM, off:off + length] = vt_ref[...]
    vta_ref[HEAD_DIM:HEAD_DIM + ONES_ROWS, :] = jnp.ones((ONES_ROWS, total), BF16)

    n_chunks = total // KEY_CHUNK
    first_max = None
    for chunk in range(n_chunks):
        first_max = score_chunk(0, 0, 0, chunk, first_max)
    first_max = jnp.max(first_max, axis=0, keepdims=True)

    def tile_body(tile, cur_max):
        for hh in range(GQA_GROUP):
            slot, next_slot = hh % 2, (hh + 1) % 2
            next_hh = (hh + 1) % GQA_GROUP
            next_tile = tile if next_hh else jnp.minimum(tile + 1, n_tiles - 1)
            next_max = None
            for chunk in range(n_chunks):
                next_max = score_chunk(next_tile, next_hh, next_slot, chunk, next_max)
                prob_chunk(slot, chunk, cur_max)
            finish(tile, hh, slot)
            cur_max = jnp.max(next_max, axis=0, keepdims=True)
        return cur_max

    lax.fori_loop(0, n_tiles, tile_body, first_max)


def _attention(q, g, sources, *, batch, q_len, tq):
    q_spec = pl.BlockSpec((q_len, GQA_GROUP * HEAD_DIM), lambda b, h: (b, h))
    in_specs = [q_spec, q_spec]
    args = [q, g]
    for k, vt, length in sources:
        in_specs += [pl.BlockSpec((length, HEAD_DIM), lambda b, h: (b, h)),
                     pl.BlockSpec((None, HEAD_DIM, length), lambda b, h: (b, h, 0))]
        args += [k, vt]
    lengths = tuple(length for _, _, length in sources)
    total = sum(lengths)
    return pl.pallas_call(
        functools.partial(_attn_kernel, lengths=lengths, tq=tq),
        grid=(batch, N_KV_HEADS),
        in_specs=in_specs,
        out_specs=q_spec,
        out_shape=jax.ShapeDtypeStruct((batch * q_len, ATTN_W), BF16),
        scratch_shapes=[pltpu.VMEM((2, total, tq), F32), pltpu.VMEM((2, total, tq), BF16),
                        pltpu.VMEM((HEAD_DIM + ONES_ROWS, total), BF16)],
        compiler_params=_params(2),
        name="attention",
    )(*args)


POOL_CHUNK = 256


def _pool_tile(up_ref, prev_ref, next_ref, g_ref, pw_ref, ps_ref, dst_ref,
               pad_ref, s2_ref, s4_ref, s8_ref, *, tile_in_seq, tiles_per_seq, seq_len):
    tm = up_ref.shape[0]
    halo = POOL_HALO
    chunk = min(tm, POOL_CHUNK)
    pad_ref[0:halo, :] = jnp.where(tile_in_seq > 0, prev_ref[...], 0.0)
    pad_ref[halo:halo + tm, :] = up_ref[...]
    pad_ref[halo + tm:2 * halo + tm, :] = jnp.where(tile_in_seq < tiles_per_seq - 1,
                                                    next_ref[...], 0.0)
    pad_ref[2 * halo + tm:4 * halo + tm, :] = jnp.zeros((2 * halo, POOL_W), F32)

    def forward_sum(dst, src, shift, length, lane0):
        for r0 in range(0, length, chunk):
            rows = min(chunk, length - r0)
            dst[r0:r0 + rows, :] = (src[r0:r0 + rows, lane0:]
                                    + src[r0 + shift:r0 + shift + rows, lane0:])

    forward_sum(s2_ref, pad_ref, 1, tm + 3 * halo, 0)
    forward_sum(s4_ref, s2_ref, 2, tm + 2 * halo, GROUP_C)
    forward_sum(s8_ref, s4_ref, 4, tm + halo, GROUP_C)

    def window_sum(gi, r0):
        if gi == 0:
            return s2_ref[halo - 1 + r0:halo - 1 + r0 + chunk, 0:GROUP_C]
        if gi == 1:
            return s4_ref[halo - 2 + r0:halo - 2 + r0 + chunk, 0:GROUP_C]
        if gi == 2:
            return s8_ref[halo - 4 + r0:halo - 4 + r0 + chunk, 0:GROUP_C]
        return (s8_ref[r0:r0 + chunk, GROUP_C:] + s8_ref[halo + r0:halo + r0 + chunk, GROUP_C:])

    def edge_inverse_count(win, local_row):
        before, after = win // 2, win - win // 2 - 1
        t = tile_in_seq * tm + local_row + lax.broadcasted_iota(jnp.int32, (halo, GROUP_C), 0)
        cnt = jnp.minimum(t + after, seq_len - 1) - jnp.maximum(t - before, 0) + 1
        return 1.0 / cnt.astype(F32)

    for gi, win in enumerate(POOL_WINDOWS):
        cols = slice(gi * GROUP_C, (gi + 1) * GROUP_C)
        w = pw_ref[gi].astype(BF16)
        for r0 in range(0, tm, chunk):
            pieces, inner = [], chunk
            if r0 == 0:
                pieces.append(edge_inverse_count(win, 0))
                inner -= halo
            tail = [edge_inverse_count(win, tm - halo)] if r0 + chunk == tm else []
            inner -= halo * len(tail)
            pieces.append(jnp.full((inner, GROUP_C), 1.0 / win, F32))
            inv_cnt = jnp.concatenate(pieces + tail, axis=0)
            pooled = window_sum(gi, r0) * inv_cnt - pad_ref[halo + r0:halo + r0 + chunk, cols]
            y = _dot(pooled.astype(BF16), w) * ps_ref[:, cols]
            dst_ref[r0:r0 + chunk, cols] = (y * _silu(g_ref[r0:r0 + chunk, cols])).astype(BF16)


def _pool_scratch(tm):
    return [pltpu.VMEM((tm + 4 * POOL_HALO, POOL_W), F32),
            pltpu.VMEM((tm + 3 * POOL_HALO, POOL_W), F32),
            pltpu.VMEM((tm + 2 * POOL_HALO, POOL_W - GROUP_C), F32),
            pltpu.VMEM((tm + POOL_HALO, POOL_W - 2 * GROUP_C), F32),
            pltpu.VMEM((tm, POOL_W), BF16)]


def _fourier_kernel(u_ref, g_ref, fw_ref, cc_ref, sc_ref, cn_ref, sn_ref, o_ref, ua_ref, ub_ref,
                    *, n, chunk):
    for gi in range(MIX_GROUPS):
        cols = slice(gi * GROUP_C, (gi + 1) * GROUP_C)
        w = fw_ref[gi].astype(BF16)
        a = _dot(cc_ref[...], w).astype(BF16)
        b = _dot(sc_ref[...], w).astype(BF16)
        ug = u_ref[:, cols].astype(BF16)
        ua_ref[:, cols] = _dot(ug, a).astype(BF16)
        ub_ref[:, cols] = _dot(ug, b).astype(BF16)
    for r0 in range(0, n, chunk):
        rows = slice(r0, r0 + chunk)
        y = _dot(cn_ref[rows, :], ua_ref[...]) - _dot(sn_ref[rows, :], ub_ref[...])
        o_ref[rows, :] = (y * _silu(g_ref[rows, :])).astype(BF16)


def _fourier(uf, g, fourier_w, tabs, l, *, batch, n):
    cc, sc, cn, sn = tabs
    seq_spec = lambda col: pl.BlockSpec((n, FOURIER_W), lambda b: (b, col))
    return pl.pallas_call(
        functools.partial(_fourier_kernel, n=n, chunk=min(n, 512)),
        grid=(batch,),
        in_specs=[seq_spec(0), seq_spec((ATTN_W + POOL_W) // FOURIER_W),
                  _layer((MIX_GROUPS, GROUP_C, GROUP_C), l),
                  _resident((GROUP_C, GROUP_C)), _resident((GROUP_C, GROUP_C)),
                  _resident((n, n)), _resident((n, n))],
        out_specs=seq_spec(0),
        out_shape=jax.ShapeDtypeStruct((batch * n, FOURIER_W), BF16),
        scratch_shapes=[pltpu.VMEM((n, FOURIER_W), BF16), pltpu.VMEM((n, FOURIER_W), BF16)],
        compiler_params=_params(1),
        name="fourier",
    )(uf, g, fourier_w, cc, sc, cn, sn)


RADIX = 8
COMBINE_ROWS = 16
SQRT_HALF = math.sqrt(0.5)


def _radix8_real(yr, yn):
    sr = [yr[n] + yr[n + 4] for n in range(4)]
    dr = [yr[n] - yr[n + 4] for n in range(4)]
    sn = {n: yn[n] + yn[n + 4] for n in (1, 3)}
    dn = {n: yn[n] - yn[n + 4] for n in (1, 2, 3)}
    t1r, t1i = (dr[1] - dn[1]) * SQRT_HALF, (dr[1] + dn[1]) * -SQRT_HALF
    t3r, t3i = (dr[3] + dn[3]) * -SQRT_HALF, (dn[3] - dr[3]) * SQRT_HALF
    out = [None] * RADIX
    ea, eb = sr[0] + sr[2], sr[1] + sr[3]
    ec, ed = sr[0] - sr[2], sn[3] - sn[1]
    out[0], out[4] = ea + eb, ea - eb
    out[2], out[6] = ec + ed, ec - ed
    oa, ob = dr[0] - dn[2], t1r + t3r
    oc, od = dr[0] + dn[2], t1i - t3i
    out[1], out[5] = oa + ob, oa - ob
    out[3], out[7] = oc + od, oc - od
    return out


def _fourier_radix_kernel(*refs, n1):
    u_refs = refs[:MIX_GROUPS]
    g_ref, fw_ref, cc_ref, sc_ref, cs_ref, o_ref = refs[MIX_GROUPS:MIX_GROUPS + 6]
    scratch = refs[MIX_GROUPS + 6:]
    z_refs, yr_refs, yn_refs = (scratch[i * MIX_GROUPS:(i + 1) * MIX_GROUPS] for i in range(3))
    pair = 2 * GROUP_C

    def matmul_steps(gi):
        wab = []

        def channel_part(n2):
            if not wab:
                w = fw_ref[gi].astype(BF16)
                wab.append(jnp.concatenate([_dot(cc_ref[...], w), _dot(sc_ref[...], w)],
                                           axis=1).astype(BF16))
            rows_n2 = u_refs[gi][pl.ds(n2, n1, stride=RADIX), :].astype(BF16)
            z_refs[gi][:, n2 * pair:(n2 + 1) * pair] = _dot(rows_n2, wab[0]).astype(BF16)

        def position_part(n2):
            prod = _dot(cs_ref[n2], z_refs[gi][:, n2 * pair:(n2 + 1) * pair])
            yr_refs[gi][n2] = prod[:n1, :GROUP_C] - prod[n1:, GROUP_C:]
            yn_refs[gi][n2] = prod[n1:, :GROUP_C] + prod[:n1, GROUP_C:]

        return ([functools.partial(channel_part, n2) for n2 in range(RADIX)]
                + [functools.partial(position_part, n2) for n2 in range(RADIX)])

    def combine_steps(gi):
        lanes = slice(gi * GROUP_C, (gi + 1) * GROUP_C)

        def combine(r0):
            rows = slice(r0, r0 + COMBINE_ROWS)
            yr = [yr_refs[gi][n2, rows, :] for n2 in range(RADIX)]
            yn = [yn_refs[gi][n2, rows, :] for n2 in range(RADIX)]
            for k2, val in enumerate(_radix8_real(yr, yn)):
                out_rows = slice(k2 * n1 + r0, k2 * n1 + r0 + COMBINE_ROWS)
                o_ref[out_rows, lanes] = (val * _silu(g_ref[out_rows, lanes])).astype(BF16)

        return [functools.partial(combine, r0) for r0 in range(0, n1, COMBINE_ROWS)]

    for step in matmul_steps(0):
        step()
    for gi in range(MIX_GROUPS):
        mxu = matmul_steps(gi + 1) if gi + 1 < MIX_GROUPS else []
        vpu = combine_steps(gi)
        for i in range(max(len(mxu), len(vpu))):
            if i < len(mxu):
                mxu[i]()
            if i < len(vpu):
                vpu[i]()


def _fourier_radix(uf, g, fourier_w, tabs, l, *, batch, n):
    cc, sc, cs = tabs
    n1 = n // RADIX
    seq_spec = lambda col: pl.BlockSpec((n, FOURIER_W), lambda b: (b, col))
    group_specs = [pl.BlockSpec((n, GROUP_C), lambda b, gi=gi: (b, gi)) for gi in range(MIX_GROUPS)]
    return pl.pallas_call(
        functools.partial(_fourier_radix_kernel, n1=n1),
        grid=(batch,),
        in_specs=group_specs + [
            seq_spec((ATTN_W + POOL_W) // FOURIER_W),
            _layer((MIX_GROUPS, GROUP_C, GROUP_C), l),
            _resident((GROUP_C, GROUP_C)), _resident((GROUP_C, GROUP_C)),
            _resident((RADIX, 2 * n1, n1))],
        out_specs=seq_spec(0),
        out_shape=jax.ShapeDtypeStruct((batch * n, FOURIER_W), BF16),
        scratch_shapes=([pltpu.VMEM((n1, 2 * RADIX * GROUP_C), BF16)] * MIX_GROUPS
                        + [pltpu.VMEM((RADIX, n1, GROUP_C), F32)] * (2 * MIX_GROUPS)),
        compiler_params=_params(1),
        name="fourier_radix",
    )(*([uf] * MIX_GROUPS), g, fourier_w, cc, sc, cs)


def _outproj_kernel(*refs, tiles_per_seq, seq_len, per_batch_mod, final):
    (att_ref, up_ref, prev_ref, next_ref, gp_ref, four_ref, w_ref, pw_ref, ps_ref,
     x_ref, mod_ref) = refs[:11]
    fg_ref = refs[11] if final else None
    o_ref, pad_ref, s2_ref, s4_ref, s8_ref, pooled_ref = refs[-6:]
    step = pl.program_id(0)
    o_ref[...] = (_dot(att_ref[...], w_ref[0:ATTN_W, :])
                  + _dot(four_ref[...], w_ref[ATTN_W + POOL_W:D_MODEL, :]))
    _pool_tile(up_ref, prev_ref, next_ref, gp_ref, pw_ref, ps_ref, pooled_ref,
               pad_ref, s2_ref, s4_ref, s8_ref, tile_in_seq=step % tiles_per_seq,
               tiles_per_seq=tiles_per_seq, seq_len=seq_len)
    row = step // tiles_per_seq if per_batch_mod else CTX_MOD_ROW
    gate = mod_ref[pl.ds(row, 1), 2 * D_MODEL:3 * D_MODEL]
    out = o_ref[...] + _dot(pooled_ref[...], w_ref[ATTN_W:ATTN_W + POOL_W, :])
    y = x_ref[...] + gate * out
    if final:
        ms = jnp.mean(y * y, axis=-1, keepdims=True)
        y = (y * lax.rsqrt(ms + EPS)) * fg_ref[...]
    o_ref[...] = y


def _outproj(att, up, g, four, w_out, pool_w, pool_scale, x2d, mod, final_g, l,
             *, tm, seq_len, per_batch_mod):
    rows = x2d.shape[0]
    final = final_g is not None
    tiles_per_seq = seq_len // tm
    halo_blocks = tm // POOL_HALO
    row_spec = lambda w: pl.BlockSpec((tm, w), lambda i: (i, 0))
    halo_spec = lambda index: pl.BlockSpec((POOL_HALO, POOL_W), lambda i: (index(i), 0))
    in_specs = [row_spec(ATTN_W), row_spec(POOL_W),
                halo_spec(lambda i: jnp.maximum(i * halo_blocks - 1, 0)),
                halo_spec(lambda i: jnp.minimum((i + 1) * halo_blocks, rows // POOL_HALO - 1)),
                pl.BlockSpec((tm, POOL_W), lambda i: (i, ATTN_W // POOL_W)), row_spec(FOURIER_W),
                _layer((D_MODEL, D_MODEL), 0), _layer((MIX_GROUPS, GROUP_C, GROUP_C), l),
                _layer((1, POOL_W), l), row_spec(D_MODEL), _layer((MOD_ROWS, 3 * D_MODEL), l)]
    args = [att, up, up, up, g, four, w_out, pool_w, pool_scale, x2d, mod]
    if final:
        in_specs.append(_resident((1, D_MODEL)))
        args.append(final_g)
    return pl.pallas_call(
        functools.partial(_outproj_kernel, tiles_per_seq=tiles_per_seq, seq_len=seq_len,
                          per_batch_mod=per_batch_mod, final=final),
        grid=(rows // tm,),
        in_specs=in_specs,
        out_specs=row_spec(D_MODEL),
        out_shape=jax.ShapeDtypeStruct((rows, D_MODEL), F32),
        scratch_shapes=_pool_scratch(tm),
        compiler_params=_params(1),
        name="outproj_final" if final else "outproj",
    )(*args)


def _rope_tables(n):
    pos = jnp.arange(n, dtype=jnp.int32)
    row = (pos // GRID_W).astype(F32)
    col = (pos % GRID_W).astype(F32)
    inv = ROPE_THETA ** (-jnp.arange(0, AXIS_ROT, 2, dtype=F32) / AXIS_ROT)
    ang_r = row[:, None] * inv
    ang_c = col[:, None] * inv
    cos = jnp.concatenate([jnp.cos(ang_r), jnp.cos(ang_r), jnp.cos(ang_c), jnp.cos(ang_c)], axis=1)
    sin = jnp.concatenate([-jnp.sin(ang_r), jnp.sin(ang_r), -jnp.sin(ang_c), jnp.sin(ang_c)], axis=1)
    return cos, sin


def _dft_cos_sin(n):
    k = jnp.arange(n, dtype=jnp.int32)
    ang = ((k[:, None] * k[None, :]) % n).astype(F32) * (2.0 * math.pi / n)
    return jnp.cos(ang), jnp.sin(ang)


def _fourier_tables(n):
    cc, sc = _dft_cos_sin(GROUP_C)
    cn, sn = _dft_cos_sin(n)
    norm = 1.0 / math.sqrt(n * GROUP_C)
    return (cc * norm).astype(BF16), (sc * norm).astype(BF16), cn.astype(BF16), sn.astype(BF16)


def _fourier_radix_tables(n):
    n1 = n // RADIX
    cc, sc = _dft_cos_sin(GROUP_C)
    norm = 1.0 / math.sqrt(n * GROUP_C)
    k1 = jnp.arange(n1, dtype=jnp.int32)[None, :, None]
    pos = (RADIX * jnp.arange(n1, dtype=jnp.int32)[None, None, :]
           + jnp.arange(RADIX, dtype=jnp.int32)[:, None, None])
    ang = ((k1 * pos) % n).astype(F32) * (2.0 * math.pi / n)
    cs = jnp.concatenate([jnp.cos(ang), jnp.sin(ang)], axis=1).astype(BF16)
    return (cc * norm).astype(BF16), (sc * norm).astype(BF16), cs


def kernel(x, c, ctx, c_ctx, ada_w, ada_b, norm_g, w_in, q_norm_g, k_norm_g, pool_w, pool_scale,
           fourier_w, w_out, final_norm_g):
    B, N, _ = x.shape
    C = ctx.shape[1]
    tm = 512
    tm_ctx = 256

    cond = jnp.concatenate([c, c_ctx[None, :], jnp.zeros((MOD_ROWS - B - 1, D_MODEL), F32)], axis=0)
    mod = _adaln(cond, ada_w, ada_b)

    rope_tabs = _rope_tables(N)
    four_tabs = _fourier_radix_tables(N)
    four_tabs_ctx = _fourier_tables(C)
    w_in_b = w_in[0:1].astype(BF16)
    w_out_b = w_out[0:1].astype(BF16)
    ng = norm_g[:, None, :]
    qg = q_norm_g[:, None, :]
    kg = k_norm_g[:, None, :]
    ps = pool_scale[:, None, :]

    xl = x.reshape(B * N, D_MODEL)
    xc = ctx.reshape(B * C, D_MODEL)
    for l in range(DEPTH):
        last = l == DEPTH - 1
        if last:
            kc, vct = _ctx_kv(xc, mod, ng, w_in_b, kg, l, tm=tm_ctx, seq_len=C)
        else:
            (qc, kc, vct, upc, ufc, gc), _ = _inproj(xc, mod, ng, w_in_b, qg, kg, None, l,
                                                     tm=tm_ctx, seq_len=C, per_batch_mod=False)
        (q, k, vt, up, uf, g), next_weights = _inproj(
            xl, mod, ng, w_in_b, qg, kg, rope_tabs, l, tm=tm, seq_len=N, per_batch_mod=True,
            cast_next=() if last else (w_in, w_out))

        att = _attention(q, g, [(kc, vct, C), (k, vt, N)], batch=B, q_len=N, tq=256)
        four = _fourier_radix(uf, g, fourier_w, four_tabs, l, batch=B, n=N)
        xl_new = _outproj(att, up, g, four, w_out_b, pool_w, ps, xl, mod,
                          final_norm_g[None, :] if last else None, l,
                          tm=tm, seq_len=N, per_batch_mod=True)

        if not last:
            attc = _attention(qc, gc, [(kc, vct, C)], batch=B, q_len=C, tq=C)
            fourc = _fourier(ufc, gc, fourier_w, four_tabs_ctx, l, batch=B, n=C)
            xc = _outproj(attc, upc, gc, fourc, w_out_b, pool_w, ps, xc, mod, None, l,
                          tm=tm_ctx, seq_len=C, per_batch_mod=False)
            w_in_b, w_out_b = next_weights
        xl = xl_new
    return xl.reshape(B, N, D_MODEL)
```

```python
import functools
import math

import jax
import jax.numpy as jnp
from jax import lax
from jax.experimental import pallas as pl
from jax.experimental.pallas import tpu as pltpu

D_MODEL = 2048
DEPTH = 2
GRID_W = 64
HEAD_DIM = 128
ATTN_W = 1024
N_HEADS = 8
N_KV_HEADS = 2
GQA_GROUP = 4
KV_W = 256
POOL_W = 512
POOL_WINDOWS = (2, 4, 8, 16)
FOURIER_W = 512
MIX_GROUPS = 4
GROUP_C = 128
OFF_K = ATTN_W
OFF_V = OFF_K + KV_W
OFF_POOL = OFF_V + KV_W
OFF_FOURIER = OFF_POOL + POOL_W
OFF_GATE = OFF_FOURIER + FOURIER_W
IN_W = OFF_GATE + D_MODEL
ROPE_THETA = 10000.0
AXIS_ROT = HEAD_DIM // 2
EPS = 1e-6

MOD_ROWS = 16
CTX_MOD_ROW = 8
SUBLANES = 8
POOL_HALO = 8
KEY_CHUNK = 256
EXP_CHUNK = 256
ONES_ROWS = 16
V7X_VMEM_BYTES = 64 * 1024 * 1024
VMEM_LIMIT = V7X_VMEM_BYTES - 8 * 1024 * 1024

BF16 = jnp.bfloat16
F32 = jnp.float32


def _silu(x):
    return x * jax.nn.sigmoid(x)


def _dot(a, b):
    return jnp.dot(a, b, preferred_element_type=F32)


def _dot_nt(a, b):
    return lax.dot_general(a, b, (((1,), (1,)), ((), ())), preferred_element_type=F32)


def _resident(shape):
    zeros = (0,) * len(shape)
    return pl.BlockSpec(shape, lambda *_: zeros, pipeline_mode=pl.Buffered(1))


def _layer(shape, l, tail=None):
    index = (l,) + (tail or (0,) * len(shape))
    return pl.BlockSpec((None,) + tuple(shape), lambda *_: index, pipeline_mode=pl.Buffered(1))


def _params(n_grid):
    return pltpu.CompilerParams(dimension_semantics=("arbitrary",) * n_grid,
                                vmem_limit_bytes=VMEM_LIMIT)


def _adaln_kernel(c_ref, w_ref, b_ref, o_ref):
    a = _silu(c_ref[...]).astype(BF16)
    o_ref[...] = _dot(a, w_ref[...].astype(BF16)) + b_ref[...]


def _adaln(cond, ada_w, ada_b):
    tn = 1024
    return pl.pallas_call(
        _adaln_kernel,
        grid=(DEPTH, 3 * D_MODEL // tn),
        in_specs=[
            pl.BlockSpec((MOD_ROWS, D_MODEL), lambda l, j: (0, 0)),
            pl.BlockSpec((None, D_MODEL, tn), lambda l, j: (l, 0, j)),
            pl.BlockSpec((None, 1, tn), lambda l, j: (l, 0, j)),
        ],
        out_specs=pl.BlockSpec((None, MOD_ROWS, tn), lambda l, j: (l, 0, j)),
        out_shape=jax.ShapeDtypeStruct((DEPTH, MOD_ROWS, 3 * D_MODEL), F32),
        compiler_params=_params(2),
        name="adaln",
    )(cond, ada_w, ada_b.reshape(DEPTH, 1, 3 * D_MODEL))


def _modulated_norm(x_ref, mod_ref, ng_ref, row):
    x = x_ref[...]
    ms = jnp.mean(x * x, axis=-1, keepdims=True)
    shift = mod_ref[pl.ds(row, 1), 0:D_MODEL]
    scale = mod_ref[pl.ds(row, 1), D_MODEL:2 * D_MODEL]
    y = (x * lax.rsqrt(ms + EPS)) * ng_ref[...]
    return (y * (1.0 + scale) + shift).astype(BF16)


def _head_norm(p, gain):
    ms = jnp.mean(p * p, axis=-1, keepdims=True)
    return (p * lax.rsqrt(ms + EPS)) * gain


def _rope(y, cos, sin, swap_lo):
    partner = jnp.where(swap_lo, pltpu.roll(y, 96, 1), pltpu.roll(y, 32, 1))
    return y * cos + partner * sin


def _inproj_kernel(*refs, rope, tiles_per_batch, n_convert):
    n_in = 8 if rope else 6
    x_ref, mod_ref, ng_ref, w_ref, qg_ref, kg_ref = refs[:6]
    cos_ref, sin_ref = refs[6:8] if rope else (None, None)
    f32_weight_refs = refs[n_in:n_in + n_convert]
    q_ref, k_ref, vt_ref, up_ref, uf_ref, g_ref = refs[n_in + n_convert:n_in + n_convert + 6]
    bf16_weight_refs = refs[n_in + n_convert + 6:n_in + 2 * n_convert + 6]
    h_ref = refs[-1]
    for src_ref, dst_ref in zip(f32_weight_refs, bf16_weight_refs):
        dst_ref[...] = src_ref[...].astype(BF16)
    tm = x_ref.shape[0]
    row = CTX_MOD_ROW if tiles_per_batch is None else pl.program_id(0) // tiles_per_batch
    h_ref[...] = _modulated_norm(x_ref, mod_ref, ng_ref, row)

    if rope:
        cos = cos_ref[...]
        sin = sin_ref[...]
        lane = lax.broadcasted_iota(jnp.int32, (tm, HEAD_DIM), 1)
        swap_lo = (lane & (AXIS_ROT // 2)) == 0

    def finish_head(p, gain):
        y = _head_norm(p, gain)
        if rope:
            y = _rope(y, cos, sin, swap_lo)
        return y.astype(BF16)

    q_gain = qg_ref[...] * (HEAD_DIM ** -0.5 * math.log2(math.e))
    chunk = GQA_GROUP * HEAD_DIM
    for c0 in range(0, ATTN_W, chunk):
        p = _dot(h_ref[...], w_ref[:, c0:c0 + chunk])
        for hh in range(GQA_GROUP):
            lo = hh * HEAD_DIM
            q_ref[:, c0 + lo:c0 + lo + HEAD_DIM] = finish_head(p[:, lo:lo + HEAD_DIM], q_gain)

    p = _dot(h_ref[...], w_ref[:, OFF_K:OFF_POOL])
    k_gain = kg_ref[...]
    for hh in range(N_KV_HEADS):
        lo = hh * HEAD_DIM
        k_ref[:, lo:lo + HEAD_DIM] = finish_head(p[:, lo:lo + HEAD_DIM], k_gain)
    vt_ref[...] = p[:, KV_W:2 * KV_W].T.astype(BF16)

    up_ref[...] = _dot(h_ref[...], w_ref[:, OFF_POOL:OFF_FOURIER])
    uf_ref[...] = _dot(h_ref[...], w_ref[:, OFF_FOURIER:OFF_GATE])
    for c0 in range(0, D_MODEL, chunk):
        g_ref[:, c0:c0 + chunk] = _dot(h_ref[...], w_ref[:, OFF_GATE + c0:OFF_GATE + c0 + chunk])


def _vt_spec(tm, tiles_per_seq):
    return pl.BlockSpec((None, KV_W, tm), lambda i: (i // tiles_per_seq, 0, i % tiles_per_seq))


def _inproj(x2d, mod, norm_g, w_in, q_g, k_g, rope_tabs, l, *, tm, seq_len, per_batch_mod,
            cast_next=()):
    rows = x2d.shape[0]
    steps = rows // tm
    rope = rope_tabs is not None
    tiles_per_seq = seq_len // tm
    row_spec = lambda w: pl.BlockSpec((tm, w), lambda i: (i, 0))
    in_specs = [row_spec(D_MODEL), _layer((MOD_ROWS, 3 * D_MODEL), l), _layer((1, D_MODEL), l),
                _layer((D_MODEL, IN_W), 0), _layer((1, HEAD_DIM), l), _layer((1, HEAD_DIM), l)]
    args = [x2d, mod, norm_g, w_in, q_g, k_g]
    if rope:
        tab_spec = pl.BlockSpec((tm, HEAD_DIM), lambda i: (i % tiles_per_seq, 0))
        in_specs += [tab_spec, tab_spec]
        args += list(rope_tabs)
    row_out = lambda w, dt: jax.ShapeDtypeStruct((rows, w), dt)
    out_specs = [row_spec(ATTN_W), row_spec(KV_W), _vt_spec(tm, tiles_per_seq),
                 row_spec(POOL_W), row_spec(FOURIER_W), row_spec(D_MODEL)]
    out_shape = [row_out(ATTN_W, BF16), row_out(KV_W, BF16),
                 jax.ShapeDtypeStruct((rows // seq_len, KV_W, seq_len), BF16),
                 row_out(POOL_W, F32), row_out(FOURIER_W, F32), row_out(D_MODEL, F32)]
    for param in cast_next:
        _, p_rows, p_cols = param.shape
        slab = (None, p_rows // steps, p_cols)
        in_specs.append(pl.BlockSpec(slab, lambda i: (l + 1, i, 0)))
        out_specs.append(pl.BlockSpec(slab, lambda i: (0, i, 0)))
        out_shape.append(jax.ShapeDtypeStruct((1, p_rows, p_cols), BF16))
        args.append(param)
    outs = pl.pallas_call(
        functools.partial(_inproj_kernel, rope=rope, n_convert=len(cast_next),
                          tiles_per_batch=tiles_per_seq if per_batch_mod else None),
        grid=(steps,),
        in_specs=in_specs,
        out_specs=out_specs,
        out_shape=out_shape,
        scratch_shapes=[pltpu.VMEM((tm, D_MODEL), BF16)],
        compiler_params=_params(1),
        name="inproj_rope" if rope else "inproj_ctx",
    )(*args)
    return outs[:6], outs[6:]


def _ctx_kv_kernel(x_ref, mod_ref, ng_ref, w_ref, kg_ref, k_ref, vt_ref):
    h = _modulated_norm(x_ref, mod_ref, ng_ref, CTX_MOD_ROW)
    p = _dot(h, w_ref[...])
    k_gain = kg_ref[...]
    for hh in range(N_KV_HEADS):
        lo = hh * HEAD_DIM
        k_ref[:, lo:lo + HEAD_DIM] = _head_norm(p[:, lo:lo + HEAD_DIM], k_gain).astype(BF16)
    vt_ref[...] = p[:, KV_W:2 * KV_W].T.astype(BF16)


def _ctx_kv(x2d, mod, norm_g, w_in, k_g, l, *, tm, seq_len):
    rows = x2d.shape[0]
    row_spec = lambda w: pl.BlockSpec((tm, w), lambda i: (i, 0))
    kv_cols = (0, OFF_K // (2 * KV_W))
    return pl.pallas_call(
        _ctx_kv_kernel,
        grid=(rows // tm,),
        in_specs=[row_spec(D_MODEL), _layer((MOD_ROWS, 3 * D_MODEL), l), _layer((1, D_MODEL), l),
                  _layer((D_MODEL, 2 * KV_W), 0, kv_cols), _layer((1, HEAD_DIM), l)],
        out_specs=[row_spec(KV_W), _vt_spec(tm, seq_len // tm)],
        out_shape=[jax.ShapeDtypeStruct((rows, KV_W), BF16),
                   jax.ShapeDtypeStruct((rows // seq_len, KV_W, seq_len), BF16)],
        compiler_params=_params(1),
        name="ctx_kv",
    )(x2d, mod, norm_g, w_in, k_g)


def _attn_kernel(*refs, lengths, tq):
    n_sources = len(lengths)
    q_ref, g_ref = refs[0], refs[1]
    k_refs = refs[2:2 + 2 * n_sources:2]
    vt_refs = refs[3:3 + 2 * n_sources:2]
    o_ref = refs[2 + 2 * n_sources]
    s_ref, p_ref, vta_ref = refs[3 + 2 * n_sources:]
    offsets = [sum(lengths[:i]) for i in range(n_sources)]
    total = sum(lengths)
    n_tiles = q_ref.shape[0] // tq
    key_chunks = [(k_ref, off, r0) for k_ref, off, length in zip(k_refs, offsets, lengths)
                  for r0 in range(0, length, KEY_CHUNK)]
    head_cols = [slice(hh * HEAD_DIM, (hh + 1) * HEAD_DIM) for hh in range(GQA_GROUP)]

    def q_rows(tile):
        return pl.ds(pl.multiple_of(tile * tq, tq), tq)

    def fold_rows(x, op):
        return op(x.reshape(x.shape[0] // SUBLANES, SUBLANES, tq), axis=0)

    def score_chunk(tile, hh, slot, chunk, col_max):
        k_ref, off, r0 = key_chunks[chunk]
        s = _dot_nt(k_ref[r0:r0 + KEY_CHUNK, :], q_ref[q_rows(tile), head_cols[hh]])
        s_ref[slot, off + r0:off + r0 + KEY_CHUNK, :] = s
        cm = fold_rows(s, jnp.max)
        return cm if col_max is None else jnp.maximum(col_max, cm)

    def prob_chunk(slot, chunk, col_max):
        for r0 in range(chunk * KEY_CHUNK, (chunk + 1) * KEY_CHUNK, EXP_CHUNK):
            p = jnp.exp2(s_ref[slot, r0:r0 + EXP_CHUNK, :] - col_max)
            p_ref[slot, r0:r0 + EXP_CHUNK, :] = p.astype(BF16)

    def finish(tile, hh, slot):
        acc = _dot(vta_ref[...], p_ref[slot])
        out = (acc[:HEAD_DIM] / acc[HEAD_DIM:HEAD_DIM + 1]).T
        gate = _silu(g_ref[q_rows(tile), head_cols[hh]])
        o_ref[q_rows(tile), head_cols[hh]] = (out * gate).astype(BF16)

    for vt_ref, off, length in zip(vt_refs, offsets, lengths):
        vta_ref[0:HEAD_DIM, off:off + length] = vt_ref[...]
    vta_ref[HEAD_DIM:HEAD_DIM + ONES_ROWS, :] = jnp.ones((ONES_ROWS, total), BF16)

    n_chunks = total // KEY_CHUNK
    first_max = None
    for chunk in range(n_chunks):
        first_max = score_chunk(0, 0, 0, chunk, first_max)
    first_max = jnp.max(first_max, axis=0, keepdims=True)

    def tile_body(tile, cur_max):
        for hh in range(GQA_GROUP):
            slot, next_slot = hh % 2, (hh + 1) % 2
            next_hh = (hh + 1) % GQA_GROUP
            next_tile = tile if next_hh else jnp.minimum(tile + 1, n_tiles - 1)
            next_max = None
            for chunk in range(n_chunks):
                next_max = score_chunk(next_tile, next_hh, next_slot, chunk, next_max)
                prob_chunk(slot, chunk, cur_max)
            finish(tile, hh, slot)
            cur_max = jnp.max(next_max, axis=0, keepdims=True)
        return cur_max

    lax.fori_loop(0, n_tiles, tile_body, first_max)


def _attention(q, g, sources, *, batch, q_len, tq):
    q_spec = pl.BlockSpec((q_len, GQA_GROUP * HEAD_DIM), lambda b, h: (b, h))
    in_specs = [q_spec, q_spec]
    args = [q, g]
    for k, vt, length in sources:
        in_specs += [pl.BlockSpec((length, HEAD_DIM), lambda b, h: (b, h)),
                     pl.BlockSpec((None, HEAD_DIM, length), lambda b, h: (b, h, 0))]
        args += [k, vt]
    lengths = tuple(length for _, _, length in sources)
    total = sum(lengths)
    return pl.pallas_call(
        functools.partial(_attn_kernel, lengths=lengths, tq=tq),
        grid=(batch, N_KV_HEADS),
        in_specs=in_specs,
        out_specs=q_spec,
        out_shape=jax.ShapeDtypeStruct((batch * q_len, ATTN_W), BF16),
        scratch_shapes=[pltpu.VMEM((2, total, tq), F32), pltpu.VMEM((2, total, tq), BF16),
                        pltpu.VMEM((HEAD_DIM + ONES_ROWS, total), BF16)],
        compiler_params=_params(2),
        name="attention",
    )(*args)


POOL_CHUNK = 256
OUT_CHUNK = 256


def _pool_steps(up_ref, prev_ref, next_ref, g_ref, pw_ref, ps_ref, dst_ref,
                pad_ref, s2_ref, s4_ref, s8_ref, *, tile_in_seq, tiles_per_seq, seq_len):
    tm = up_ref.shape[0]
    halo = POOL_HALO
    chunk = min(tm, POOL_CHUNK)

    def forward_sum(dst, src, shift, length, lane0):
        for r0 in range(0, length, chunk):
            rows = min(chunk, length - r0)
            dst[r0:r0 + rows, :] = (src[r0:r0 + rows, lane0:]
                                    + src[r0 + shift:r0 + shift + rows, lane0:])

    def window_sums():
        pad_ref[0:halo, :] = jnp.where(tile_in_seq > 0, prev_ref[...], 0.0)
        pad_ref[halo:halo + tm, :] = up_ref[...]
        pad_ref[halo + tm:2 * halo + tm, :] = jnp.where(tile_in_seq < tiles_per_seq - 1,
                                                        next_ref[...], 0.0)
        pad_ref[2 * halo + tm:4 * halo + tm, :] = jnp.zeros((2 * halo, POOL_W), F32)
        forward_sum(s2_ref, pad_ref, 1, tm + 3 * halo, 0)
        forward_sum(s4_ref, s2_ref, 2, tm + 2 * halo, GROUP_C)
        forward_sum(s8_ref, s4_ref, 4, tm + halo, GROUP_C)

    def window_sum(gi, r0):
        if gi == 0:
            return s2_ref[halo - 1 + r0:halo - 1 + r0 + chunk, 0:GROUP_C]
        if gi == 1:
            return s4_ref[halo - 2 + r0:halo - 2 + r0 + chunk, 0:GROUP_C]
        if gi == 2:
            return s8_ref[halo - 4 + r0:halo - 4 + r0 + chunk, 0:GROUP_C]
        return (s8_ref[r0:r0 + chunk, GROUP_C:] + s8_ref[halo + r0:halo + r0 + chunk, GROUP_C:])

    def edge_inverse_count(win, local_row):
        before, after = win // 2, win - win // 2 - 1
        t = tile_in_seq * tm + local_row + lax.broadcasted_iota(jnp.int32, (halo, GROUP_C), 0)
        cnt = jnp.minimum(t + after, seq_len - 1) - jnp.maximum(t - before, 0) + 1
        return 1.0 / cnt.astype(F32)

    def pooled_block(gi, win, r0):
        cols = slice(gi * GROUP_C, (gi + 1) * GROUP_C)
        pieces, inner = [], chunk
        if r0 == 0:
            pieces.append(edge_inverse_count(win, 0))
            inner -= halo
        tail = [edge_inverse_count(win, tm - halo)] if r0 + chunk == tm else []
        inner -= halo * len(tail)
        pieces.append(jnp.full((inner, GROUP_C), 1.0 / win, F32))
        inv_cnt = jnp.concatenate(pieces + tail, axis=0)
        pooled = window_sum(gi, r0) * inv_cnt - pad_ref[halo + r0:halo + r0 + chunk, cols]
        y = _dot(pooled.astype(BF16), pw_ref[gi].astype(BF16)) * ps_ref[:, cols]
        dst_ref[r0:r0 + chunk, cols] = (y * _silu(g_ref[r0:r0 + chunk, cols])).astype(BF16)

    return [window_sums] + [functools.partial(pooled_block, gi, win, r0)
                            for gi, win in enumerate(POOL_WINDOWS)
                            for r0 in range(0, tm, chunk)]


def _pool_scratch(tm):
    return [pltpu.VMEM((tm + 4 * POOL_HALO, POOL_W), F32),
            pltpu.VMEM((tm + 3 * POOL_HALO, POOL_W), F32),
            pltpu.VMEM((tm + 2 * POOL_HALO, POOL_W - GROUP_C), F32),
            pltpu.VMEM((tm + POOL_HALO, POOL_W - 2 * GROUP_C), F32),
            pltpu.VMEM((tm, POOL_W), BF16)]


def _fourier_kernel(u_ref, g_ref, fw_ref, cc_ref, sc_ref, cn_ref, sn_ref, o_ref, ua_ref, ub_ref,
                    *, n, chunk):
    for gi in range(MIX_GROUPS):
        cols = slice(gi * GROUP_C, (gi + 1) * GROUP_C)
        w = fw_ref[gi].astype(BF16)
        a = _dot(cc_ref[...], w).astype(BF16)
        b = _dot(sc_ref[...], w).astype(BF16)
        ug = u_ref[:, cols].astype(BF16)
        ua_ref[:, cols] = _dot(ug, a).astype(BF16)
        ub_ref[:, cols] = _dot(ug, b).astype(BF16)
    for r0 in range(0, n, chunk):
        rows = slice(r0, r0 + chunk)
        y = _dot(cn_ref[rows, :], ua_ref[...]) - _dot(sn_ref[rows, :], ub_ref[...])
        o_ref[rows, :] = (y * _silu(g_ref[rows, :])).astype(BF16)


def _fourier(uf, g, fourier_w, tabs, l, *, batch, n):
    cc, sc, cn, sn = tabs
    seq_spec = lambda col: pl.BlockSpec((n, FOURIER_W), lambda b: (b, col))
    return pl.pallas_call(
        functools.partial(_fourier_kernel, n=n, chunk=min(n, 512)),
        grid=(batch,),
        in_specs=[seq_spec(0), seq_spec((ATTN_W + POOL_W) // FOURIER_W),
                  _layer((MIX_GROUPS, GROUP_C, GROUP_C), l),
                  _resident((GROUP_C, GROUP_C)), _resident((GROUP_C, GROUP_C)),
                  _resident((n, n)), _resident((n, n))],
        out_specs=seq_spec(0),
        out_shape=jax.ShapeDtypeStruct((batch * n, FOURIER_W), BF16),
        scratch_shapes=[pltpu.VMEM((n, FOURIER_W), BF16), pltpu.VMEM((n, FOURIER_W), BF16)],
        compiler_params=_params(1),
        name="fourier",
    )(uf, g, fourier_w, cc, sc, cn, sn)


RADIX = 8
COMBINE_ROWS = 16
SQRT_HALF = math.sqrt(0.5)


def _radix8_real(yr, yn):
    sr = [yr[n] + yr[n + 4] for n in range(4)]
    dr = [yr[n] - yr[n + 4] for n in range(4)]
    sn = {n: yn[n] + yn[n + 4] for n in (1, 3)}
    dn = {n: yn[n] - yn[n + 4] for n in (1, 2, 3)}
    t1r, t1i = (dr[1] - dn[1]) * SQRT_HALF, (dr[1] + dn[1]) * -SQRT_HALF
    t3r, t3i = (dr[3] + dn[3]) * -SQRT_HALF, (dn[3] - dr[3]) * SQRT_HALF
    out = [None] * RADIX
    ea, eb = sr[0] + sr[2], sr[1] + sr[3]
    ec, ed = sr[0] - sr[2], sn[3] - sn[1]
    out[0], out[4] = ea + eb, ea - eb
    out[2], out[6] = ec + ed, ec - ed
    oa, ob = dr[0] - dn[2], t1r + t3r
    oc, od = dr[0] + dn[2], t1i - t3i
    out[1], out[5] = oa + ob, oa - ob
    out[3], out[7] = oc + od, oc - od
    return out


def _fourier_radix_kernel(*refs, n1):
    u_refs = refs[:MIX_GROUPS]
    g_ref, fw_ref, cc_ref, sc_ref, cs_ref, o_ref = refs[MIX_GROUPS:MIX_GROUPS + 6]
    scratch = refs[MIX_GROUPS + 6:]
    z_refs, yr_refs, yn_refs = (scratch[i * MIX_GROUPS:(i + 1) * MIX_GROUPS] for i in range(3))
    pair = 2 * GROUP_C

    def matmul_steps(gi):
        wab = []

        def channel_part(n2):
            if not wab:
                w = fw_ref[gi].astype(BF16)
                wab.append(jnp.concatenate([_dot(cc_ref[...], w), _dot(sc_ref[...], w)],
                                           axis=1).astype(BF16))
            rows_n2 = u_refs[gi][pl.ds(n2, n1, stride=RADIX), :].astype(BF16)
            z_refs[gi][:, n2 * pair:(n2 + 1) * pair] = _dot(rows_n2, wab[0]).astype(BF16)

        def position_part(n2):
            prod = _dot(cs_ref[n2], z_refs[gi][:, n2 * pair:(n2 + 1) * pair])
            yr_refs[gi][n2] = prod[:n1, :GROUP_C] - prod[n1:, GROUP_C:]
            yn_refs[gi][n2] = prod[n1:, :GROUP_C] + prod[:n1, GROUP_C:]

        return ([functools.partial(channel_part, n2) for n2 in range(RADIX)]
                + [functools.partial(position_part, n2) for n2 in range(RADIX)])

    def combine_steps(gi):
        lanes = slice(gi * GROUP_C, (gi + 1) * GROUP_C)

        def combine(r0):
            rows = slice(r0, r0 + COMBINE_ROWS)
            yr = [yr_refs[gi][n2, rows, :] for n2 in range(RADIX)]
            yn = [yn_refs[gi][n2, rows, :] for n2 in range(RADIX)]
            for k2, val in enumerate(_radix8_real(yr, yn)):
                out_rows = slice(k2 * n1 + r0, k2 * n1 + r0 + COMBINE_ROWS)
                o_ref[out_rows, lanes] = (val * _silu(g_ref[out_rows, lanes])).astype(BF16)

        return [functools.partial(combine, r0) for r0 in range(0, n1, COMBINE_ROWS)]

    for step in matmul_steps(0):
        step()
    for gi in range(MIX_GROUPS):
        mxu = matmul_steps(gi + 1) if gi + 1 < MIX_GROUPS else []
        vpu = combine_steps(gi)
        for step in mxu:
            step()
        for step in vpu:
            step()


def _fourier_radix(uf, g, fourier_w, tabs, l, *, batch, n):
    cc, sc, cs = tabs
    n1 = n // RADIX
    seq_spec = lambda col: pl.BlockSpec((n, FOURIER_W), lambda b: (b, col))
    group_specs = [pl.BlockSpec((n, GROUP_C), lambda b, gi=gi: (b, gi)) for gi in range(MIX_GROUPS)]
    return pl.pallas_call(
        functools.partial(_fourier_radix_kernel, n1=n1),
        grid=(batch,),
        in_specs=group_specs + [
            seq_spec((ATTN_W + POOL_W) // FOURIER_W),
            _layer((MIX_GROUPS, GROUP_C, GROUP_C), l),
            _resident((GROUP_C, GROUP_C)), _resident((GROUP_C, GROUP_C)),
            _resident((RADIX, 2 * n1, n1))],
        out_specs=seq_spec(0),
        out_shape=jax.ShapeDtypeStruct((batch * n, FOURIER_W), BF16),
        scratch_shapes=([pltpu.VMEM((n1, 2 * RADIX * GROUP_C), BF16)] * MIX_GROUPS
                        + [pltpu.VMEM((RADIX, n1, GROUP_C), F32)] * (2 * MIX_GROUPS)),
        compiler_params=_params(1),
        name="fourier_radix",
    )(*([uf] * MIX_GROUPS), g, fourier_w, cc, sc, cs)


def _outproj_kernel(*refs, tiles_per_seq, seq_len, per_batch_mod, final):
    (att_ref, up_ref, prev_ref, next_ref, gp_ref, four_ref, w_ref, pw_ref, ps_ref,
     x_ref, mod_ref) = refs[:11]
    fg_ref = refs[11] if final else None
    o_ref, pad_ref, s2_ref, s4_ref, s8_ref, pooled_ref = refs[-6:]
    step = pl.program_id(0)
    pool_steps = _pool_steps(up_ref, prev_ref, next_ref, gp_ref, pw_ref, ps_ref, pooled_ref,
                             pad_ref, s2_ref, s4_ref, s8_ref, tile_in_seq=step % tiles_per_seq,
                             tiles_per_seq=tiles_per_seq, seq_len=seq_len)
    col_chunks = [slice(c0, c0 + OUT_CHUNK) for c0 in range(0, D_MODEL, OUT_CHUNK)]

    pool_steps.pop(0)()
    for cols in col_chunks:
        o_ref[:, cols] = (_dot(att_ref[...], w_ref[0:ATTN_W, cols])
                          + _dot(four_ref[...], w_ref[ATTN_W + POOL_W:D_MODEL, cols]))
        if pool_steps:
            pool_steps.pop(0)()
    for pool_step in pool_steps:
        pool_step()

    row = step // tiles_per_seq if per_batch_mod else CTX_MOD_ROW
    sum_sq = None
    for cols in col_chunks:
        gate = mod_ref[pl.ds(row, 1), 2 * D_MODEL + cols.start:2 * D_MODEL + cols.stop]
        out = o_ref[:, cols] + _dot(pooled_ref[...], w_ref[ATTN_W:ATTN_W + POOL_W, cols])
        y = x_ref[:, cols] + gate * out
        o_ref[:, cols] = y
        if final:
            part = jnp.sum(y * y, axis=-1, keepdims=True)
            sum_sq = part if sum_sq is None else sum_sq + part
    if final:
        inv_rms = lax.rsqrt(sum_sq * (1.0 / D_MODEL) + EPS)
        o_ref[...] = (o_ref[...] * inv_rms) * fg_ref[...]


def _outproj(att, up, g, four, w_out, pool_w, pool_scale, x2d, mod, final_g, l,
             *, tm, seq_len, per_batch_mod):
    rows = x2d.shape[0]
    final = final_g is not None
    tiles_per_seq = seq_len // tm
    halo_blocks = tm // POOL_HALO
    row_spec = lambda w: pl.BlockSpec((tm, w), lambda i: (i, 0))
    halo_spec = lambda index: pl.BlockSpec((POOL_HALO, POOL_W), lambda i: (index(i), 0))
    in_specs = [row_spec(ATTN_W), row_spec(POOL_W),
                halo_spec(lambda i: jnp.maximum(i * halo_blocks - 1, 0)),
                halo_spec(lambda i: jnp.minimum((i + 1) * halo_blocks, rows // POOL_HALO - 1)),
                pl.BlockSpec((tm, POOL_W), lambda i: (i, ATTN_W // POOL_W)), row_spec(FOURIER_W),
                _layer((D_MODEL, D_MODEL), 0), _layer((MIX_GROUPS, GROUP_C, GROUP_C), l),
                _layer((1, POOL_W), l), row_spec(D_MODEL), _layer((MOD_ROWS, 3 * D_MODEL), l)]
    args = [att, up, up, up, g, four, w_out, pool_w, pool_scale, x2d, mod]
    if final:
        in_specs.append(_resident((1, D_MODEL)))
        args.append(final_g)
    return pl.pallas_call(
        functools.partial(_outproj_kernel, tiles_per_seq=tiles_per_seq, seq_len=seq_len,
                          per_batch_mod=per_batch_mod, final=final),
        grid=(rows // tm,),
        in_specs=in_specs,
        out_specs=row_spec(D_MODEL),
        out_shape=jax.ShapeDtypeStruct((rows, D_MODEL), F32),
        scratch_shapes=_pool_scratch(tm),
        compiler_params=_params(1),
        name="outproj_final" if final else "outproj",
    )(*args)


def _rope_tables(n):
    pos = jnp.arange(n, dtype=jnp.int32)
    row = (pos // GRID_W).astype(F32)
    col = (pos % GRID_W).astype(F32)
    inv = ROPE_THETA ** (-jnp.arange(0, AXIS_ROT, 2, dtype=F32) / AXIS_ROT)
    ang_r = row[:, None] * inv
    ang_c = col[:, None] * inv
    cos = jnp.concatenate([jnp.cos(ang_r), jnp.cos(ang_r), jnp.cos(ang_c), jnp.cos(ang_c)], axis=1)
    sin = jnp.concatenate([-jnp.sin(ang_r), jnp.sin(ang_r), -jnp.sin(ang_c), jnp.sin(ang_c)], axis=1)
    return cos, sin


def _dft_cos_sin(n):
    k = jnp.arange(n, dtype=jnp.int32)
    ang = ((k[:, None] * k[None, :]) % n).astype(F32) * (2.0 * math.pi / n)
    return jnp.cos(ang), jnp.sin(ang)


def _fourier_tables(n):
    cc, sc = _dft_cos_sin(GROUP_C)
    cn, sn = _dft_cos_sin(n)
    norm = 1.0 / math.sqrt(n * GROUP_C)
    return (cc * norm).astype(BF16), (sc * norm).astype(BF16), cn.astype(BF16), sn.astype(BF16)


def _fourier_radix_tables(n):
    n1 = n // RADIX
    cc, sc = _dft_cos_sin(GROUP_C)
    norm = 1.0 / math.sqrt(n * GROUP_C)
    k1 = jnp.arange(n1, dtype=jnp.int32)[None, :, None]
    pos = (RADIX * jnp.arange(n1, dtype=jnp.int32)[None, None, :]
           + jnp.arange(RADIX, dtype=jnp.int32)[:, None, None])
    ang = ((k1 * pos) % n).astype(F32) * (2.0 * math.pi / n)
    cs = jnp.concatenate([jnp.cos(ang), jnp.sin(ang)], axis=1).astype(BF16)
    return (cc * norm).astype(BF16), (sc * norm).astype(BF16), cs


def kernel(x, c, ctx, c_ctx, ada_w, ada_b, norm_g, w_in, q_norm_g, k_norm_g, pool_w, pool_scale,
           fourier_w, w_out, final_norm_g):
    B, N, _ = x.shape
    C = ctx.shape[1]
    tm = 512
    tm_ctx = 256

    cond = jnp.concatenate([c, c_ctx[None, :], jnp.zeros((MOD_ROWS - B - 1, D_MODEL), F32)], axis=0)
    mod = _adaln(cond, ada_w, ada_b)

    rope_tabs = _rope_tables(N)
    four_tabs = _fourier_radix_tables(N)
    four_tabs_ctx = _fourier_tables(C)
    w_in_b = w_in[0:1].astype(BF16)
    w_out_b = w_out[0:1].astype(BF16)
    ng = norm_g[:, None, :]
    qg = q_norm_g[:, None, :]
    kg = k_norm_g[:, None, :]
    ps = pool_scale[:, None, :]

    xl = x.reshape(B * N, D_MODEL)
    xc = ctx.reshape(B * C, D_MODEL)
    for l in range(DEPTH):
        last = l == DEPTH - 1
        if last:
            kc, vct = _ctx_kv(xc, mod, ng, w_in_b, kg, l, tm=tm_ctx, seq_len=C)
        else:
            (qc, kc, vct, upc, ufc, gc), _ = _inproj(xc, mod, ng, w_in_b, qg, kg, None, l,
                                                     tm=tm_ctx, seq_len=C, per_batch_mod=False)
        (q, k, vt, up, uf, g), next_weights = _inproj(
            xl, mod, ng, w_in_b, qg, kg, rope_tabs, l, tm=tm, seq_len=N, per_batch_mod=True,
            cast_next=() if last else (w_in, w_out))

        att = _attention(q, g, [(kc, vct, C), (k, vt, N)], batch=B, q_len=N, tq=256)
        four = _fourier_radix(uf, g, fourier_w, four_tabs, l, batch=B, n=N)
        xl_new = _outproj(att, up, g, four, w_out_b, pool_w, ps, xl, mod,
                          final_norm_g[None, :] if last else None, l,
                          tm=tm, seq_len=N, per_batch_mod=True)

        if not last:
            attc = _attention(qc, gc, [(kc, vct, C)], batch=B, q_len=C, tq=C)
            fourc = _fourier(ufc, gc, fourier_w, four_tabs_ctx, l, batch=B, n=C)
            xc = _outproj(attc, upc, gc, fourc, w_out_b, pool_w, ps, xc, mod, None, l,
                          tm=tm_ctx, seq_len=C, per_batch_mod=False)
            w_in_b, w_out_b = next_weights
        xl = xl_new
    return xl.reshape(B, N, D_MODEL)
```

```python
import functools
import math

import jax
import jax.numpy as jnp
from jax import lax
from jax.experimental import pallas as pl
from jax.experimental.pallas import tpu as pltpu

D_MODEL = 2048
DEPTH = 2
GRID_W = 64
HEAD_DIM = 128
ATTN_W = 1024
N_HEADS = 8
N_KV_HEADS = 2
GQA_GROUP = 4
KV_W = 256
POOL_W = 512
POOL_WINDOWS = (2, 4, 8, 16)
FOURIER_W = 512
MIX_GROUPS = 4
GROUP_C = 128
OFF_K = ATTN_W
OFF_V = OFF_K + KV_W
OFF_POOL = OFF_V + KV_W
OFF_FOURIER = OFF_POOL + POOL_W
OFF_GATE = OFF_FOURIER + FOURIER_W
IN_W = OFF_GATE + D_MODEL
ROPE_THETA = 10000.0
AXIS_ROT = HEAD_DIM // 2
EPS = 1e-6

MOD_ROWS = 16
CTX_MOD_ROW = 8
SUBLANES = 8
PART_ROWS = 256
NORM_ROWS = 32
POOL_HALO = 8
KEY_CHUNK = 256
EXP_CHUNK = 256
ONES_ROWS = 16
V7X_VMEM_BYTES = 64 * 1024 * 1024
VMEM_LIMIT = V7X_VMEM_BYTES - 8 * 1024 * 1024

BF16 = jnp.bfloat16
F32 = jnp.float32


def _silu(x):
    return x * jax.nn.sigmoid(x)


def _dot(a, b):
    return jnp.dot(a, b, preferred_element_type=F32)


def _dot_nt(a, b):
    return lax.dot_general(a, b, (((1,), (1,)), ((), ())), preferred_element_type=F32)


def _resident(shape):
    zeros = (0,) * len(shape)
    return pl.BlockSpec(shape, lambda *_: zeros, pipeline_mode=pl.Buffered(1))


def _layer(shape, l, tail=None):
    index = (l,) + (tail or (0,) * len(shape))
    return pl.BlockSpec((None,) + tuple(shape), lambda *_: index, pipeline_mode=pl.Buffered(1))


def _params(n_grid):
    return pltpu.CompilerParams(dimension_semantics=("arbitrary",) * n_grid,
                                vmem_limit_bytes=VMEM_LIMIT)


def _adaln_kernel(c_ref, w_ref, b_ref, o_ref):
    a = _silu(c_ref[...]).astype(BF16)
    o_ref[...] = _dot(a, w_ref[...].astype(BF16)) + b_ref[...]


def _adaln(cond, ada_w, ada_b):
    tn = 1024
    return pl.pallas_call(
        _adaln_kernel,
        grid=(DEPTH, 3 * D_MODEL // tn),
        in_specs=[
            pl.BlockSpec((MOD_ROWS, D_MODEL), lambda l, j: (0, 0)),
            pl.BlockSpec((None, D_MODEL, tn), lambda l, j: (l, 0, j)),
            pl.BlockSpec((None, 1, tn), lambda l, j: (l, 0, j)),
        ],
        out_specs=pl.BlockSpec((None, MOD_ROWS, tn), lambda l, j: (l, 0, j)),
        out_shape=jax.ShapeDtypeStruct((DEPTH, MOD_ROWS, 3 * D_MODEL), F32),
        compiler_params=_params(2),
        name="adaln",
    )(cond, ada_w, ada_b.reshape(DEPTH, 1, 3 * D_MODEL))


def _modulated_norm(x_ref, mod_ref, ng_ref, row, rows=slice(None)):
    x = x_ref[rows, :]
    ms = jnp.mean(x * x, axis=-1, keepdims=True)
    shift = mod_ref[pl.ds(row, 1), 0:D_MODEL]
    scale = mod_ref[pl.ds(row, 1), D_MODEL:2 * D_MODEL]
    y = (x * lax.rsqrt(ms + EPS)) * ng_ref[...]
    return (y * (1.0 + scale) + shift).astype(BF16)


def _head_norm(p, gain):
    ms = jnp.mean(p * p, axis=-1, keepdims=True)
    return (p * lax.rsqrt(ms + EPS)) * gain


def _rope(y, cos, sin, swap_lo):
    partner = jnp.where(swap_lo, pltpu.roll(y, 96, 1), pltpu.roll(y, 32, 1))
    return y * cos + partner * sin


def _inproj_kernel(*refs, rope, tiles_per_batch, n_convert):
    n_in = 8 if rope else 6
    x_ref, mod_ref, ng_ref, w_ref, qg_ref, kg_ref = refs[:6]
    cos_ref, sin_ref = refs[6:8] if rope else (None, None)
    f32_weight_refs = refs[n_in:n_in + n_convert]
    q_ref, k_ref, vt_ref, up_ref, uf_ref, g_ref = refs[n_in + n_convert:n_in + n_convert + 6]
    bf16_weight_refs = refs[n_in + n_convert + 6:n_in + 2 * n_convert + 6]
    h_ref = refs[-1]
    for src_ref, dst_ref in zip(f32_weight_refs, bf16_weight_refs):
        dst_ref[...] = src_ref[...].astype(BF16)
    tm = x_ref.shape[0]
    row = CTX_MOD_ROW if tiles_per_batch is None else pl.program_id(0) // tiles_per_batch
    q_gain = qg_ref[...] * (HEAD_DIM ** -0.5 * math.log2(math.e))
    k_gain = kg_ref[...]
    chunk = GQA_GROUP * HEAD_DIM

    def norm_steps(rows):
        pieces = [slice(r0, r0 + NORM_ROWS) for r0 in range(rows.start, rows.stop, NORM_ROWS)]

        def norm(piece):
            h_ref[piece, :] = _modulated_norm(x_ref, mod_ref, ng_ref, row, piece)

        return [functools.partial(norm, piece) for piece in pieces]

    def matmul_steps(rows):
        n_rows = rows.stop - rows.start
        if rope:
            cos, sin = cos_ref[rows, :], sin_ref[rows, :]
            lane = lax.broadcasted_iota(jnp.int32, (n_rows, HEAD_DIM), 1)
            swap_lo = (lane & (AXIS_ROT // 2)) == 0

        def finish_head(p, gain):
            y = _head_norm(p, gain)
            if rope:
                y = _rope(y, cos, sin, swap_lo)
            return y.astype(BF16)

        def project(c0, width):
            return _dot(h_ref[rows, :], w_ref[:, c0:c0 + width])

        def q_chunk(c0):
            p = project(c0, chunk)
            for hh in range(GQA_GROUP):
                lo = hh * HEAD_DIM
                q_ref[rows, c0 + lo:c0 + lo + HEAD_DIM] = finish_head(p[:, lo:lo + HEAD_DIM], q_gain)

        def kv_chunk():
            p = project(OFF_K, 2 * KV_W)
            for hh in range(N_KV_HEADS):
                lo = hh * HEAD_DIM
                k_ref[rows, lo:lo + HEAD_DIM] = finish_head(p[:, lo:lo + HEAD_DIM], k_gain)
            vt_ref[:, rows] = p[:, KV_W:2 * KV_W].T.astype(BF16)

        def up_chunk():
            up_ref[rows, :] = project(OFF_POOL, POOL_W)

        def uf_chunk():
            uf_ref[rows, :] = project(OFF_FOURIER, FOURIER_W)

        def gate_chunk(c0):
            g_ref[rows, c0:c0 + chunk] = project(OFF_GATE + c0, chunk)

        return ([functools.partial(q_chunk, c0) for c0 in range(0, ATTN_W, chunk)]
                + [kv_chunk, up_chunk, uf_chunk]
                + [functools.partial(gate_chunk, c0) for c0 in range(0, D_MODEL, chunk)])

    part_rows = min(tm, PART_ROWS)
    parts = [slice(r0, r0 + part_rows) for r0 in range(0, tm, part_rows)]
    for step in norm_steps(parts[0]):
        step()
    for index, rows in enumerate(parts):
        fill = norm_steps(parts[index + 1]) if index + 1 < len(parts) else []
        for step in matmul_steps(rows):
            step()
            if fill:
                fill.pop(0)()
        for step in fill:
            step()


def _vt_spec(tm, tiles_per_seq):
    return pl.BlockSpec((None, KV_W, tm), lambda i: (i // tiles_per_seq, 0, i % tiles_per_seq))


def _inproj(x2d, mod, norm_g, w_in, q_g, k_g, rope_tabs, l, *, tm, seq_len, per_batch_mod,
            cast_next=()):
    rows = x2d.shape[0]
    steps = rows // tm
    rope = rope_tabs is not None
    tiles_per_seq = seq_len // tm
    row_spec = lambda w: pl.BlockSpec((tm, w), lambda i: (i, 0))
    in_specs = [row_spec(D_MODEL), _layer((MOD_ROWS, 3 * D_MODEL), l), _layer((1, D_MODEL), l),
                _layer((D_MODEL, IN_W), 0), _layer((1, HEAD_DIM), l), _layer((1, HEAD_DIM), l)]
    args = [x2d, mod, norm_g, w_in, q_g, k_g]
    if rope:
        tab_spec = pl.BlockSpec((tm, HEAD_DIM), lambda i: (i % tiles_per_seq, 0))
        in_specs += [tab_spec, tab_spec]
        args += list(rope_tabs)
    row_out = lambda w, dt: jax.ShapeDtypeStruct((rows, w), dt)
    out_specs = [row_spec(ATTN_W), row_spec(KV_W), _vt_spec(tm, tiles_per_seq),
                 row_spec(POOL_W), row_spec(FOURIER_W), row_spec(D_MODEL)]
    out_shape = [row_out(ATTN_W, BF16), row_out(KV_W, BF16),
                 jax.ShapeDtypeStruct((rows // seq_len, KV_W, seq_len), BF16),
                 row_out(POOL_W, F32), row_out(FOURIER_W, F32), row_out(D_MODEL, F32)]
    for param in cast_next:
        _, p_rows, p_cols = param.shape
        slab = (None, p_rows // steps, p_cols)
        in_specs.append(pl.BlockSpec(slab, lambda i: (l + 1, i, 0)))
        out_specs.append(pl.BlockSpec(slab, lambda i: (0, i, 0)))
        out_shape.append(jax.ShapeDtypeStruct((1, p_rows, p_cols), BF16))
        args.append(param)
    outs = pl.pallas_call(
        functools.partial(_inproj_kernel, rope=rope, n_convert=len(cast_next),
                          tiles_per_batch=tiles_per_seq if per_batch_mod else None),
        grid=(steps,),
        in_specs=in_specs,
        out_specs=out_specs,
        out_shape=out_shape,
        scratch_shapes=[pltpu.VMEM((tm, D_MODEL), BF16)],
        compiler_params=_params(1),
        name="inproj_rope" if rope else "inproj_ctx",
    )(*args)
    return outs[:6], outs[6:]


def _ctx_kv_kernel(x_ref, mod_ref, ng_ref, w_ref, kg_ref, k_ref, vt_ref):
    h = _modulated_norm(x_ref, mod_ref, ng_ref, CTX_MOD_ROW)
    p = _dot(h, w_ref[...])
    k_gain = kg_ref[...]
    for hh in range(N_KV_HEADS):
        lo = hh * HEAD_DIM
        k_ref[:, lo:lo + HEAD_DIM] = _head_norm(p[:, lo:lo + HEAD_DIM], k_gain).astype(BF16)
    vt_ref[...] = p[:, KV_W:2 * KV_W].T.astype(BF16)


def _ctx_kv(x2d, mod, norm_g, w_in, k_g, l, *, tm, seq_len):
    rows = x2d.shape[0]
    row_spec = lambda w: pl.BlockSpec((tm, w), lambda i: (i, 0))
    kv_cols = (0, OFF_K // (2 * KV_W))
    return pl.pallas_call(
        _ctx_kv_kernel,
        grid=(rows // tm,),
        in_specs=[row_spec(D_MODEL), _layer((MOD_ROWS, 3 * D_MODEL), l), _layer((1, D_MODEL), l),
                  _layer((D_MODEL, 2 * KV_W), 0, kv_cols), _layer((1, HEAD_DIM), l)],
        out_specs=[row_spec(KV_W), _vt_spec(tm, seq_len // tm)],
        out_shape=[jax.ShapeDtypeStruct((rows, KV_W), BF16),
                   jax.ShapeDtypeStruct((rows // seq_len, KV_W, seq_len), BF16)],
        compiler_params=_params(1),
        name="ctx_kv",
    )(x2d, mod, norm_g, w_in, k_g)


def _attn_kernel(*refs, lengths, tq):
    n_sources = len(lengths)
    q_ref, g_ref = refs[0], refs[1]
    k_refs = refs[2:2 + 2 * n_sources:2]
    vt_refs = refs[3:3 + 2 * n_sources:2]
    o_ref = refs[2 + 2 * n_sources]
    s_ref, p_ref, vta_ref = refs[3 + 2 * n_sources:]
    offsets = [sum(lengths[:i]) for i in range(n_sources)]
    total = sum(lengths)
    n_tiles = q_ref.shape[0] // tq
    key_chunks = [(k_ref, off, r0) for k_ref, off, length in zip(k_refs, offsets, lengths)
                  for r0 in range(0, length, KEY_CHUNK)]
    head_cols = [slice(hh * HEAD_DIM, (hh + 1) * HEAD_DIM) for hh in range(GQA_GROUP)]

    def q_rows(tile):
        return pl.ds(pl.multiple_of(tile * tq, tq), tq)

    def fold_rows(x, op):
        return op(x.reshape(x.shape[0] // SUBLANES, SUBLANES, tq), axis=0)

    def score_chunk(tile, hh, slot, chunk, col_max):
        k_ref, off, r0 = key_chunks[chunk]
        s = _dot_nt(k_ref[r0:r0 + KEY_CHUNK, :], q_ref[q_rows(tile), head_cols[hh]])
        s_ref[slot, off + r0:off + r0 + KEY_CHUNK, :] = s
        cm = fold_rows(s, jnp.max)
        return cm if col_max is None else jnp.maximum(col_max, cm)

    def prob_chunk(slot, chunk, col_max):
        for r0 in range(chunk * KEY_CHUNK, (chunk + 1) * KEY_CHUNK, EXP_CHUNK):
            p = jnp.exp2(s_ref[slot, r0:r0 + EXP_CHUNK, :] - col_max)
            p_ref[slot, r0:r0 + EXP_CHUNK, :] = p.astype(BF16)

    def finish(tile, hh, slot):
        acc = _dot(vta_ref[...], p_ref[slot])
        out = (acc[:HEAD_DIM] / acc[HEAD_DIM:HEAD_DIM + 1]).T
        gate = _silu(g_ref[q_rows(tile), head_cols[hh]])
        o_ref[q_rows(tile), head_cols[hh]] = (out * gate).astype(BF16)

    for vt_ref, off, length in zip(vt_refs, offsets, lengths):
        vta_ref[0:HEAD_DIM, off:off + length] = vt_ref[...]
    vta_ref[HEAD_DIM:HEAD_DIM + ONES_ROWS, :] = jnp.ones((ONES_ROWS, total), BF16)

    n_chunks = total // KEY_CHUNK
    first_max = None
    for chunk in range(n_chunks):
        first_max = score_chunk(0, 0, 0, chunk, first_max)
    first_max = jnp.max(first_max, axis=0, keepdims=True)

    def tile_body(tile, cur_max):
        for hh in range(GQA_GROUP):
            slot, next_slot = hh % 2, (hh + 1) % 2
            next_hh = (hh + 1) % GQA_GROUP
            next_tile = tile if next_hh else jnp.minimum(tile + 1, n_tiles - 1)
            next_max = None
            for chunk in range(n_chunks):
                next_max = score_chunk(next_tile, next_hh, next_slot, chunk, next_max)
                prob_chunk(slot, chunk, cur_max)
            finish(tile, hh, slot)
            cur_max = jnp.max(next_max, axis=0, keepdims=True)
        return cur_max

    lax.fori_loop(0, n_tiles, tile_body, first_max)


def _attention(q, g, sources, *, batch, q_len, tq):
    q_spec = pl.BlockSpec((q_len, GQA_GROUP * HEAD_DIM), lambda b, h: (b, h))
    in_specs = [q_spec, q_spec]
    args = [q, g]
    for k, vt, length in sources:
        in_specs += [pl.BlockSpec((length, HEAD_DIM), lambda b, h: (b, h)),
                     pl.BlockSpec((None, HEAD_DIM, length), lambda b, h: (b, h, 0))]
        args += [k, vt]
    lengths = tuple(length for _, _, length in sources)
    total = sum(lengths)
    return pl.pallas_call(
        functools.partial(_attn_kernel, lengths=lengths, tq=tq),
        grid=(batch, N_KV_HEADS),
        in_specs=in_specs,
        out_specs=q_spec,
        out_shape=jax.ShapeDtypeStruct((batch * q_len, ATTN_W), BF16),
        scratch_shapes=[pltpu.VMEM((2, total, tq), F32), pltpu.VMEM((2, total, tq), BF16),
                        pltpu.VMEM((HEAD_DIM + ONES_ROWS, total), BF16)],
        compiler_params=_params(2),
        name="attention",
    )(*args)


POOL_CHUNK = 256
OUT_CHUNK = 256


def _pool_steps(up_ref, prev_ref, next_ref, g_ref, pw_ref, ps_ref, dst_ref,
                pad_ref, s2_ref, s4_ref, s8_ref, *, tile_in_seq, tiles_per_seq, seq_len):
    tm = up_ref.shape[0]
    halo = POOL_HALO
    chunk = min(tm, POOL_CHUNK)

    def forward_sum(dst, src, shift, length, lane0):
        for r0 in range(0, length, chunk):
            rows = min(chunk, length - r0)
            dst[r0:r0 + rows, :] = (src[r0:r0 + rows, lane0:]
                                    + src[r0 + shift:r0 + shift + rows, lane0:])

    def window_sums():
        pad_ref[0:halo, :] = jnp.where(tile_in_seq > 0, prev_ref[...], 0.0)
        pad_ref[halo:halo + tm, :] = up_ref[...]
        pad_ref[halo + tm:2 * halo + tm, :] = jnp.where(tile_in_seq < tiles_per_seq - 1,
                                                        next_ref[...], 0.0)
        pad_ref[2 * halo + tm:4 * halo + tm, :] = jnp.zeros((2 * halo, POOL_W), F32)
        forward_sum(s2_ref, pad_ref, 1, tm + 3 * halo, 0)
        forward_sum(s4_ref, s2_ref, 2, tm + 2 * halo, GROUP_C)
        forward_sum(s8_ref, s4_ref, 4, tm + halo, GROUP_C)

    def window_sum(gi, r0):
        if gi == 0:
            return s2_ref[halo - 1 + r0:halo - 1 + r0 + chunk, 0:GROUP_C]
        if gi == 1:
            return s4_ref[halo - 2 + r0:halo - 2 + r0 + chunk, 0:GROUP_C]
        if gi == 2:
            return s8_ref[halo - 4 + r0:halo - 4 + r0 + chunk, 0:GROUP_C]
        return (s8_ref[r0:r0 + chunk, GROUP_C:] + s8_ref[halo + r0:halo + r0 + chunk, GROUP_C:])

    def edge_inverse_count(win, local_row):
        before, after = win // 2, win - win // 2 - 1
        t = tile_in_seq * tm + local_row + lax.broadcasted_iota(jnp.int32, (halo, GROUP_C), 0)
        cnt = jnp.minimum(t + after, seq_len - 1) - jnp.maximum(t - before, 0) + 1
        return 1.0 / cnt.astype(F32)

    def pooled_block(gi, win, r0):
        cols = slice(gi * GROUP_C, (gi + 1) * GROUP_C)
        pieces, inner = [], chunk
        if r0 == 0:
            pieces.append(edge_inverse_count(win, 0))
            inner -= halo
        tail = [edge_inverse_count(win, tm - halo)] if r0 + chunk == tm else []
        inner -= halo * len(tail)
        pieces.append(jnp.full((inner, GROUP_C), 1.0 / win, F32))
        inv_cnt = jnp.concatenate(pieces + tail, axis=0)
        pooled = window_sum(gi, r0) * inv_cnt - pad_ref[halo + r0:halo + r0 + chunk, cols]
        y = _dot(pooled.astype(BF16), pw_ref[gi].astype(BF16)) * ps_ref[:, cols]
        dst_ref[r0:r0 + chunk, cols] = (y * _silu(g_ref[r0:r0 + chunk, cols])).astype(BF16)

    return [window_sums] + [functools.partial(pooled_block, gi, win, r0)
                            for gi, win in enumerate(POOL_WINDOWS)
                            for r0 in range(0, tm, chunk)]


def _pool_scratch(tm):
    return [pltpu.VMEM((tm + 4 * POOL_HALO, POOL_W), F32),
            pltpu.VMEM((tm + 3 * POOL_HALO, POOL_W), F32),
            pltpu.VMEM((tm + 2 * POOL_HALO, POOL_W - GROUP_C), F32),
            pltpu.VMEM((tm + POOL_HALO, POOL_W - 2 * GROUP_C), F32),
            pltpu.VMEM((tm, POOL_W), BF16)]


def _fourier_kernel(u_ref, g_ref, fw_ref, cc_ref, sc_ref, cn_ref, sn_ref, o_ref, ua_ref, ub_ref,
                    *, n, chunk):
    for gi in range(MIX_GROUPS):
        cols = slice(gi * GROUP_C, (gi + 1) * GROUP_C)
        w = fw_ref[gi].astype(BF16)
        a = _dot(cc_ref[...], w).astype(BF16)
        b = _dot(sc_ref[...], w).astype(BF16)
        ug = u_ref[:, cols].astype(BF16)
        ua_ref[:, cols] = _dot(ug, a).astype(BF16)
        ub_ref[:, cols] = _dot(ug, b).astype(BF16)
    for r0 in range(0, n, chunk):
        rows = slice(r0, r0 + chunk)
        y = _dot(cn_ref[rows, :], ua_ref[...]) - _dot(sn_ref[rows, :], ub_ref[...])
        o_ref[rows, :] = (y * _silu(g_ref[rows, :])).astype(BF16)


def _fourier(uf, g, fourier_w, tabs, l, *, batch, n):
    cc, sc, cn, sn = tabs
    seq_spec = lambda col: pl.BlockSpec((n, FOURIER_W), lambda b: (b, col))
    return pl.pallas_call(
        functools.partial(_fourier_kernel, n=n, chunk=min(n, 512)),
        grid=(batch,),
        in_specs=[seq_spec(0), seq_spec((ATTN_W + POOL_W) // FOURIER_W),
                  _layer((MIX_GROUPS, GROUP_C, GROUP_C), l),
                  _resident((GROUP_C, GROUP_C)), _resident((GROUP_C, GROUP_C)),
                  _resident((n, n)), _resident((n, n))],
        out_specs=seq_spec(0),
        out_shape=jax.ShapeDtypeStruct((batch * n, FOURIER_W), BF16),
        scratch_shapes=[pltpu.VMEM((n, FOURIER_W), BF16), pltpu.VMEM((n, FOURIER_W), BF16)],
        compiler_params=_params(1),
        name="fourier",
    )(uf, g, fourier_w, cc, sc, cn, sn)


RADIX = 8
COMBINE_ROWS = 16
SQRT_HALF = math.sqrt(0.5)


def _radix8_real(yr, yn):
    sr = [yr[n] + yr[n + 4] for n in range(4)]
    dr = [yr[n] - yr[n + 4] for n in range(4)]
    sn = {n: yn[n] + yn[n + 4] for n in (1, 3)}
    dn = {n: yn[n] - yn[n + 4] for n in (1, 2, 3)}
    t1r, t1i = (dr[1] - dn[1]) * SQRT_HALF, (dr[1] + dn[1]) * -SQRT_HALF
    t3r, t3i = (dr[3] + dn[3]) * -SQRT_HALF, (dn[3] - dr[3]) * SQRT_HALF
    out = [None] * RADIX
    ea, eb = sr[0] + sr[2], sr[1] + sr[3]
    ec, ed = sr[0] - sr[2], sn[3] - sn[1]
    out[0], out[4] = ea + eb, ea - eb
    out[2], out[6] = ec + ed, ec - ed
    oa, ob = dr[0] - dn[2], t1r + t3r
    oc, od = dr[0] + dn[2], t1i - t3i
    out[1], out[5] = oa + ob, oa - ob
    out[3], out[7] = oc + od, oc - od
    return out


def _fourier_radix_kernel(*refs, n1):
    u_refs = refs[:MIX_GROUPS]
    g_ref, fw_ref, cc_ref, sc_ref, cs_ref, o_ref, ub_ref = refs[MIX_GROUPS:MIX_GROUPS + 7]
    scratch = refs[MIX_GROUPS + 7:]
    pq_refs, yr_refs, yn_refs = (scratch[i * MIX_GROUPS:(i + 1) * MIX_GROUPS] for i in range(3))
    group_lanes = [slice(gi * GROUP_C, (gi + 1) * GROUP_C) for gi in range(MIX_GROUPS)]

    for n2 in range(RADIX):
        for gi, lanes in enumerate(group_lanes):
            ub_ref[n2, :, lanes] = u_refs[gi][pl.ds(n2, n1, stride=RADIX), :].astype(BF16)
        pq = _dot(cs_ref[n2], ub_ref[n2])
        rows = slice(n2 * n1, (n2 + 1) * n1)
        for gi, lanes in enumerate(group_lanes):
            pq_refs[gi][rows, :GROUP_C] = pq[:n1, lanes].astype(BF16)
            pq_refs[gi][rows, GROUP_C:] = pq[n1:, lanes].astype(BF16)

    for gi, lanes in enumerate(group_lanes):
        w = fw_ref[gi].astype(BF16)
        a, b = _dot(cc_ref[...], w), _dot(sc_ref[...], w)
        mix = jnp.concatenate([jnp.concatenate([a, b], axis=1),
                               jnp.concatenate([-b, a], axis=1)], axis=0).astype(BF16)
        y = _dot(pq_refs[gi][...], mix)
        for n2 in range(RADIX):
            yr_refs[gi][n2] = y[n2 * n1:(n2 + 1) * n1, :GROUP_C]
            yn_refs[gi][n2] = y[n2 * n1:(n2 + 1) * n1, GROUP_C:]

        for r0 in range(0, n1, COMBINE_ROWS):
            rows = slice(r0, r0 + COMBINE_ROWS)
            yr = [yr_refs[gi][n2, rows, :] for n2 in range(RADIX)]
            yn = [yn_refs[gi][n2, rows, :] for n2 in range(RADIX)]
            for k2, val in enumerate(_radix8_real(yr, yn)):
                out_rows = slice(k2 * n1 + r0, k2 * n1 + r0 + COMBINE_ROWS)
                o_ref[out_rows, lanes] = (val * _silu(g_ref[out_rows, lanes])).astype(BF16)


def _fourier_radix(uf, g, fourier_w, tabs, l, *, batch, n):
    cc, sc, cs = tabs
    n1 = n // RADIX
    seq_spec = lambda col: pl.BlockSpec((n, FOURIER_W), lambda b: (b, col))
    group_specs = [pl.BlockSpec((n, GROUP_C), lambda b, gi=gi: (b, gi)) for gi in range(MIX_GROUPS)]
    return pl.pallas_call(
        functools.partial(_fourier_radix_kernel, n1=n1),
        grid=(batch,),
        in_specs=group_specs + [
            seq_spec((ATTN_W + POOL_W) // FOURIER_W),
            _layer((MIX_GROUPS, GROUP_C, GROUP_C), l),
            _resident((GROUP_C, GROUP_C)), _resident((GROUP_C, GROUP_C)),
            _resident((RADIX, 2 * n1, n1))],
        out_specs=seq_spec(0),
        out_shape=jax.ShapeDtypeStruct((batch * n, FOURIER_W), BF16),
        scratch_shapes=([pltpu.VMEM((RADIX, n1, FOURIER_W), BF16)]
                        + [pltpu.VMEM((n, 2 * GROUP_C), BF16)] * MIX_GROUPS
                        + [pltpu.VMEM((RADIX, n1, GROUP_C), F32)] * (2 * MIX_GROUPS)),
        compiler_params=_params(1),
        name="fourier_radix",
    )(*([uf] * MIX_GROUPS), g, fourier_w, cc, sc, cs)


def _outproj_kernel(*refs, tiles_per_seq, seq_len, per_batch_mod, final):
    (att_ref, up_ref, prev_ref, next_ref, gp_ref, four_ref, w_ref, pw_ref, ps_ref,
     x_ref, mod_ref) = refs[:11]
    fg_ref = refs[11] if final else None
    o_ref, pad_ref, s2_ref, s4_ref, s8_ref, pooled_ref = refs[-6:]
    step = pl.program_id(0)
    pool_steps = _pool_steps(up_ref, prev_ref, next_ref, gp_ref, pw_ref, ps_ref, pooled_ref,
                             pad_ref, s2_ref, s4_ref, s8_ref, tile_in_seq=step % tiles_per_seq,
                             tiles_per_seq=tiles_per_seq, seq_len=seq_len)
    col_chunks = [slice(c0, c0 + OUT_CHUNK) for c0 in range(0, D_MODEL, OUT_CHUNK)]

    pool_steps.pop(0)()
    for cols in col_chunks:
        o_ref[:, cols] = (_dot(att_ref[...], w_ref[0:ATTN_W, cols])
                          + _dot(four_ref[...], w_ref[ATTN_W + POOL_W:D_MODEL, cols]))
        if pool_steps:
            pool_steps.pop(0)()
    for pool_step in pool_steps:
        pool_step()

    row = step // tiles_per_seq if per_batch_mod else CTX_MOD_ROW
    sum_sq = None
    for cols in col_chunks:
        gate = mod_ref[pl.ds(row, 1), 2 * D_MODEL + cols.start:2 * D_MODEL + cols.stop]
        out = o_ref[:, cols] + _dot(pooled_ref[...], w_ref[ATTN_W:ATTN_W + POOL_W, cols])
        y = x_ref[:, cols] + gate * out
        o_ref[:, cols] = y
        if final:
            part = jnp.sum(y * y, axis=-1, keepdims=True)
            sum_sq = part if sum_sq is None else sum_sq + part
    if final:
        inv_rms = lax.rsqrt(sum_sq * (1.0 / D_MODEL) + EPS)
        o_ref[...] = (o_ref[...] * inv_rms) * fg_ref[...]


def _outproj(att, up, g, four, w_out, pool_w, pool_scale, x2d, mod, final_g, l,
             *, tm, seq_len, per_batch_mod):
    rows = x2d.shape[0]
    final = final_g is not None
    tiles_per_seq = seq_len // tm
    halo_blocks = tm // POOL_HALO
    row_spec = lambda w: pl.BlockSpec((tm, w), lambda i: (i, 0))
    halo_spec = lambda index: pl.BlockSpec((POOL_HALO, POOL_W), lambda i: (index(i), 0))
    in_specs = [row_spec(ATTN_W), row_spec(POOL_W),
                halo_spec(lambda i: jnp.maximum(i * halo_blocks - 1, 0)),
                halo_spec(lambda i: jnp.minimum((i + 1) * halo_blocks, rows // POOL_HALO - 1)),
                pl.BlockSpec((tm, POOL_W), lambda i: (i, ATTN_W // POOL_W)), row_spec(FOURIER_W),
                _layer((D_MODEL, D_MODEL), 0), _layer((MIX_GROUPS, GROUP_C, GROUP_C), l),
                _layer((1, POOL_W), l), row_spec(D_MODEL), _layer((MOD_ROWS, 3 * D_MODEL), l)]
    args = [att, up, up, up, g, four, w_out, pool_w, pool_scale, x2d, mod]
    if final:
        in_specs.append(_resident((1, D_MODEL)))
        args.append(final_g)
    return pl.pallas_call(
        functools.partial(_outproj_kernel, tiles_per_seq=tiles_per_seq, seq_len=seq_len,
                          per_batch_mod=per_batch_mod, final=final),
        grid=(rows // tm,),
        in_specs=in_specs,
        out_specs=row_spec(D_MODEL),
        out_shape=jax.ShapeDtypeStruct((rows, D_MODEL), F32),
        scratch_shapes=_pool_scratch(tm),
        compiler_params=_params(1),
        name="outproj_final" if final else "outproj",
    )(*args)


def _rope_tables(n):
    pos = jnp.arange(n, dtype=jnp.int32)
    row = (pos // GRID_W).astype(F32)
    col = (pos % GRID_W).astype(F32)
    inv = ROPE_THETA ** (-jnp.arange(0, AXIS_ROT, 2, dtype=F32) / AXIS_ROT)
    ang_r = row[:, None] * inv
    ang_c = col[:, None] * inv
    cos = jnp.concatenate([jnp.cos(ang_r), jnp.cos(ang_r), jnp.cos(ang_c), jnp.cos(ang_c)], axis=1)
    sin = jnp.concatenate([-jnp.sin(ang_r), jnp.sin(ang_r), -jnp.sin(ang_c), jnp.sin(ang_c)], axis=1)
    return cos, sin


def _dft_cos_sin(n):
    k = jnp.arange(n, dtype=jnp.int32)
    ang = ((k[:, None] * k[None, :]) % n).astype(F32) * (2.0 * math.pi / n)
    return jnp.cos(ang), jnp.sin(ang)


def _fourier_tables(n):
    cc, sc = _dft_cos_sin(GROUP_C)
    cn, sn = _dft_cos_sin(n)
    norm = 1.0 / math.sqrt(n * GROUP_C)
    return (cc * norm).astype(BF16), (sc * norm).astype(BF16), cn.astype(BF16), sn.astype(BF16)


def _fourier_radix_tables(n):
    n1 = n // RADIX
    cc, sc = _dft_cos_sin(GROUP_C)
    norm = 1.0 / math.sqrt(n * GROUP_C)
    k1 = jnp.arange(n1, dtype=jnp.int32)[None, :, None]
    pos = (RADIX * jnp.arange(n1, dtype=jnp.int32)[None, None, :]
           + jnp.arange(RADIX, dtype=jnp.int32)[:, None, None])
    ang = ((k1 * pos) % n).astype(F32) * (2.0 * math.pi / n)
    cs = jnp.concatenate([jnp.cos(ang), jnp.sin(ang)], axis=1).astype(BF16)
    return (cc * norm).astype(BF16), (sc * norm).astype(BF16), cs


def kernel(x, c, ctx, c_ctx, ada_w, ada_b, norm_g, w_in, q_norm_g, k_norm_g, pool_w, pool_scale,
           fourier_w, w_out, final_norm_g):
    B, N, _ = x.shape
    C = ctx.shape[1]
    tm = 512
    tm_ctx = 256

    cond = jnp.concatenate([c, c_ctx[None, :], jnp.zeros((MOD_ROWS - B - 1, D_MODEL), F32)], axis=0)
    mod = _adaln(cond, ada_w, ada_b)

    rope_tabs = _rope_tables(N)
    four_tabs = _fourier_radix_tables(N)
    four_tabs_ctx = _fourier_tables(C)
    w_in_b = w_in[0:1].astype(BF16)
    w_out_b = w_out[0:1].astype(BF16)
    ng = norm_g[:, None, :]
    qg = q_norm_g[:, None, :]
    kg = k_norm_g[:, None, :]
    ps = pool_scale[:, None, :]

    xl = x.reshape(B * N, D_MODEL)
    xc = ctx.reshape(B * C, D_MODEL)
    for l in range(DEPTH):
        last = l == DEPTH - 1
        if last:
            kc, vct = _ctx_kv(xc, mod, ng, w_in_b, kg, l, tm=tm_ctx, seq_len=C)
        else:
            (qc, kc, vct, upc, ufc, gc), _ = _inproj(xc, mod, ng, w_in_b, qg, kg, None, l,
                                                     tm=tm_ctx, seq_len=C, per_batch_mod=False)
        (q, k, vt, up, uf, g), next_weights = _inproj(
            xl, mod, ng, w_in_b, qg, kg, rope_tabs, l, tm=tm, seq_len=N, per_batch_mod=True,
            cast_next=() if last else (w_in, w_out))

        att = _attention(q, g, [(kc, vct, C), (k, vt, N)], batch=B, q_len=N, tq=256)
        four = _fourier_radix(uf, g, fourier_w, four_tabs, l, batch=B, n=N)
        xl_new = _outproj(att, up, g, four, w_out_b, pool_w, ps, xl, mod,
                          final_norm_g[None, :] if last else None, l,
                          tm=tm, seq_len=N, per_batch_mod=True)

        if not last:
            attc = _attention(qc, gc, [(kc, vct, C)], batch=B, q_len=C, tq=C)
            fourc = _fourier(ufc, gc, fourier_w, four_tabs_ctx, l, batch=B, n=C)
            xc = _outproj(attc, upc, gc, fourc, w_out_b, pool_w, ps, xc, mod, None, l,
                          tm=tm_ctx, seq_len=C, per_batch_mod=False)
            w_in_b, w_out_b = next_weights
        xl = xl_new
    return xl.reshape(B, N, D_MODEL)
```

```python
import functools
import math

import jax
import jax.numpy as jnp
from jax import lax
from jax.experimental import pallas as pl
from jax.experimental.pallas import tpu as pltpu

D_MODEL = 2048
DEPTH = 2
GRID_W = 64
HEAD_DIM = 128
ATTN_W = 1024
N_HEADS = 8
N_KV_HEADS = 2
GQA_GROUP = 4
KV_W = 256
POOL_W = 512
POOL_WINDOWS = (2, 4, 8, 16)
FOURIER_W = 512
MIX_GROUPS = 4
GROUP_C = 128
OFF_K = ATTN_W
OFF_V = OFF_K + KV_W
OFF_POOL = OFF_V + KV_W
OFF_FOURIER = OFF_POOL + POOL_W
OFF_GATE = OFF_FOURIER + FOURIER_W
IN_W = OFF_GATE + D_MODEL
ROPE_THETA = 10000.0
AXIS_ROT = HEAD_DIM // 2
EPS = 1e-6

MOD_ROWS = 16
CTX_MOD_ROW = 8
SUBLANES = 8
ROW_PARTS = 2
NORM_PIECES = 8
POOL_HALO = 8
KEY_CHUNK = 256
EXP_CHUNK = 256
ONES_ROWS = 16
V7X_VMEM_BYTES = 64 * 1024 * 1024
VMEM_LIMIT = V7X_VMEM_BYTES - 8 * 1024 * 1024

BF16 = jnp.bfloat16
F32 = jnp.float32


def _silu(x):
    return x * jax.nn.sigmoid(x)


def _dot(a, b):
    return jnp.dot(a, b, preferred_element_type=F32)


def _dot_nt(a, b):
    return lax.dot_general(a, b, (((1,), (1,)), ((), ())), preferred_element_type=F32)


def _resident(shape):
    zeros = (0,) * len(shape)
    return pl.BlockSpec(shape, lambda *_: zeros, pipeline_mode=pl.Buffered(1))


def _layer(shape, l, tail=None):
    index = (l,) + (tail or (0,) * len(shape))
    return pl.BlockSpec((None,) + tuple(shape), lambda *_: index, pipeline_mode=pl.Buffered(1))


def _params(n_grid):
    return pltpu.CompilerParams(dimension_semantics=("arbitrary",) * n_grid,
                                vmem_limit_bytes=VMEM_LIMIT)


def _adaln_kernel(c_ref, w_ref, b_ref, o_ref):
    a = _silu(c_ref[...]).astype(BF16)
    o_ref[...] = _dot(a, w_ref[...].astype(BF16)) + b_ref[...]


def _adaln_specs(l, tn, step):
    in_specs = [pl.BlockSpec((MOD_ROWS, D_MODEL), lambda *i: (0, 0)),
                pl.BlockSpec((None, D_MODEL, tn), lambda *i: (l, 0, step(*i))),
                pl.BlockSpec((None, 1, tn), lambda *i: (l, 0, step(*i)))]
    out_spec = pl.BlockSpec((None, MOD_ROWS, tn), lambda *i: (0, 0, step(*i)))
    return in_specs, out_spec, jax.ShapeDtypeStruct((1, MOD_ROWS, 3 * D_MODEL), F32)


def _adaln(cond, ada_w, ada_b, l):
    tn = 1024
    in_specs, out_spec, out_shape = _adaln_specs(l, tn, lambda j: j)
    return pl.pallas_call(
        _adaln_kernel,
        grid=(3 * D_MODEL // tn,),
        in_specs=in_specs,
        out_specs=out_spec,
        out_shape=out_shape,
        compiler_params=_params(1),
        name="adaln",
    )(cond, ada_w, ada_b)


def _modulated_norm(x_ref, mod_ref, ng_ref, row, rows=slice(None)):
    x = x_ref[rows, :]
    ms = jnp.mean(x * x, axis=-1, keepdims=True)
    shift = mod_ref[pl.ds(row, 1), 0:D_MODEL]
    scale = mod_ref[pl.ds(row, 1), D_MODEL:2 * D_MODEL]
    y = (x * lax.rsqrt(ms + EPS)) * ng_ref[...]
    return (y * (1.0 + scale) + shift).astype(BF16)


def _head_norm(p, gain):
    ms = jnp.mean(p * p, axis=-1, keepdims=True)
    return (p * lax.rsqrt(ms + EPS)) * gain


def _rope(y, cos, sin, swap_lo):
    partner = jnp.where(swap_lo, pltpu.roll(y, 96, 1), pltpu.roll(y, 32, 1))
    return y * cos + partner * sin


def _inproj_kernel(*refs, rope, tiles_per_batch, n_convert):
    n_in = 8 if rope else 6
    x_ref, mod_ref, ng_ref, w_ref, qg_ref, kg_ref = refs[:6]
    cos_ref, sin_ref = refs[6:8] if rope else (None, None)
    f32_weight_refs = refs[n_in:n_in + n_convert]
    q_ref, k_ref, vt_ref, up_ref, uf_ref, g_ref = refs[n_in + n_convert:n_in + n_convert + 6]
    bf16_weight_refs = refs[n_in + n_convert + 6:n_in + 2 * n_convert + 6]
    h_ref = refs[-1]
    for src_ref, dst_ref in zip(f32_weight_refs, bf16_weight_refs):
        dst_ref[...] = src_ref[...].astype(BF16)
    tm = x_ref.shape[0]
    row = CTX_MOD_ROW if tiles_per_batch is None else pl.program_id(0) // tiles_per_batch
    q_gain = qg_ref[...] * (HEAD_DIM ** -0.5 * math.log2(math.e))
    k_gain = kg_ref[...]
    chunk = GQA_GROUP * HEAD_DIM

    def norm_steps(rows):
        piece_rows = (rows.stop - rows.start) // NORM_PIECES
        pieces = [slice(r0, r0 + piece_rows) for r0 in range(rows.start, rows.stop, piece_rows)]

        def norm(piece):
            h_ref[piece, :] = _modulated_norm(x_ref, mod_ref, ng_ref, row, piece)

        return [functools.partial(norm, piece) for piece in pieces]

    def matmul_steps(rows):
        n_rows = rows.stop - rows.start
        if rope:
            cos, sin = cos_ref[rows, :], sin_ref[rows, :]
            lane = lax.broadcasted_iota(jnp.int32, (n_rows, HEAD_DIM), 1)
            swap_lo = (lane & (AXIS_ROT // 2)) == 0

        def finish_head(p, gain):
            y = _head_norm(p, gain)
            if rope:
                y = _rope(y, cos, sin, swap_lo)
            return y.astype(BF16)

        def project(c0, width):
            return _dot(h_ref[rows, :], w_ref[:, c0:c0 + width])

        def q_chunk(c0):
            p = project(c0, chunk)
            for hh in range(GQA_GROUP):
                lo = hh * HEAD_DIM
                q_ref[rows, c0 + lo:c0 + lo + HEAD_DIM] = finish_head(p[:, lo:lo + HEAD_DIM], q_gain)

        def kv_chunk():
            p = project(OFF_K, 2 * KV_W)
            for hh in range(N_KV_HEADS):
                lo = hh * HEAD_DIM
                k_ref[rows, lo:lo + HEAD_DIM] = finish_head(p[:, lo:lo + HEAD_DIM], k_gain)
            vt_ref[:, rows] = p[:, KV_W:2 * KV_W].T.astype(BF16)

        def up_chunk():
            up_ref[rows, :] = project(OFF_POOL, POOL_W)

        def uf_chunk():
            uf_ref[rows, :] = project(OFF_FOURIER, FOURIER_W)

        def gate_chunk(c0):
            g_ref[rows, c0:c0 + chunk] = project(OFF_GATE + c0, chunk)

        return ([functools.partial(q_chunk, c0) for c0 in range(0, ATTN_W, chunk)]
                + [kv_chunk, up_chunk, uf_chunk]
                + [functools.partial(gate_chunk, c0) for c0 in range(0, D_MODEL, chunk)])

    part_rows = tm // ROW_PARTS
    parts = [slice(r0, r0 + part_rows) for r0 in range(0, tm, part_rows)]
    for step in norm_steps(parts[0]):
        step()
    for index, rows in enumerate(parts):
        fill = norm_steps(parts[index + 1]) if index + 1 < len(parts) else []
        for step in matmul_steps(rows):
            step()
            if fill:
                fill.pop(0)()
        for step in fill:
            step()


def _vt_spec(tm, tiles_per_seq):
    return pl.BlockSpec((None, KV_W, tm), lambda i: (i // tiles_per_seq, 0, i % tiles_per_seq))


def _inproj(x2d, mod, norm_g, w_in, q_g, k_g, rope_tabs, l, *, tm, seq_len, per_batch_mod,
            cast_next=()):
    rows = x2d.shape[0]
    steps = rows // tm
    rope = rope_tabs is not None
    tiles_per_seq = seq_len // tm
    row_spec = lambda w: pl.BlockSpec((tm, w), lambda i: (i, 0))
    in_specs = [row_spec(D_MODEL), _layer((MOD_ROWS, 3 * D_MODEL), 0), _layer((1, D_MODEL), l),
                _layer((D_MODEL, IN_W), 0), _layer((1, HEAD_DIM), l), _layer((1, HEAD_DIM), l)]
    args = [x2d, mod, norm_g, w_in, q_g, k_g]
    if rope:
        tab_spec = pl.BlockSpec((tm, HEAD_DIM), lambda i: (i % tiles_per_seq, 0))
        in_specs += [tab_spec, tab_spec]
        args += list(rope_tabs)
    row_out = lambda w, dt: jax.ShapeDtypeStruct((rows, w), dt)
    out_specs = [row_spec(ATTN_W), row_spec(KV_W), _vt_spec(tm, tiles_per_seq),
                 row_spec(POOL_W), row_spec(FOURIER_W), row_spec(D_MODEL)]
    out_shape = [row_out(ATTN_W, BF16), row_out(KV_W, BF16),
                 jax.ShapeDtypeStruct((rows // seq_len, KV_W, seq_len), BF16),
                 row_out(POOL_W, F32), row_out(FOURIER_W, F32), row_out(D_MODEL, F32)]
    for param in cast_next:
        _, p_rows, p_cols = param.shape
        slab = (None, p_rows // steps, p_cols)
        in_specs.append(pl.BlockSpec(slab, lambda i: (l + 1, i, 0)))
        out_specs.append(pl.BlockSpec(slab, lambda i: (0, i, 0)))
        out_shape.append(jax.ShapeDtypeStruct((1, p_rows, p_cols), BF16))
        args.append(param)
    outs = pl.pallas_call(
        functools.partial(_inproj_kernel, rope=rope, n_convert=len(cast_next),
                          tiles_per_batch=tiles_per_seq if per_batch_mod else None),
        grid=(steps,),
        in_specs=in_specs,
        out_specs=out_specs,
        out_shape=out_shape,
        scratch_shapes=[pltpu.VMEM((tm, D_MODEL), BF16)],
        compiler_params=_params(1),
        name="inproj_rope" if rope else "inproj_ctx",
    )(*args)
    return outs[:6], outs[6:]


def _ctx_kv_kernel(x_ref, mod_ref, ng_ref, w_ref, kg_ref, k_ref, vt_ref):
    h = _modulated_norm(x_ref, mod_ref, ng_ref, CTX_MOD_ROW)
    p = _dot(h, w_ref[...])
    k_gain = kg_ref[...]
    for hh in range(N_KV_HEADS):
        lo = hh * HEAD_DIM
        k_ref[:, lo:lo + HEAD_DIM] = _head_norm(p[:, lo:lo + HEAD_DIM], k_gain).astype(BF16)
    vt_ref[...] = p[:, KV_W:2 * KV_W].T.astype(BF16)


def _ctx_kv(x2d, mod, norm_g, w_in, k_g, l, *, tm, seq_len):
    rows = x2d.shape[0]
    row_spec = lambda w: pl.BlockSpec((tm, w), lambda i: (i, 0))
    kv_cols = (0, OFF_K // (2 * KV_W))
    return pl.pallas_call(
        _ctx_kv_kernel,
        grid=(rows // tm,),
        in_specs=[row_spec(D_MODEL), _layer((MOD_ROWS, 3 * D_MODEL), 0), _layer((1, D_MODEL), l),
                  _layer((D_MODEL, 2 * KV_W), 0, kv_cols), _layer((1, HEAD_DIM), l)],
        out_specs=[row_spec(KV_W), _vt_spec(tm, seq_len // tm)],
        out_shape=[jax.ShapeDtypeStruct((rows, KV_W), BF16),
                   jax.ShapeDtypeStruct((rows // seq_len, KV_W, seq_len), BF16)],
        compiler_params=_params(1),
        name="ctx_kv",
    )(x2d, mod, norm_g, w_in, k_g)


def _attn_kernel(*refs, lengths, tq, with_adaln):
    n_sources = len(lengths)
    q_ref, g_ref = refs[0], refs[1]
    k_refs = refs[2:2 + 2 * n_sources:2]
    vt_refs = refs[3:3 + 2 * n_sources:2]
    n_in = 2 + 2 * n_sources
    o_ref, s_ref, p_ref, vta_ref = refs[-4:]
    if with_adaln:
        _adaln_kernel(*refs[n_in:n_in + 4])
    offsets = [sum(lengths[:i]) for i in range(n_sources)]
    total = sum(lengths)
    n_tiles = q_ref.shape[0] // tq
    key_chunks = [(k_ref, off, r0, min(KEY_CHUNK, length))
                  for k_ref, off, length in zip(k_refs, offsets, lengths)
                  for r0 in range(0, length, KEY_CHUNK)]
    head_cols = [slice(hh * HEAD_DIM, (hh + 1) * HEAD_DIM) for hh in range(GQA_GROUP)]

    def q_rows(tile):
        return pl.ds(pl.multiple_of(tile * tq, tq), tq)

    def fold_rows(x, op):
        return op(x.reshape(x.shape[0] // SUBLANES, SUBLANES, tq), axis=0)

    def score_chunk(tile, hh, slot, chunk, col_max):
        k_ref, off, r0, size = key_chunks[chunk]
        s = _dot_nt(k_ref[r0:r0 + size, :], q_ref[q_rows(tile), head_cols[hh]])
        s_ref[slot, off + r0:off + r0 + size, :] = s
        cm = fold_rows(s, jnp.max)
        return cm if col_max is None else jnp.maximum(col_max, cm)

    def prob_chunk(slot, chunk, col_max):
        _, off, start, size = key_chunks[chunk]
        for r0 in range(off + start, off + start + size, EXP_CHUNK):
            p = jnp.exp2(s_ref[slot, r0:r0 + EXP_CHUNK, :] - col_max)
            p_ref[slot, r0:r0 + EXP_CHUNK, :] = p.astype(BF16)

    def finish(tile, hh, slot):
        acc = _dot(vta_ref[...], p_ref[slot])
        out = (acc[:HEAD_DIM] / acc[HEAD_DIM:HEAD_DIM + 1]).T
        gate = _silu(g_ref[q_rows(tile), head_cols[hh]])
        o_ref[q_rows(tile), head_cols[hh]] = (out * gate).astype(BF16)

    for vt_ref, off, length in zip(vt_refs, offsets, lengths):
        vta_ref[0:HEAD_DIM, off:off + length] = vt_ref[...]
    vta_ref[HEAD_DIM:HEAD_DIM + ONES_ROWS, :] = jnp.ones((ONES_ROWS, total), BF16)

    n_chunks = len(key_chunks)
    first_max = None
    for chunk in range(n_chunks):
        first_max = score_chunk(0, 0, 0, chunk, first_max)
    first_max = jnp.max(first_max, axis=0, keepdims=True)

    def tile_body(tile, cur_max):
        for hh in range(GQA_GROUP):
            slot, next_slot = hh % 2, (hh + 1) % 2
            next_hh = (hh + 1) % GQA_GROUP
            next_tile = tile if next_hh else jnp.minimum(tile + 1, n_tiles - 1)
            next_max = None
            for chunk in range(n_chunks):
                next_max = score_chunk(next_tile, next_hh, next_slot, chunk, next_max)
                prob_chunk(slot, chunk, cur_max)
            finish(tile, hh, slot)
            cur_max = jnp.max(next_max, axis=0, keepdims=True)
        return cur_max

    lax.fori_loop(0, n_tiles, tile_body, first_max)


def _attention(q, g, sources, *, batch, q_len, tq, adaln_next=None):
    q_spec = pl.BlockSpec((q_len, GQA_GROUP * HEAD_DIM), lambda b, h: (b, h))
    in_specs = [q_spec, q_spec]
    args = [q, g]
    for k, vt, length in sources:
        in_specs += [pl.BlockSpec((length, HEAD_DIM), lambda b, h: (b, h)),
                     pl.BlockSpec((None, HEAD_DIM, length), lambda b, h: (b, h, 0))]
        args += [k, vt]
    out_specs = [q_spec]
    out_shape = [jax.ShapeDtypeStruct((batch * q_len, ATTN_W), BF16)]
    if adaln_next is not None:
        cond, ada_w, ada_b, layer = adaln_next
        slab = 3 * D_MODEL // (batch * N_KV_HEADS)
        side_in, side_out, side_shape = _adaln_specs(layer, slab, lambda b, h: b * N_KV_HEADS + h)
        in_specs += side_in
        args += [cond, ada_w, ada_b]
        out_specs = [side_out] + out_specs
        out_shape = [side_shape] + out_shape
    lengths = tuple(length for _, _, length in sources)
    total = sum(lengths)
    outs = pl.pallas_call(
        functools.partial(_attn_kernel, lengths=lengths, tq=tq, with_adaln=adaln_next is not None),
        grid=(batch, N_KV_HEADS),
        in_specs=in_specs,
        out_specs=out_specs,
        out_shape=out_shape,
        scratch_shapes=[pltpu.VMEM((2, total, tq), F32), pltpu.VMEM((2, total, tq), BF16),
                        pltpu.VMEM((HEAD_DIM + ONES_ROWS, total), BF16)],
        compiler_params=_params(2),
        name="attention",
    )(*args)
    return outs[-1], (outs[0] if adaln_next is not None else None)


def _short_attn_kernel(q_ref, g_ref, k_ref, vt_ref, o_ref):
    length = q_ref.shape[0]
    heads = [slice(hh * HEAD_DIM, (hh + 1) * HEAD_DIM) for hh in range(GQA_GROUP)]
    q_rows = jnp.concatenate([q_ref[:, cols] for cols in heads], axis=0)
    s = _dot_nt(k_ref[...], q_rows)
    p = jnp.exp2(s - jnp.max(s, axis=0, keepdims=True)).astype(BF16)
    vt_aug = jnp.concatenate([vt_ref[...], jnp.ones((ONES_ROWS, length), BF16)], axis=0)
    acc = _dot(vt_aug, p)
    for hh, cols in enumerate(heads):
        span = slice(hh * length, (hh + 1) * length)
        out = (acc[:HEAD_DIM, span] / acc[HEAD_DIM:HEAD_DIM + 1, span]).T
        o_ref[:, cols] = (out * _silu(g_ref[:, cols])).astype(BF16)


def _short_attention(q, g, k, vt, *, batch, length):
    q_spec = pl.BlockSpec((length, GQA_GROUP * HEAD_DIM), lambda b, h: (b, h))
    return pl.pallas_call(
        _short_attn_kernel,
        grid=(batch, N_KV_HEADS),
        in_specs=[q_spec, q_spec, pl.BlockSpec((length, HEAD_DIM), lambda b, h: (b, h)),
                  pl.BlockSpec((None, HEAD_DIM, length), lambda b, h: (b, h, 0))],
        out_specs=q_spec,
        out_shape=jax.ShapeDtypeStruct((batch * length, ATTN_W), BF16),
        compiler_params=_params(2),
        name="short_attention",
    )(q, g, k, vt)


POOL_CHUNK = 256
OUT_CHUNK = 256


def _pool_steps(up_ref, prev_ref, next_ref, g_ref, pw_ref, ps_ref, dst_ref,
                pad_ref, s2_ref, s4_ref, s8_ref, *, tile_in_seq, tiles_per_seq, seq_len):
    tm = up_ref.shape[0]
    halo = POOL_HALO
    chunk = min(tm, POOL_CHUNK)

    def forward_sum(dst, src, shift, length, lane0):
        for r0 in range(0, length, chunk):
            rows = min(chunk, length - r0)
            dst[r0:r0 + rows, :] = (src[r0:r0 + rows, lane0:]
                                    + src[r0 + shift:r0 + shift + rows, lane0:])

    def window_sums():
        pad_ref[0:halo, :] = jnp.where(tile_in_seq > 0, prev_ref[...], 0.0)
        pad_ref[halo:halo + tm, :] = up_ref[...]
        pad_ref[halo + tm:2 * halo + tm, :] = jnp.where(tile_in_seq < tiles_per_seq - 1,
                                                        next_ref[...], 0.0)
        pad_ref[2 * halo + tm:4 * halo + tm, :] = jnp.zeros((2 * halo, POOL_W), F32)
        forward_sum(s2_ref, pad_ref, 1, tm + 3 * halo, 0)
        forward_sum(s4_ref, s2_ref, 2, tm + 2 * halo, GROUP_C)
        forward_sum(s8_ref, s4_ref, 4, tm + halo, GROUP_C)

    def window_sum(gi, r0):
        if gi == 0:
            return s2_ref[halo - 1 + r0:halo - 1 + r0 + chunk, 0:GROUP_C]
        if gi == 1:
            return s4_ref[halo - 2 + r0:halo - 2 + r0 + chunk, 0:GROUP_C]
        if gi == 2:
            return s8_ref[halo - 4 + r0:halo - 4 + r0 + chunk, 0:GROUP_C]
        return (s8_ref[r0:r0 + chunk, GROUP_C:] + s8_ref[halo + r0:halo + r0 + chunk, GROUP_C:])

    def edge_inverse_count(win, local_row):
        before, after = win // 2, win - win // 2 - 1
        t = tile_in_seq * tm + local_row + lax.broadcasted_iota(jnp.int32, (halo, GROUP_C), 0)
        cnt = jnp.minimum(t + after, seq_len - 1) - jnp.maximum(t - before, 0) + 1
        return 1.0 / cnt.astype(F32)

    def pooled_block(gi, win, r0):
        cols = slice(gi * GROUP_C, (gi + 1) * GROUP_C)
        pieces, inner = [], chunk
        if r0 == 0:
            pieces.append(edge_inverse_count(win, 0))
            inner -= halo
        tail = [edge_inverse_count(win, tm - halo)] if r0 + chunk == tm else []
        inner -= halo * len(tail)
        pieces.append(jnp.full((inner, GROUP_C), 1.0 / win, F32))
        inv_cnt = jnp.concatenate(pieces + tail, axis=0)
        pooled = window_sum(gi, r0) * inv_cnt - pad_ref[halo + r0:halo + r0 + chunk, cols]
        y = _dot(pooled.astype(BF16), pw_ref[gi].astype(BF16)) * ps_ref[:, cols]
        dst_ref[r0:r0 + chunk, cols] = (y * _silu(g_ref[r0:r0 + chunk, cols])).astype(BF16)

    return [window_sums] + [functools.partial(pooled_block, gi, win, r0)
                            for gi, win in enumerate(POOL_WINDOWS)
                            for r0 in range(0, tm, chunk)]


def _pool_scratch(tm):
    return [pltpu.VMEM((tm + 4 * POOL_HALO, POOL_W), F32),
            pltpu.VMEM((tm + 3 * POOL_HALO, POOL_W), F32),
            pltpu.VMEM((tm + 2 * POOL_HALO, POOL_W - GROUP_C), F32),
            pltpu.VMEM((tm + POOL_HALO, POOL_W - 2 * GROUP_C), F32),
            pltpu.VMEM((tm, POOL_W), BF16)]


def _fourier_kernel(u_ref, g_ref, fw_ref, cc_ref, sc_ref, cn_ref, sn_ref, o_ref, ua_ref, ub_ref,
                    *, n, chunk):
    for gi in range(MIX_GROUPS):
        cols = slice(gi * GROUP_C, (gi + 1) * GROUP_C)
        w = fw_ref[gi].astype(BF16)
        a = _dot(cc_ref[...], w).astype(BF16)
        b = _dot(sc_ref[...], w).astype(BF16)
        ug = u_ref[:, cols].astype(BF16)
        ua_ref[:, cols] = _dot(ug, a).astype(BF16)
        ub_ref[:, cols] = _dot(ug, b).astype(BF16)
    for r0 in range(0, n, chunk):
        rows = slice(r0, r0 + chunk)
        y = _dot(cn_ref[rows, :], ua_ref[...]) - _dot(sn_ref[rows, :], ub_ref[...])
        o_ref[rows, :] = (y * _silu(g_ref[rows, :])).astype(BF16)


def _fourier(uf, g, fourier_w, tabs, l, *, batch, n):
    cc, sc, cn, sn = tabs
    seq_spec = lambda col: pl.BlockSpec((n, FOURIER_W), lambda b: (b, col))
    return pl.pallas_call(
        functools.partial(_fourier_kernel, n=n, chunk=min(n, 512)),
        grid=(batch,),
        in_specs=[seq_spec(0), seq_spec((ATTN_W + POOL_W) // FOURIER_W),
                  _layer((MIX_GROUPS, GROUP_C, GROUP_C), l),
                  _resident((GROUP_C, GROUP_C)), _resident((GROUP_C, GROUP_C)),
                  _resident((n, n)), _resident((n, n))],
        out_specs=seq_spec(0),
        out_shape=jax.ShapeDtypeStruct((batch * n, FOURIER_W), BF16),
        scratch_shapes=[pltpu.VMEM((n, FOURIER_W), BF16), pltpu.VMEM((n, FOURIER_W), BF16)],
        compiler_params=_params(1),
        name="fourier",
    )(uf, g, fourier_w, cc, sc, cn, sn)


RADIX = 8
COMBINE_ROWS = 16
SQRT_HALF = math.sqrt(0.5)


def _radix8_real(yr, yn):
    sr = [yr[n] + yr[n + 4] for n in range(4)]
    dr = [yr[n] - yr[n + 4] for n in range(4)]
    sn = {n: yn[n] + yn[n + 4] for n in (1, 3)}
    dn = {n: yn[n] - yn[n + 4] for n in (1, 2, 3)}
    t1r, t1i = (dr[1] - dn[1]) * SQRT_HALF, (dr[1] + dn[1]) * -SQRT_HALF
    t3r, t3i = (dr[3] + dn[3]) * -SQRT_HALF, (dn[3] - dr[3]) * SQRT_HALF
    out = [None] * RADIX
    ea, eb = sr[0] + sr[2], sr[1] + sr[3]
    ec, ed = sr[0] - sr[2], sn[3] - sn[1]
    out[0], out[4] = ea + eb, ea - eb
    out[2], out[6] = ec + ed, ec - ed
    oa, ob = dr[0] - dn[2], t1r + t3r
    oc, od = dr[0] + dn[2], t1i - t3i
    out[1], out[5] = oa + ob, oa - ob
    out[3], out[7] = oc + od, oc - od
    return out


def _fourier_radix_kernel(*refs, n1):
    u_refs = refs[:MIX_GROUPS]
    g_ref, fw_ref, cc_ref, sc_ref, cs_ref, o_ref, ub_ref = refs[MIX_GROUPS:MIX_GROUPS + 7]
    scratch = refs[MIX_GROUPS + 7:]
    pq_refs, yr_refs, yn_refs = (scratch[i * MIX_GROUPS:(i + 1) * MIX_GROUPS] for i in range(3))
    group_lanes = [slice(gi * GROUP_C, (gi + 1) * GROUP_C) for gi in range(MIX_GROUPS)]

    for n2 in range(RADIX):
        for gi, lanes in enumerate(group_lanes):
            ub_ref[n2, :, lanes] = u_refs[gi][pl.ds(n2, n1, stride=RADIX), :].astype(BF16)
        pq = _dot(cs_ref[n2], ub_ref[n2])
        rows = slice(n2 * n1, (n2 + 1) * n1)
        for gi, lanes in enumerate(group_lanes):
            pq_refs[gi][rows, :GROUP_C] = pq[:n1, lanes].astype(BF16)
            pq_refs[gi][rows, GROUP_C:] = pq[n1:, lanes].astype(BF16)

    for gi, lanes in enumerate(group_lanes):
        w = fw_ref[gi].astype(BF16)
        a, b = _dot(cc_ref[...], w), _dot(sc_ref[...], w)
        mix = jnp.concatenate([jnp.concatenate([a, b], axis=1),
                               jnp.concatenate([-b, a], axis=1)], axis=0).astype(BF16)
        y = _dot(pq_refs[gi][...], mix)
        for n2 in range(RADIX):
            yr_refs[gi][n2] = y[n2 * n1:(n2 + 1) * n1, :GROUP_C]
            yn_refs[gi][n2] = y[n2 * n1:(n2 + 1) * n1, GROUP_C:]

        for r0 in range(0, n1, COMBINE_ROWS):
            rows = slice(r0, r0 + COMBINE_ROWS)
            yr = [yr_refs[gi][n2, rows, :] for n2 in range(RADIX)]
            yn = [yn_refs[gi][n2, rows, :] for n2 in range(RADIX)]
            for k2, val in enumerate(_radix8_real(yr, yn)):
                out_rows = slice(k2 * n1 + r0, k2 * n1 + r0 + COMBINE_ROWS)
                o_ref[out_rows, lanes] = (val * _silu(g_ref[out_rows, lanes])).astype(BF16)


def _fourier_radix(uf, g, fourier_w, tabs, l, *, batch, n):
    cc, sc, cs = tabs
    n1 = n // RADIX
    seq_spec = lambda col: pl.BlockSpec((n, FOURIER_W), lambda b: (b, col))
    group_specs = [pl.BlockSpec((n, GROUP_C), lambda b, gi=gi: (b, gi)) for gi in range(MIX_GROUPS)]
    return pl.pallas_call(
        functools.partial(_fourier_radix_kernel, n1=n1),
        grid=(batch,),
        in_specs=group_specs + [
            seq_spec((ATTN_W + POOL_W) // FOURIER_W),
            _layer((MIX_GROUPS, GROUP_C, GROUP_C), l),
            _resident((GROUP_C, GROUP_C)), _resident((GROUP_C, GROUP_C)),
            _resident((RADIX, 2 * n1, n1))],
        out_specs=seq_spec(0),
        out_shape=jax.ShapeDtypeStruct((batch * n, FOURIER_W), BF16),
        scratch_shapes=([pltpu.VMEM((RADIX, n1, FOURIER_W), BF16)]
                        + [pltpu.VMEM((n, 2 * GROUP_C), BF16)] * MIX_GROUPS
                        + [pltpu.VMEM((RADIX, n1, GROUP_C), F32)] * (2 * MIX_GROUPS)),
        compiler_params=_params(1),
        name="fourier_radix",
    )(*([uf] * MIX_GROUPS), g, fourier_w, cc, sc, cs)


def _outproj_kernel(*refs, tiles_per_seq, seq_len, per_batch_mod, final):
    (att_ref, up_ref, prev_ref, next_ref, gp_ref, four_ref, w_ref, pw_ref, ps_ref,
     x_ref, mod_ref) = refs[:11]
    fg_ref = refs[11] if final else None
    o_ref, pad_ref, s2_ref, s4_ref, s8_ref, pooled_ref = refs[-6:]
    step = pl.program_id(0)
    pool_steps = _pool_steps(up_ref, prev_ref, next_ref, gp_ref, pw_ref, ps_ref, pooled_ref,
                             pad_ref, s2_ref, s4_ref, s8_ref, tile_in_seq=step % tiles_per_seq,
                             tiles_per_seq=tiles_per_seq, seq_len=seq_len)
    col_chunks = [slice(c0, c0 + OUT_CHUNK) for c0 in range(0, D_MODEL, OUT_CHUNK)]

    pool_steps.pop(0)()
    for cols in col_chunks:
        o_ref[:, cols] = (_dot(att_ref[...], w_ref[0:ATTN_W, cols])
                          + _dot(four_ref[...], w_ref[ATTN_W + POOL_W:D_MODEL, cols]))
        if pool_steps:
            pool_steps.pop(0)()
    for pool_step in pool_steps:
        pool_step()

    row = step // tiles_per_seq if per_batch_mod else CTX_MOD_ROW
    sum_sq = None
    for cols in col_chunks:
        gate = mod_ref[pl.ds(row, 1), 2 * D_MODEL + cols.start:2 * D_MODEL + cols.stop]
        out = o_ref[:, cols] + _dot(pooled_ref[...], w_ref[ATTN_W:ATTN_W + POOL_W, cols])
        y = x_ref[:, cols] + gate * out
        o_ref[:, cols] = y
        if final:
            part = jnp.sum(y * y, axis=-1, keepdims=True)
            sum_sq = part if sum_sq is None else sum_sq + part
    if final:
        inv_rms = lax.rsqrt(sum_sq * (1.0 / D_MODEL) + EPS)
        o_ref[...] = (o_ref[...] * inv_rms) * fg_ref[...]


def _outproj(att, up, g, four, w_out, pool_w, pool_scale, x2d, mod, final_g, l,
             *, tm, seq_len, per_batch_mod):
    rows = x2d.shape[0]
    final = final_g is not None
    tiles_per_seq = seq_len // tm
    halo_blocks = tm // POOL_HALO
    row_spec = lambda w: pl.BlockSpec((tm, w), lambda i: (i, 0))
    halo_spec = lambda index: pl.BlockSpec((POOL_HALO, POOL_W), lambda i: (index(i), 0))
    in_specs = [row_spec(ATTN_W), row_spec(POOL_W),
                halo_spec(lambda i: jnp.maximum(i * halo_blocks - 1, 0)),
                halo_spec(lambda i: jnp.minimum((i + 1) * halo_blocks, rows // POOL_HALO - 1)),
                pl.BlockSpec((tm, POOL_W), lambda i: (i, ATTN_W // POOL_W)), row_spec(FOURIER_W),
                _layer((D_MODEL, D_MODEL), 0), _layer((MIX_GROUPS, GROUP_C, GROUP_C), l),
                _layer((1, POOL_W), l), row_spec(D_MODEL), _layer((MOD_ROWS, 3 * D_MODEL), 0)]
    args = [att, up, up, up, g, four, w_out, pool_w, pool_scale, x2d, mod]
    if final:
        in_specs.append(_resident((1, D_MODEL)))
        args.append(final_g)
    return pl.pallas_call(
        functools.partial(_outproj_kernel, tiles_per_seq=tiles_per_seq, seq_len=seq_len,
                          per_batch_mod=per_batch_mod, final=final),
        grid=(rows // tm,),
        in_specs=in_specs,
        out_specs=row_spec(D_MODEL),
        out_shape=jax.ShapeDtypeStruct((rows, D_MODEL), F32),
        scratch_shapes=_pool_scratch(tm),
        compiler_params=_params(1),
        name="outproj_final" if final else "outproj",
    )(*args)


def _rope_tables(n):
    pos = jnp.arange(n, dtype=jnp.int32)
    row = (pos // GRID_W).astype(F32)
    col = (pos % GRID_W).astype(F32)
    inv = ROPE_THETA ** (-jnp.arange(0, AXIS_ROT, 2, dtype=F32) / AXIS_ROT)
    ang_r = row[:, None] * inv
    ang_c = col[:, None] * inv
    cos = jnp.concatenate([jnp.cos(ang_r), jnp.cos(ang_r), jnp.cos(ang_c), jnp.cos(ang_c)], axis=1)
    sin = jnp.concatenate([-jnp.sin(ang_r), jnp.sin(ang_r), -jnp.sin(ang_c), jnp.sin(ang_c)], axis=1)
    return cos, sin


def _dft_cos_sin(n):
    k = jnp.arange(n, dtype=jnp.int32)
    ang = ((k[:, None] * k[None, :]) % n).astype(F32) * (2.0 * math.pi / n)
    return jnp.cos(ang), jnp.sin(ang)


def _fourier_tables(n):
    cc, sc = _dft_cos_sin(GROUP_C)
    cn, sn = _dft_cos_sin(n)
    norm = 1.0 / math.sqrt(n * GROUP_C)
    return (cc * norm).astype(BF16), (sc * norm).astype(BF16), cn.astype(BF16), sn.astype(BF16)


def _fourier_radix_tables(n):
    n1 = n // RADIX
    cc, sc = _dft_cos_sin(GROUP_C)
    norm = 1.0 / math.sqrt(n * GROUP_C)
    k1 = jnp.arange(n1, dtype=jnp.int32)[None, :, None]
    pos = (RADIX * jnp.arange(n1, dtype=jnp.int32)[None, None, :]
           + jnp.arange(RADIX, dtype=jnp.int32)[:, None, None])
    ang = ((k1 * pos) % n).astype(F32) * (2.0 * math.pi / n)
    cs = jnp.concatenate([jnp.cos(ang), jnp.sin(ang)], axis=1).astype(BF16)
    return (cc * norm).astype(BF16), (sc * norm).astype(BF16), cs


def kernel(x, c, ctx, c_ctx, ada_w, ada_b, norm_g, w_in, q_norm_g, k_norm_g, pool_w, pool_scale,
           fourier_w, w_out, final_norm_g):
    B, N, _ = x.shape
    C = ctx.shape[1]
    tm = 512
    tm_ctx = 256

    cond = jnp.concatenate([c, c_ctx[None, :], jnp.zeros((MOD_ROWS - B - 1, D_MODEL), F32)], axis=0)
    ada_b3 = ada_b.reshape(DEPTH, 1, 3 * D_MODEL)
    mod = _adaln(cond, ada_w, ada_b3, 0)

    rope_tabs = _rope_tables(N)
    four_tabs = _fourier_radix_tables(N)
    four_tabs_ctx = _fourier_tables(C)
    w_in_b = w_in[0:1].astype(BF16)
    w_out_b = w_out[0:1].astype(BF16)
    ng = norm_g[:, None, :]
    qg = q_norm_g[:, None, :]
    kg = k_norm_g[:, None, :]
    ps = pool_scale[:, None, :]

    xl = x.reshape(B * N, D_MODEL)
    xc = ctx.reshape(B * C, D_MODEL)
    for l in range(DEPTH):
        last = l == DEPTH - 1
        if last:
            kc, vct = _ctx_kv(xc, mod, ng, w_in_b, kg, l, tm=tm_ctx, seq_len=C)
        else:
            (qc, kc, vct, upc, ufc, gc), _ = _inproj(xc, mod, ng, w_in_b, qg, kg, None, l,
                                                     tm=tm_ctx, seq_len=C, per_batch_mod=False)
        (q, k, vt, up, uf, g), next_weights = _inproj(
            xl, mod, ng, w_in_b, qg, kg, rope_tabs, l, tm=tm, seq_len=N, per_batch_mod=True,
            cast_next=() if last else (w_in, w_out))

        att, next_mod = _attention(q, g, [(kc, vct, C), (k, vt, N)], batch=B, q_len=N, tq=256,
                                   adaln_next=None if last else (cond, ada_w, ada_b3, l + 1))
        four = _fourier_radix(uf, g, fourier_w, four_tabs, l, batch=B, n=N)
        xl_new = _outproj(att, up, g, four, w_out_b, pool_w, ps, xl, mod,
                          final_norm_g[None, :] if last else None, l,
                          tm=tm, seq_len=N, per_batch_mod=True)

        if not last:
            attc = _short_attention(qc, gc, kc, vct, batch=B, length=C)
            fourc = _fourier(ufc, gc, fourier_w, four_tabs_ctx, l, batch=B, n=C)
            xc = _outproj(attc, upc, gc, fourc, w_out_b, pool_w, ps, xc, mod, None, l,
                          tm=tm_ctx, seq_len=C, per_batch_mod=False)
            w_in_b, w_out_b = next_weights
            mod = next_mod
        xl = xl_new
    return xl.reshape(B, N, D_MODEL)
```

```python
import functools
import math

import jax
import jax.numpy as jnp
from jax import lax
from jax.experimental import pallas as pl
from jax.experimental.pallas import tpu as pltpu

D_MODEL = 2048
DEPTH = 2
GRID_W = 64
HEAD_DIM = 128
ATTN_W = 1024
N_HEADS = 8
N_KV_HEADS = 2
GQA_GROUP = 4
KV_W = 256
POOL_W = 512
POOL_WINDOWS = (2, 4, 8, 16)
FOURIER_W = 512
MIX_GROUPS = 4
GROUP_C = 128
OFF_K = ATTN_W
OFF_V = OFF_K + KV_W
OFF_POOL = OFF_V + KV_W
OFF_FOURIER = OFF_POOL + POOL_W
OFF_GATE = OFF_FOURIER + FOURIER_W
IN_W = OFF_GATE + D_MODEL
ROPE_THETA = 10000.0
AXIS_ROT = HEAD_DIM // 2
EPS = 1e-6

MOD_ROWS = 16
CTX_MOD_ROW = 8
SUBLANES = 8
ROW_PARTS = 2
NORM_PIECES = 8
POOL_HALO = 8
KEY_CHUNK = 256
EXP_CHUNK = 256
ONES_ROWS = 16
V7X_VMEM_BYTES = 64 * 1024 * 1024
VMEM_LIMIT = V7X_VMEM_BYTES - 8 * 1024 * 1024

BF16 = jnp.bfloat16
F32 = jnp.float32


def _silu(x):
    return x * jax.nn.sigmoid(x)


def _dot(a, b):
    return jnp.dot(a, b, preferred_element_type=F32)


def _dot_nt(a, b):
    return lax.dot_general(a, b, (((1,), (1,)), ((), ())), preferred_element_type=F32)


def _resident(shape):
    zeros = (0,) * len(shape)
    return pl.BlockSpec(shape, lambda *_: zeros, pipeline_mode=pl.Buffered(1))


def _layer(shape, l, tail=None):
    index = (l,) + (tail or (0,) * len(shape))
    return pl.BlockSpec((None,) + tuple(shape), lambda *_: index, pipeline_mode=pl.Buffered(1))


def _params(n_grid):
    return pltpu.CompilerParams(dimension_semantics=("arbitrary",) * n_grid,
                                vmem_limit_bytes=VMEM_LIMIT)


def _adaln_kernel(c_ref, w_ref, b_ref, o_ref):
    a = _silu(c_ref[...]).astype(BF16)
    o_ref[...] = _dot(a, w_ref[...].astype(BF16)) + b_ref[...]


def _adaln_specs(l, tn, step):
    in_specs = [pl.BlockSpec((MOD_ROWS, D_MODEL), lambda *i: (0, 0)),
                pl.BlockSpec((None, D_MODEL, tn), lambda *i: (l, 0, step(*i))),
                pl.BlockSpec((None, 1, tn), lambda *i: (l, 0, step(*i)))]
    out_spec = pl.BlockSpec((None, MOD_ROWS, tn), lambda *i: (0, 0, step(*i)))
    return in_specs, out_spec, jax.ShapeDtypeStruct((1, MOD_ROWS, 3 * D_MODEL), F32)


def _adaln(cond, ada_w, ada_b, l):
    tn = 1024
    in_specs, out_spec, out_shape = _adaln_specs(l, tn, lambda j: j)
    return pl.pallas_call(
        _adaln_kernel,
        grid=(3 * D_MODEL // tn,),
        in_specs=in_specs,
        out_specs=out_spec,
        out_shape=out_shape,
        compiler_params=_params(1),
        name="adaln",
    )(cond, ada_w, ada_b)


def _modulated_norm(x_ref, mod_ref, ng_ref, row, rows=slice(None)):
    x = x_ref[rows, :]
    ms = jnp.mean(x * x, axis=-1, keepdims=True)
    shift = mod_ref[pl.ds(row, 1), 0:D_MODEL]
    scale = mod_ref[pl.ds(row, 1), D_MODEL:2 * D_MODEL]
    y = (x * lax.rsqrt(ms + EPS)) * ng_ref[...]
    return (y * (1.0 + scale) + shift).astype(BF16)


def _head_norm(p, gain):
    ms = jnp.mean(p * p, axis=-1, keepdims=True)
    return (p * lax.rsqrt(ms + EPS)) * gain


def _rope(y, cos, sin, swap_lo):
    partner = jnp.where(swap_lo, pltpu.roll(y, 96, 1), pltpu.roll(y, 32, 1))
    return y * cos + partner * sin


def _inproj_kernel(*refs, rope, tiles_per_batch, n_convert):
    n_in = 8 if rope else 6
    x_ref, mod_ref, ng_ref, w_ref, qg_ref, kg_ref = refs[:6]
    cos_ref, sin_ref = refs[6:8] if rope else (None, None)
    f32_weight_refs = refs[n_in:n_in + n_convert]
    q_ref, k_ref, vt_ref, up_ref, uf_ref, g_ref = refs[n_in + n_convert:n_in + n_convert + 6]
    bf16_weight_refs = refs[n_in + n_convert + 6:n_in + 2 * n_convert + 6]
    h_ref = refs[-1]
    for src_ref, dst_ref in zip(f32_weight_refs, bf16_weight_refs):
        dst_ref[...] = src_ref[...].astype(BF16)
    tm = x_ref.shape[0]
    row = CTX_MOD_ROW if tiles_per_batch is None else pl.program_id(0) // tiles_per_batch
    q_gain = qg_ref[...] * (HEAD_DIM ** -0.5 * math.log2(math.e))
    k_gain = kg_ref[...]
    chunk = GQA_GROUP * HEAD_DIM

    def norm_steps(rows):
        piece_rows = (rows.stop - rows.start) // NORM_PIECES
        pieces = [slice(r0, r0 + piece_rows) for r0 in range(rows.start, rows.stop, piece_rows)]

        def norm(piece):
            h_ref[piece, :] = _modulated_norm(x_ref, mod_ref, ng_ref, row, piece)

        return [functools.partial(norm, piece) for piece in pieces]

    def matmul_steps(rows):
        n_rows = rows.stop - rows.start
        if rope:
            cos, sin = cos_ref[rows, :], sin_ref[rows, :]
            lane = lax.broadcasted_iota(jnp.int32, (n_rows, HEAD_DIM), 1)
            swap_lo = (lane & (AXIS_ROT // 2)) == 0

        def finish_head(p, gain):
            y = _head_norm(p, gain)
            if rope:
                y = _rope(y, cos, sin, swap_lo)
            return y.astype(BF16)

        def project(c0, width):
            return _dot(h_ref[rows, :], w_ref[:, c0:c0 + width])

        def q_chunk(c0):
            p = project(c0, chunk)
            for hh in range(GQA_GROUP):
                lo = hh * HEAD_DIM
                q_ref[rows, c0 + lo:c0 + lo + HEAD_DIM] = finish_head(p[:, lo:lo + HEAD_DIM], q_gain)

        def kv_chunk():
            p = project(OFF_K, 2 * KV_W)
            for hh in range(N_KV_HEADS):
                lo = hh * HEAD_DIM
                k_ref[rows, lo:lo + HEAD_DIM] = finish_head(p[:, lo:lo + HEAD_DIM], k_gain)
            vt_ref[:, rows] = p[:, KV_W:2 * KV_W].T.astype(BF16)

        def up_chunk():
            up_ref[rows, :] = project(OFF_POOL, POOL_W)

        def uf_chunk():
            uf_ref[rows, :] = project(OFF_FOURIER, FOURIER_W)

        def gate_chunk(c0):
            g_ref[rows, c0:c0 + chunk] = project(OFF_GATE + c0, chunk)

        return ([functools.partial(q_chunk, c0) for c0 in range(0, ATTN_W, chunk)]
                + [kv_chunk, up_chunk, uf_chunk]
                + [functools.partial(gate_chunk, c0) for c0 in range(0, D_MODEL, chunk)])

    part_rows = tm // ROW_PARTS
    parts = [slice(r0, r0 + part_rows) for r0 in range(0, tm, part_rows)]
    for step in norm_steps(parts[0]):
        step()
    for index, rows in enumerate(parts):
        fill = norm_steps(parts[index + 1]) if index + 1 < len(parts) else []
        for step in matmul_steps(rows):
            step()
            if fill:
                fill.pop(0)()
        for step in fill:
            step()


def _vt_spec(tm, tiles_per_seq):
    return pl.BlockSpec((None, KV_W, tm), lambda i: (i // tiles_per_seq, 0, i % tiles_per_seq))


def _inproj(x2d, mod, norm_g, w_in, q_g, k_g, rope_tabs, l, *, tm, seq_len, per_batch_mod,
            cast_next=()):
    rows = x2d.shape[0]
    steps = rows // tm
    rope = rope_tabs is not None
    tiles_per_seq = seq_len // tm
    row_spec = lambda w: pl.BlockSpec((tm, w), lambda i: (i, 0))
    in_specs = [row_spec(D_MODEL), _layer((MOD_ROWS, 3 * D_MODEL), 0), _layer((1, D_MODEL), l),
                _layer((D_MODEL, IN_W), 0), _layer((1, HEAD_DIM), l), _layer((1, HEAD_DIM), l)]
    args = [x2d, mod, norm_g, w_in, q_g, k_g]
    if rope:
        tab_spec = pl.BlockSpec((tm, HEAD_DIM), lambda i: (i % tiles_per_seq, 0))
        in_specs += [tab_spec, tab_spec]
        args += list(rope_tabs)
    row_out = lambda w, dt: jax.ShapeDtypeStruct((rows, w), dt)
    out_specs = [row_spec(ATTN_W), row_spec(KV_W), _vt_spec(tm, tiles_per_seq),
                 row_spec(POOL_W), row_spec(FOURIER_W), row_spec(D_MODEL)]
    out_shape = [row_out(ATTN_W, BF16), row_out(KV_W, BF16),
                 jax.ShapeDtypeStruct((rows // seq_len, KV_W, seq_len), BF16),
                 row_out(POOL_W, F32), row_out(FOURIER_W, F32), row_out(D_MODEL, F32)]
    for param in cast_next:
        _, p_rows, p_cols = param.shape
        slab = (None, p_rows // steps, p_cols)
        in_specs.append(pl.BlockSpec(slab, lambda i: (l + 1, i, 0)))
        out_specs.append(pl.BlockSpec(slab, lambda i: (0, i, 0)))
        out_shape.append(jax.ShapeDtypeStruct((1, p_rows, p_cols), BF16))
        args.append(param)
    outs = pl.pallas_call(
        functools.partial(_inproj_kernel, rope=rope, n_convert=len(cast_next),
                          tiles_per_batch=tiles_per_seq if per_batch_mod else None),
        grid=(steps,),
        in_specs=in_specs,
        out_specs=out_specs,
        out_shape=out_shape,
        scratch_shapes=[pltpu.VMEM((tm, D_MODEL), BF16)],
        compiler_params=_params(1),
        name="inproj_rope" if rope else "inproj_ctx",
    )(*args)
    return outs[:6], outs[6:]


def _ctx_kv_kernel(x_ref, mod_ref, ng_ref, w_ref, kg_ref, k_ref, vt_ref):
    h = _modulated_norm(x_ref, mod_ref, ng_ref, CTX_MOD_ROW)
    p = _dot(h, w_ref[...])
    k_gain = kg_ref[...]
    for hh in range(N_KV_HEADS):
        lo = hh * HEAD_DIM
        k_ref[:, lo:lo + HEAD_DIM] = _head_norm(p[:, lo:lo + HEAD_DIM], k_gain).astype(BF16)
    n_seqs, _, seq_len = vt_ref.shape
    for seq in range(n_seqs):
        vt_ref[seq] = p[seq * seq_len:(seq + 1) * seq_len, KV_W:2 * KV_W].T.astype(BF16)


def _ctx_kv(x2d, mod, norm_g, w_in, k_g, l, *, seqs_per_tile, seq_len):
    rows = x2d.shape[0]
    tm = seqs_per_tile * seq_len
    row_spec = lambda w: pl.BlockSpec((tm, w), lambda i: (i, 0))
    kv_cols = (0, OFF_K // (2 * KV_W))
    return pl.pallas_call(
        _ctx_kv_kernel,
        grid=(rows // tm,),
        in_specs=[row_spec(D_MODEL), _layer((MOD_ROWS, 3 * D_MODEL), 0), _layer((1, D_MODEL), l),
                  _layer((D_MODEL, 2 * KV_W), 0, kv_cols), _layer((1, HEAD_DIM), l)],
        out_specs=[row_spec(KV_W),
                   pl.BlockSpec((seqs_per_tile, KV_W, seq_len), lambda i: (i, 0, 0))],
        out_shape=[jax.ShapeDtypeStruct((rows, KV_W), BF16),
                   jax.ShapeDtypeStruct((rows // seq_len, KV_W, seq_len), BF16)],
        compiler_params=_params(1),
        name="ctx_kv",
    )(x2d, mod, norm_g, w_in, k_g)


def _attn_kernel(*refs, lengths, tq, with_adaln):
    n_sources = len(lengths)
    q_ref, g_ref = refs[0], refs[1]
    k_refs = refs[2:2 + 2 * n_sources:2]
    vt_refs = refs[3:3 + 2 * n_sources:2]
    n_in = 2 + 2 * n_sources
    o_ref, s_ref, p_ref, vta_ref = refs[-4:]
    if with_adaln:
        _adaln_kernel(*refs[n_in:n_in + 4])
    offsets = [sum(lengths[:i]) for i in range(n_sources)]
    total = sum(lengths)
    n_tiles = q_ref.shape[0] // tq
    key_chunks = [(k_ref, off, r0, min(KEY_CHUNK, length))
                  for k_ref, off, length in zip(k_refs, offsets, lengths)
                  for r0 in range(0, length, KEY_CHUNK)]
    head_cols = [slice(hh * HEAD_DIM, (hh + 1) * HEAD_DIM) for hh in range(GQA_GROUP)]

    def q_rows(tile):
        return pl.ds(pl.multiple_of(tile * tq, tq), tq)

    def fold_rows(x, op):
        return op(x.reshape(x.shape[0] // SUBLANES, SUBLANES, tq), axis=0)

    def score_chunk(tile, hh, slot, chunk, col_max):
        k_ref, off, r0, size = key_chunks[chunk]
        s = _dot_nt(k_ref[r0:r0 + size, :], q_ref[q_rows(tile), head_cols[hh]])
        s_ref[slot, off + r0:off + r0 + size, :] = s
        cm = fold_rows(s, jnp.max)
        return cm if col_max is None else jnp.maximum(col_max, cm)

    def prob_chunk(slot, chunk, col_max):
        _, off, start, size = key_chunks[chunk]
        for r0 in range(off + start, off + start + size, EXP_CHUNK):
            p = jnp.exp2(s_ref[slot, r0:r0 + EXP_CHUNK, :] - col_max)
            p_ref[slot, r0:r0 + EXP_CHUNK, :] = p.astype(BF16)

    def finish(tile, hh, slot):
        acc = _dot(vta_ref[...], p_ref[slot])
        out = (acc[:HEAD_DIM] / acc[HEAD_DIM:HEAD_DIM + 1]).T
        gate = _silu(g_ref[q_rows(tile), head_cols[hh]])
        o_ref[q_rows(tile), head_cols[hh]] = (out * gate).astype(BF16)

    for vt_ref, off, length in zip(vt_refs, offsets, lengths):
        vta_ref[0:HEAD_DIM, off:off + length] = vt_ref[...]
    vta_ref[HEAD_DIM:HEAD_DIM + ONES_ROWS, :] = jnp.ones((ONES_ROWS, total), BF16)

    n_chunks = len(key_chunks)
    first_max = None
    for chunk in range(n_chunks):
        first_max = score_chunk(0, 0, 0, chunk, first_max)
    first_max = jnp.max(first_max, axis=0, keepdims=True)

    def tile_body(tile, cur_max):
        for hh in range(GQA_GROUP):
            slot, next_slot = hh % 2, (hh + 1) % 2
            next_hh = (hh + 1) % GQA_GROUP
            next_tile = tile if next_hh else jnp.minimum(tile + 1, n_tiles - 1)
            next_max = None
            for chunk in range(n_chunks):
                next_max = score_chunk(next_tile, next_hh, next_slot, chunk, next_max)
                prob_chunk(slot, chunk, cur_max)
            finish(tile, hh, slot)
            cur_max = jnp.max(next_max, axis=0, keepdims=True)
        return cur_max

    lax.fori_loop(0, n_tiles, tile_body, first_max)


def _attention(q, g, sources, *, batch, q_len, tq, adaln_next=None):
    q_spec = pl.BlockSpec((q_len, GQA_GROUP * HEAD_DIM), lambda b, h: (b, h))
    in_specs = [q_spec, q_spec]
    args = [q, g]
    for k, vt, length in sources:
        in_specs += [pl.BlockSpec((length, HEAD_DIM), lambda b, h: (b, h)),
                     pl.BlockSpec((None, HEAD_DIM, length), lambda b, h: (b, h, 0))]
        args += [k, vt]
    out_specs = [q_spec]
    out_shape = [jax.ShapeDtypeStruct((batch * q_len, ATTN_W), BF16)]
    if adaln_next is not None:
        cond, ada_w, ada_b, layer = adaln_next
        slab = 3 * D_MODEL // (batch * N_KV_HEADS)
        side_in, side_out, side_shape = _adaln_specs(layer, slab, lambda b, h: b * N_KV_HEADS + h)
        in_specs += side_in
        args += [cond, ada_w, ada_b]
        out_specs = [side_out] + out_specs
        out_shape = [side_shape] + out_shape
    lengths = tuple(length for _, _, length in sources)
    total = sum(lengths)
    outs = pl.pallas_call(
        functools.partial(_attn_kernel, lengths=lengths, tq=tq, with_adaln=adaln_next is not None),
        grid=(batch, N_KV_HEADS),
        in_specs=in_specs,
        out_specs=out_specs,
        out_shape=out_shape,
        scratch_shapes=[pltpu.VMEM((2, total, tq), F32), pltpu.VMEM((2, total, tq), BF16),
                        pltpu.VMEM((HEAD_DIM + ONES_ROWS, total), BF16)],
        compiler_params=_params(2),
        name="attention",
    )(*args)
    return outs[-1], (outs[0] if adaln_next is not None else None)


SHORT_ATTN_SEQS = 2


def _short_attn_kernel(q_ref, g_ref, k_ref, vt_ref, o_ref):
    n_seqs, _, length = vt_ref.shape
    for seq in range(n_seqs):
        rows = slice(seq * length, (seq + 1) * length)
        for kvh in range(N_KV_HEADS):
            kv_cols = slice(kvh * HEAD_DIM, (kvh + 1) * HEAD_DIM)
            heads = [slice((kvh * GQA_GROUP + hh) * HEAD_DIM, (kvh * GQA_GROUP + hh + 1) * HEAD_DIM)
                     for hh in range(GQA_GROUP)]
            q_rows = jnp.concatenate([q_ref[rows, cols] for cols in heads], axis=0)
            s = _dot_nt(k_ref[rows, kv_cols], q_rows)
            p = jnp.exp2(s - jnp.max(s, axis=0, keepdims=True)).astype(BF16)
            vt_aug = jnp.concatenate([vt_ref[seq, kv_cols, :],
                                      jnp.ones((ONES_ROWS, length), BF16)], axis=0)
            acc = _dot(vt_aug, p)
            for hh, cols in enumerate(heads):
                span = slice(hh * length, (hh + 1) * length)
                out = (acc[:HEAD_DIM, span] / acc[HEAD_DIM:HEAD_DIM + 1, span]).T
                o_ref[rows, cols] = (out * _silu(g_ref[rows, cols])).astype(BF16)


def _short_attention(q, g, k, vt, *, batch, length):
    seqs = SHORT_ATTN_SEQS
    row_spec = lambda w: pl.BlockSpec((seqs * length, w), lambda i: (i, 0))
    return pl.pallas_call(
        _short_attn_kernel,
        grid=(batch // seqs,),
        in_specs=[row_spec(ATTN_W), row_spec(ATTN_W), row_spec(KV_W),
                  pl.BlockSpec((seqs, KV_W, length), lambda i: (i, 0, 0))],
        out_specs=row_spec(ATTN_W),
        out_shape=jax.ShapeDtypeStruct((batch * length, ATTN_W), BF16),
        compiler_params=_params(1),
        name="short_attention",
    )(q, g, k, vt)


POOL_CHUNK = 256
OUT_CHUNK = 256


def _pool_steps(up_ref, prev_ref, next_ref, g_ref, pw_ref, ps_ref, dst_ref,
                pad_ref, s2_ref, s4_ref, s8_ref, *, tile_in_seq, tiles_per_seq, seq_len):
    tm = up_ref.shape[0]
    halo = POOL_HALO
    chunk = min(tm, POOL_CHUNK)

    def forward_sum(dst, src, shift, length, lane0):
        for r0 in range(0, length, chunk):
            rows = min(chunk, length - r0)
            dst[r0:r0 + rows, :] = (src[r0:r0 + rows, lane0:]
                                    + src[r0 + shift:r0 + shift + rows, lane0:])

    def window_sums():
        pad_ref[0:halo, :] = jnp.where(tile_in_seq > 0, prev_ref[...], 0.0)
        pad_ref[halo:halo + tm, :] = up_ref[...]
        pad_ref[halo + tm:2 * halo + tm, :] = jnp.where(tile_in_seq < tiles_per_seq - 1,
                                                        next_ref[...], 0.0)
        pad_ref[2 * halo + tm:4 * halo + tm, :] = jnp.zeros((2 * halo, POOL_W), F32)
        forward_sum(s2_ref, pad_ref, 1, tm + 3 * halo, 0)
        forward_sum(s4_ref, s2_ref, 2, tm + 2 * halo, GROUP_C)
        forward_sum(s8_ref, s4_ref, 4, tm + halo, GROUP_C)

    def window_sum(gi, r0):
        if gi == 0:
            return s2_ref[halo - 1 + r0:halo - 1 + r0 + chunk, 0:GROUP_C]
        if gi == 1:
            return s4_ref[halo - 2 + r0:halo - 2 + r0 + chunk, 0:GROUP_C]
        if gi == 2:
            return s8_ref[halo - 4 + r0:halo - 4 + r0 + chunk, 0:GROUP_C]
        return (s8_ref[r0:r0 + chunk, GROUP_C:] + s8_ref[halo + r0:halo + r0 + chunk, GROUP_C:])

    def edge_inverse_count(win, local_row):
        before, after = win // 2, win - win // 2 - 1
        t = tile_in_seq * tm + local_row + lax.broadcasted_iota(jnp.int32, (halo, GROUP_C), 0)
        cnt = jnp.minimum(t + after, seq_len - 1) - jnp.maximum(t - before, 0) + 1
        return 1.0 / cnt.astype(F32)

    def pooled_block(gi, win, r0):
        cols = slice(gi * GROUP_C, (gi + 1) * GROUP_C)
        pieces, inner = [], chunk
        if r0 == 0:
            pieces.append(edge_inverse_count(win, 0))
            inner -= halo
        tail = [edge_inverse_count(win, tm - halo)] if r0 + chunk == tm else []
        inner -= halo * len(tail)
        pieces.append(jnp.full((inner, GROUP_C), 1.0 / win, F32))
        inv_cnt = jnp.concatenate(pieces + tail, axis=0)
        pooled = window_sum(gi, r0) * inv_cnt - pad_ref[halo + r0:halo + r0 + chunk, cols]
        y = _dot(pooled.astype(BF16), pw_ref[gi].astype(BF16)) * ps_ref[:, cols]
        dst_ref[r0:r0 + chunk, cols] = (y * _silu(g_ref[r0:r0 + chunk, cols])).astype(BF16)

    return [window_sums] + [functools.partial(pooled_block, gi, win, r0)
                            for gi, win in enumerate(POOL_WINDOWS)
                            for r0 in range(0, tm, chunk)]


def _pool_scratch(tm):
    return [pltpu.VMEM((tm + 4 * POOL_HALO, POOL_W), F32),
            pltpu.VMEM((tm + 3 * POOL_HALO, POOL_W), F32),
            pltpu.VMEM((tm + 2 * POOL_HALO, POOL_W - GROUP_C), F32),
            pltpu.VMEM((tm + POOL_HALO, POOL_W - 2 * GROUP_C), F32),
            pltpu.VMEM((tm, POOL_W), BF16)]


def _fourier_kernel(u_ref, g_ref, fw_ref, cc_ref, sc_ref, cn_ref, sn_ref, o_ref, ua_ref, ub_ref,
                    *, n, chunk):
    for gi in range(MIX_GROUPS):
        cols = slice(gi * GROUP_C, (gi + 1) * GROUP_C)
        w = fw_ref[gi].astype(BF16)
        a = _dot(cc_ref[...], w).astype(BF16)
        b = _dot(sc_ref[...], w).astype(BF16)
        ug = u_ref[:, cols].astype(BF16)
        ua_ref[:, cols] = _dot(ug, a).astype(BF16)
        ub_ref[:, cols] = _dot(ug, b).astype(BF16)
    for r0 in range(0, n, chunk):
        rows = slice(r0, r0 + chunk)
        y = _dot(cn_ref[rows, :], ua_ref[...]) - _dot(sn_ref[rows, :], ub_ref[...])
        o_ref[rows, :] = (y * _silu(g_ref[rows, :])).astype(BF16)


def _fourier(uf, g, fourier_w, tabs, l, *, batch, n):
    cc, sc, cn, sn = tabs
    seq_spec = lambda col: pl.BlockSpec((n, FOURIER_W), lambda b: (b, col))
    return pl.pallas_call(
        functools.partial(_fourier_kernel, n=n, chunk=min(n, 512)),
        grid=(batch,),
        in_specs=[seq_spec(0), seq_spec((ATTN_W + POOL_W) // FOURIER_W),
                  _layer((MIX_GROUPS, GROUP_C, GROUP_C), l),
                  _resident((GROUP_C, GROUP_C)), _resident((GROUP_C, GROUP_C)),
                  _resident((n, n)), _resident((n, n))],
        out_specs=seq_spec(0),
        out_shape=jax.ShapeDtypeStruct((batch * n, FOURIER_W), BF16),
        scratch_shapes=[pltpu.VMEM((n, FOURIER_W), BF16), pltpu.VMEM((n, FOURIER_W), BF16)],
        compiler_params=_params(1),
        name="fourier",
    )(uf, g, fourier_w, cc, sc, cn, sn)


RADIX = 8
COMBINE_ROWS = 16
SQRT_HALF = math.sqrt(0.5)


def _radix8_real(yr, yn):
    sr = [yr[n] + yr[n + 4] for n in range(4)]
    dr = [yr[n] - yr[n + 4] for n in range(4)]
    sn = {n: yn[n] + yn[n + 4] for n in (1, 3)}
    dn = {n: yn[n] - yn[n + 4] for n in (1, 2, 3)}
    t1r, t1i = (dr[1] - dn[1]) * SQRT_HALF, (dr[1] + dn[1]) * -SQRT_HALF
    t3r, t3i = (dr[3] + dn[3]) * -SQRT_HALF, (dn[3] - dr[3]) * SQRT_HALF
    out = [None] * RADIX
    ea, eb = sr[0] + sr[2], sr[1] + sr[3]
    ec, ed = sr[0] - sr[2], sn[3] - sn[1]
    out[0], out[4] = ea + eb, ea - eb
    out[2], out[6] = ec + ed, ec - ed
    oa, ob = dr[0] - dn[2], t1r + t3r
    oc, od = dr[0] + dn[2], t1i - t3i
    out[1], out[5] = oa + ob, oa - ob
    out[3], out[7] = oc + od, oc - od
    return out


def _fourier_radix_kernel(*refs, n1):
    u_refs = refs[:MIX_GROUPS]
    g_ref, fw_ref, cc_ref, sc_ref, cs_ref, o_ref, ub_ref = refs[MIX_GROUPS:MIX_GROUPS + 7]
    scratch = refs[MIX_GROUPS + 7:]
    pq_refs, yr_refs, yn_refs = (scratch[i * MIX_GROUPS:(i + 1) * MIX_GROUPS] for i in range(3))
    group_lanes = [slice(gi * GROUP_C, (gi + 1) * GROUP_C) for gi in range(MIX_GROUPS)]

    for n2 in range(RADIX):
        for gi, lanes in enumerate(group_lanes):
            ub_ref[n2, :, lanes] = u_refs[gi][pl.ds(n2, n1, stride=RADIX), :].astype(BF16)
        pq = _dot(cs_ref[n2], ub_ref[n2])
        rows = slice(n2 * n1, (n2 + 1) * n1)
        for gi, lanes in enumerate(group_lanes):
            pq_refs[gi][rows, :GROUP_C] = pq[:n1, lanes].astype(BF16)
            pq_refs[gi][rows, GROUP_C:] = pq[n1:, lanes].astype(BF16)

    for gi, lanes in enumerate(group_lanes):
        w = fw_ref[gi].astype(BF16)
        a, b = _dot(cc_ref[...], w), _dot(sc_ref[...], w)
        mix = jnp.concatenate([jnp.concatenate([a, b], axis=1),
                               jnp.concatenate([-b, a], axis=1)], axis=0).astype(BF16)
        y = _dot(pq_refs[gi][...], mix)
        for n2 in range(RADIX):
            yr_refs[gi][n2] = y[n2 * n1:(n2 + 1) * n1, :GROUP_C]
            yn_refs[gi][n2] = y[n2 * n1:(n2 + 1) * n1, GROUP_C:]

        for r0 in range(0, n1, COMBINE_ROWS):
            rows = slice(r0, r0 + COMBINE_ROWS)
            yr = [yr_refs[gi][n2, rows, :] for n2 in range(RADIX)]
            yn = [yn_refs[gi][n2, rows, :] for n2 in range(RADIX)]
            for k2, val in enumerate(_radix8_real(yr, yn)):
                out_rows = slice(k2 * n1 + r0, k2 * n1 + r0 + COMBINE_ROWS)
                o_ref[out_rows, lanes] = (val * _silu(g_ref[out_rows, lanes])).astype(BF16)


def _fourier_radix(uf, g, fourier_w, tabs, l, *, batch, n):
    cc, sc, cs = tabs
    n1 = n // RADIX
    seq_spec = lambda col: pl.BlockSpec((n, FOURIER_W), lambda b: (b, col))
    group_specs = [pl.BlockSpec((n, GROUP_C), lambda b, gi=gi: (b, gi)) for gi in range(MIX_GROUPS)]
    return pl.pallas_call(
        functools.partial(_fourier_radix_kernel, n1=n1),
        grid=(batch,),
        in_specs=group_specs + [
            seq_spec((ATTN_W + POOL_W) // FOURIER_W),
            _layer((MIX_GROUPS, GROUP_C, GROUP_C), l),
            _resident((GROUP_C, GROUP_C)), _resident((GROUP_C, GROUP_C)),
            _resident((RADIX, 2 * n1, n1))],
        out_specs=seq_spec(0),
        out_shape=jax.ShapeDtypeStruct((batch * n, FOURIER_W), BF16),
        scratch_shapes=([pltpu.VMEM((RADIX, n1, FOURIER_W), BF16)]
                        + [pltpu.VMEM((n, 2 * GROUP_C), BF16)] * MIX_GROUPS
                        + [pltpu.VMEM((RADIX, n1, GROUP_C), F32)] * (2 * MIX_GROUPS)),
        compiler_params=_params(1),
        name="fourier_radix",
    )(*([uf] * MIX_GROUPS), g, fourier_w, cc, sc, cs)


def _outproj_kernel(*refs, tiles_per_seq, seq_len, per_batch_mod, final):
    (att_ref, up_ref, prev_ref, next_ref, gp_ref, four_ref, w_ref, pw_ref, ps_ref,
     x_ref, mod_ref) = refs[:11]
    fg_ref = refs[11] if final else None
    o_ref, pad_ref, s2_ref, s4_ref, s8_ref, pooled_ref = refs[-6:]
    step = pl.program_id(0)
    pool_steps = _pool_steps(up_ref, prev_ref, next_ref, gp_ref, pw_ref, ps_ref, pooled_ref,
                             pad_ref, s2_ref, s4_ref, s8_ref, tile_in_seq=step % tiles_per_seq,
                             tiles_per_seq=tiles_per_seq, seq_len=seq_len)
    col_chunks = [slice(c0, c0 + OUT_CHUNK) for c0 in range(0, D_MODEL, OUT_CHUNK)]

    pool_steps.pop(0)()
    for cols in col_chunks:
        o_ref[:, cols] = (_dot(att_ref[...], w_ref[0:ATTN_W, cols])
                          + _dot(four_ref[...], w_ref[ATTN_W + POOL_W:D_MODEL, cols]))
        if pool_steps:
            pool_steps.pop(0)()
    for pool_step in pool_steps:
        pool_step()

    row = step // tiles_per_seq if per_batch_mod else CTX_MOD_ROW
    sum_sq = None
    for cols in col_chunks:
        gate = mod_ref[pl.ds(row, 1), 2 * D_MODEL + cols.start:2 * D_MODEL + cols.stop]
        out = o_ref[:, cols] + _dot(pooled_ref[...], w_ref[ATTN_W:ATTN_W + POOL_W, cols])
        y = x_ref[:, cols] + gate * out
        o_ref[:, cols] = y
        if final:
            part = jnp.sum(y * y, axis=-1, keepdims=True)
            sum_sq = part if sum_sq is None else sum_sq + part
    if final:
        inv_rms = lax.rsqrt(sum_sq * (1.0 / D_MODEL) + EPS)
        o_ref[...] = (o_ref[...] * inv_rms) * fg_ref[...]


def _outproj(att, up, g, four, w_out, pool_w, pool_scale, x2d, mod, final_g, l,
             *, tm, seq_len, per_batch_mod):
    rows = x2d.shape[0]
    final = final_g is not None
    tiles_per_seq = seq_len // tm
    halo_blocks = tm // POOL_HALO
    row_spec = lambda w: pl.BlockSpec((tm, w), lambda i: (i, 0))
    halo_spec = lambda index: pl.BlockSpec((POOL_HALO, POOL_W), lambda i: (index(i), 0))
    in_specs = [row_spec(ATTN_W), row_spec(POOL_W),
                halo_spec(lambda i: jnp.maximum(i * halo_blocks - 1, 0)),
                halo_spec(lambda i: jnp.minimum((i + 1) * halo_blocks, rows // POOL_HALO - 1)),
                pl.BlockSpec((tm, POOL_W), lambda i: (i, ATTN_W // POOL_W)), row_spec(FOURIER_W),
                _layer((D_MODEL, D_MODEL), 0), _layer((MIX_GROUPS, GROUP_C, GROUP_C), l),
                _layer((1, POOL_W), l), row_spec(D_MODEL), _layer((MOD_ROWS, 3 * D_MODEL), 0)]
    args = [att, up, up, up, g, four, w_out, pool_w, pool_scale, x2d, mod]
    if final:
        in_specs.append(_resident((1, D_MODEL)))
        args.append(final_g)
    return pl.pallas_call(
        functools.partial(_outproj_kernel, tiles_per_seq=tiles_per_seq, seq_len=seq_len,
                          per_batch_mod=per_batch_mod, final=final),
        grid=(rows // tm,),
        in_specs=in_specs,
        out_specs=row_spec(D_MODEL),
        out_shape=jax.ShapeDtypeStruct((rows, D_MODEL), F32),
        scratch_shapes=_pool_scratch(tm),
        compiler_params=_params(1),
        name="outproj_final" if final else "outproj",
    )(*args)


def _rope_tables(n):
    pos = jnp.arange(n, dtype=jnp.int32)
    row = (pos // GRID_W).astype(F32)
    col = (pos % GRID_W).astype(F32)
    inv = ROPE_THETA ** (-jnp.arange(0, AXIS_ROT, 2, dtype=F32) / AXIS_ROT)
    ang_r = row[:, None] * inv
    ang_c = col[:, None] * inv
    cos = jnp.concatenate([jnp.cos(ang_r), jnp.cos(ang_r), jnp.cos(ang_c), jnp.cos(ang_c)], axis=1)
    sin = jnp.concatenate([-jnp.sin(ang_r), jnp.sin(ang_r), -jnp.sin(ang_c), jnp.sin(ang_c)], axis=1)
    return cos, sin


def _dft_cos_sin(n):
    k = jnp.arange(n, dtype=jnp.int32)
    ang = ((k[:, None] * k[None, :]) % n).astype(F32) * (2.0 * math.pi / n)
    return jnp.cos(ang), jnp.sin(ang)


def _fourier_tables(n):
    cc, sc = _dft_cos_sin(GROUP_C)
    cn, sn = _dft_cos_sin(n)
    norm = 1.0 / math.sqrt(n * GROUP_C)
    return (cc * norm).astype(BF16), (sc * norm).astype(BF16), cn.astype(BF16), sn.astype(BF16)


def _fourier_radix_tables(n):
    n1 = n // RADIX
    cc, sc = _dft_cos_sin(GROUP_C)
    norm = 1.0 / math.sqrt(n * GROUP_C)
    k1 = jnp.arange(n1, dtype=jnp.int32)[None, :, None]
    pos = (RADIX * jnp.arange(n1, dtype=jnp.int32)[None, None, :]
           + jnp.arange(RADIX, dtype=jnp.int32)[:, None, None])
    ang = ((k1 * pos) % n).astype(F32) * (2.0 * math.pi / n)
    cs = jnp.concatenate([jnp.cos(ang), jnp.sin(ang)], axis=1).astype(BF16)
    return (cc * norm).astype(BF16), (sc * norm).astype(BF16), cs


def kernel(x, c, ctx, c_ctx, ada_w, ada_b, norm_g, w_in, q_norm_g, k_norm_g, pool_w, pool_scale,
           fourier_w, w_out, final_norm_g):
    B, N, _ = x.shape
    C = ctx.shape[1]
    tm = 512
    tm_ctx = 256

    cond = jnp.concatenate([c, c_ctx[None, :], jnp.zeros((MOD_ROWS - B - 1, D_MODEL), F32)], axis=0)
    ada_b3 = ada_b.reshape(DEPTH, 1, 3 * D_MODEL)
    mod = _adaln(cond, ada_w, ada_b3, 0)

    rope_tabs = _rope_tables(N)
    four_tabs = _fourier_radix_tables(N)
    four_tabs_ctx = _fourier_tables(C)
    w_in_b = w_in[0:1].astype(BF16)
    w_out_b = w_out[0:1].astype(BF16)
    ng = norm_g[:, None, :]
    qg = q_norm_g[:, None, :]
    kg = k_norm_g[:, None, :]
    ps = pool_scale[:, None, :]

    xl = x.reshape(B * N, D_MODEL)
    xc = ctx.reshape(B * C, D_MODEL)
    for l in range(DEPTH):
        last = l == DEPTH - 1
        if last:
            kc, vct = _ctx_kv(xc, mod, ng, w_in_b, kg, l, seqs_per_tile=2, seq_len=C)
        else:
            (qc, kc, vct, upc, ufc, gc), _ = _inproj(xc, mod, ng, w_in_b, qg, kg, None, l,
                                                     tm=tm_ctx, seq_len=C, per_batch_mod=False)
        (q, k, vt, up, uf, g), next_weights = _inproj(
            xl, mod, ng, w_in_b, qg, kg, rope_tabs, l, tm=tm, seq_len=N, per_batch_mod=True,
            cast_next=() if last else (w_in, w_out))

        att, next_mod = _attention(q, g, [(kc, vct, C), (k, vt, N)], batch=B, q_len=N, tq=256,
                                   adaln_next=None if last else (cond, ada_w, ada_b3, l + 1))
        four = _fourier_radix(uf, g, fourier_w, four_tabs, l, batch=B, n=N)
        xl_new = _outproj(att, up, g, four, w_out_b, pool_w, ps, xl, mod,
                          final_norm_g[None, :] if last else None, l,
                          tm=tm, seq_len=N, per_batch_mod=True)

        if not last:
            attc = _short_attention(qc, gc, kc, vct, batch=B, length=C)
            fourc = _fourier(ufc, gc, fourier_w, four_tabs_ctx, l, batch=B, n=C)
            xc = _outproj(attc, upc, gc, fourc, w_out_b, pool_w, ps, xc, mod, None, l,
                          tm=tm_ctx, seq_len=C, per_batch_mod=False)
            w_in_b, w_out_b = next_weights
            mod = next_mod
        xl = xl_new
    return xl.reshape(B, N, D_MODEL)
```

```python
import functools
import math

import jax
import jax.numpy as jnp
from jax import lax
from jax.experimental import pallas as pl
from jax.experimental.pallas import tpu as pltpu

D_MODEL = 2048
DEPTH = 2
GRID_W = 64
HEAD_DIM = 128
ATTN_W = 1024
N_HEADS = 8
N_KV_HEADS = 2
GQA_GROUP = 4
KV_W = 256
POOL_W = 512
POOL_WINDOWS = (2, 4, 8, 16)
FOURIER_W = 512
MIX_GROUPS = 4
GROUP_C = 128
OFF_K = ATTN_W
OFF_V = OFF_K + KV_W
OFF_POOL = OFF_V + KV_W
OFF_FOURIER = OFF_POOL + POOL_W
OFF_GATE = OFF_FOURIER + FOURIER_W
IN_W = OFF_GATE + D_MODEL
ROPE_THETA = 10000.0
AXIS_ROT = HEAD_DIM // 2
EPS = 1e-6

MOD_ROWS = 16
CTX_MOD_ROW = 8
SUBLANES = 8
ROW_PARTS = 2
NORM_PIECES = 8
POOL_HALO = 8
KEY_CHUNK = 256
EXP_CHUNK = 256
ONES_ROWS = 16
V7X_VMEM_BYTES = 64 * 1024 * 1024
VMEM_LIMIT = V7X_VMEM_BYTES - 8 * 1024 * 1024

BF16 = jnp.bfloat16
F32 = jnp.float32


def _silu(x):
    return x * jax.nn.sigmoid(x)


def _dot(a, b):
    return jnp.dot(a, b, preferred_element_type=F32)


def _dot_nt(a, b):
    return lax.dot_general(a, b, (((1,), (1,)), ((), ())), preferred_element_type=F32)


def _resident(shape):
    zeros = (0,) * len(shape)
    return pl.BlockSpec(shape, lambda *_: zeros, pipeline_mode=pl.Buffered(1))


def _layer(shape, l, tail=None):
    index = (l,) + (tail or (0,) * len(shape))
    return pl.BlockSpec((None,) + tuple(shape), lambda *_: index, pipeline_mode=pl.Buffered(1))


def _params(n_grid):
    return pltpu.CompilerParams(dimension_semantics=("arbitrary",) * n_grid,
                                vmem_limit_bytes=VMEM_LIMIT)


def _adaln_kernel(c_ref, w_ref, b_ref, o_ref):
    a = _silu(c_ref[...]).astype(BF16)
    o_ref[...] = _dot(a, w_ref[...].astype(BF16)) + b_ref[...]


def _adaln_specs(l, tn, step):
    in_specs = [pl.BlockSpec((MOD_ROWS, D_MODEL), lambda *i: (0, 0)),
                pl.BlockSpec((None, D_MODEL, tn), lambda *i: (l, 0, step(*i))),
                pl.BlockSpec((None, 1, tn), lambda *i: (l, 0, step(*i)))]
    out_spec = pl.BlockSpec((None, MOD_ROWS, tn), lambda *i: (0, 0, step(*i)))
    return in_specs, out_spec, jax.ShapeDtypeStruct((1, MOD_ROWS, 3 * D_MODEL), F32)


def _adaln(cond, ada_w, ada_b, l):
    tn = 1024
    in_specs, out_spec, out_shape = _adaln_specs(l, tn, lambda j: j)
    return pl.pallas_call(
        _adaln_kernel,
        grid=(3 * D_MODEL // tn,),
        in_specs=in_specs,
        out_specs=out_spec,
        out_shape=out_shape,
        compiler_params=_params(1),
        name="adaln",
    )(cond, ada_w, ada_b)


def _modulated_norm(x_ref, mod_ref, ng_ref, row, rows=slice(None)):
    x = x_ref[rows, :]
    ms = jnp.mean(x * x, axis=-1, keepdims=True)
    shift = mod_ref[pl.ds(row, 1), 0:D_MODEL]
    scale = mod_ref[pl.ds(row, 1), D_MODEL:2 * D_MODEL]
    y = (x * lax.rsqrt(ms + EPS)) * ng_ref[...]
    return (y * (1.0 + scale) + shift).astype(BF16)


def _head_norm(p, gain):
    ms = jnp.mean(p * p, axis=-1, keepdims=True)
    return (p * lax.rsqrt(ms + EPS)) * gain


def _rope(y, cos, sin, swap_lo):
    partner = jnp.where(swap_lo, pltpu.roll(y, 96, 1), pltpu.roll(y, 32, 1))
    return y * cos + partner * sin


def _inproj_kernel(*refs, rope, tiles_per_batch, n_convert):
    n_in = 8 if rope else 6
    x_ref, mod_ref, ng_ref, w_ref, qg_ref, kg_ref = refs[:6]
    cos_ref, sin_ref = refs[6:8] if rope else (None, None)
    f32_weight_refs = refs[n_in:n_in + n_convert]
    q_ref, k_ref, vt_ref, up_ref, uf_ref, g_ref = refs[n_in + n_convert:n_in + n_convert + 6]
    bf16_weight_refs = refs[n_in + n_convert + 6:n_in + 2 * n_convert + 6]
    h_ref = refs[-1]
    for src_ref, dst_ref in zip(f32_weight_refs, bf16_weight_refs):
        dst_ref[...] = src_ref[...].astype(BF16)
    tm = x_ref.shape[0]
    row = CTX_MOD_ROW if tiles_per_batch is None else pl.program_id(0) // tiles_per_batch
    q_gain = qg_ref[...] * (HEAD_DIM ** -0.5 * math.log2(math.e))
    k_gain = kg_ref[...]
    chunk = GQA_GROUP * HEAD_DIM

    def norm_steps(rows):
        piece_rows = (rows.stop - rows.start) // NORM_PIECES
        pieces = [slice(r0, r0 + piece_rows) for r0 in range(rows.start, rows.stop, piece_rows)]

        def norm(piece):
            h_ref[piece, :] = _modulated_norm(x_ref, mod_ref, ng_ref, row, piece)

        return [functools.partial(norm, piece) for piece in pieces]

    def matmul_steps(rows):
        n_rows = rows.stop - rows.start
        if rope:
            cos, sin = cos_ref[rows, :], sin_ref[rows, :]
            lane = lax.broadcasted_iota(jnp.int32, (n_rows, HEAD_DIM), 1)
            swap_lo = (lane & (AXIS_ROT // 2)) == 0

        def finish_head(p, gain):
            y = _head_norm(p, gain)
            if rope:
                y = _rope(y, cos, sin, swap_lo)
            return y.astype(BF16)

        def project(c0, width):
            return _dot(h_ref[rows, :], w_ref[:, c0:c0 + width])

        def q_chunk(c0):
            p = project(c0, chunk)
            for hh in range(GQA_GROUP):
                lo = hh * HEAD_DIM
                q_ref[rows, c0 + lo:c0 + lo + HEAD_DIM] = finish_head(p[:, lo:lo + HEAD_DIM], q_gain)

        def kv_chunk():
            p = project(OFF_K, 2 * KV_W)
            for hh in range(N_KV_HEADS):
                lo = hh * HEAD_DIM
                k_ref[rows, lo:lo + HEAD_DIM] = finish_head(p[:, lo:lo + HEAD_DIM], k_gain)
            vt_ref[:, rows] = p[:, KV_W:2 * KV_W].T.astype(BF16)

        def up_chunk():
            up_ref[rows, :] = project(OFF_POOL, POOL_W)

        def uf_chunk():
            uf_ref[rows, :] = project(OFF_FOURIER, FOURIER_W)

        def gate_chunk(c0):
            g_ref[rows, c0:c0 + chunk] = project(OFF_GATE + c0, chunk)

        return ([functools.partial(q_chunk, c0) for c0 in range(0, ATTN_W, chunk)]
                + [kv_chunk, up_chunk, uf_chunk]
                + [functools.partial(gate_chunk, c0) for c0 in range(0, D_MODEL, chunk)])

    part_rows = tm // ROW_PARTS
    parts = [slice(r0, r0 + part_rows) for r0 in range(0, tm, part_rows)]
    for step in norm_steps(parts[0]):
        step()
    for index, rows in enumerate(parts):
        fill = norm_steps(parts[index + 1]) if index + 1 < len(parts) else []
        for step in matmul_steps(rows):
            step()
            if fill:
                fill.pop(0)()
        for step in fill:
            step()


def _vt_spec(tm, tiles_per_seq):
    return pl.BlockSpec((None, KV_W, tm), lambda i: (i // tiles_per_seq, 0, i % tiles_per_seq))


def _inproj(x2d, mod, norm_g, w_in, q_g, k_g, rope_tabs, l, *, tm, seq_len, per_batch_mod,
            casts=()):
    rows = x2d.shape[0]
    steps = rows // tm
    rope = rope_tabs is not None
    tiles_per_seq = seq_len // tm
    row_spec = lambda w: pl.BlockSpec((tm, w), lambda i: (i, 0))
    in_specs = [row_spec(D_MODEL), _layer((MOD_ROWS, 3 * D_MODEL), 0), _layer((1, D_MODEL), l),
                _layer((D_MODEL, IN_W), 0), _layer((1, HEAD_DIM), l), _layer((1, HEAD_DIM), l)]
    args = [x2d, mod, norm_g, w_in, q_g, k_g]
    if rope:
        tab_spec = pl.BlockSpec((tm, HEAD_DIM), lambda i: (i % tiles_per_seq, 0))
        in_specs += [tab_spec, tab_spec]
        args += list(rope_tabs)
    row_out = lambda w, dt: jax.ShapeDtypeStruct((rows, w), dt)
    out_specs = [row_spec(ATTN_W), row_spec(KV_W), _vt_spec(tm, tiles_per_seq),
                 row_spec(POOL_W), row_spec(FOURIER_W), row_spec(D_MODEL)]
    out_shape = [row_out(ATTN_W, BF16), row_out(KV_W, BF16),
                 jax.ShapeDtypeStruct((rows // seq_len, KV_W, seq_len), BF16),
                 row_out(POOL_W, F32), row_out(FOURIER_W, F32), row_out(D_MODEL, F32)]
    for param, layer in casts:
        _, p_rows, p_cols = param.shape
        slab = (None, p_rows // steps, p_cols)
        in_specs.append(pl.BlockSpec(slab, lambda i, layer=layer: (layer, i, 0)))
        out_specs.append(pl.BlockSpec(slab, lambda i: (0, i, 0)))
        out_shape.append(jax.ShapeDtypeStruct((1, p_rows, p_cols), BF16))
        args.append(param)
    outs = pl.pallas_call(
        functools.partial(_inproj_kernel, rope=rope, n_convert=len(casts),
                          tiles_per_batch=tiles_per_seq if per_batch_mod else None),
        grid=(steps,),
        in_specs=in_specs,
        out_specs=out_specs,
        out_shape=out_shape,
        scratch_shapes=[pltpu.VMEM((tm, D_MODEL), BF16)],
        compiler_params=_params(1),
        name="inproj_rope" if rope else "inproj_ctx",
    )(*args)
    return outs[:6], outs[6:]


def _ctx_kv_kernel(x_ref, mod_ref, ng_ref, w_ref, kg_ref, k_ref, vt_ref):
    h = _modulated_norm(x_ref, mod_ref, ng_ref, CTX_MOD_ROW)
    p = _dot(h, w_ref[...])
    k_gain = kg_ref[...]
    for hh in range(N_KV_HEADS):
        lo = hh * HEAD_DIM
        k_ref[:, lo:lo + HEAD_DIM] = _head_norm(p[:, lo:lo + HEAD_DIM], k_gain).astype(BF16)
    n_seqs, _, seq_len = vt_ref.shape
    for seq in range(n_seqs):
        vt_ref[seq] = p[seq * seq_len:(seq + 1) * seq_len, KV_W:2 * KV_W].T.astype(BF16)


def _ctx_kv(x2d, mod, norm_g, w_in, k_g, l, *, seqs_per_tile, seq_len):
    rows = x2d.shape[0]
    tm = seqs_per_tile * seq_len
    row_spec = lambda w: pl.BlockSpec((tm, w), lambda i: (i, 0))
    kv_cols = (0, OFF_K // (2 * KV_W))
    return pl.pallas_call(
        _ctx_kv_kernel,
        grid=(rows // tm,),
        in_specs=[row_spec(D_MODEL), _layer((MOD_ROWS, 3 * D_MODEL), 0), _layer((1, D_MODEL), l),
                  _layer((D_MODEL, 2 * KV_W), 0, kv_cols), _layer((1, HEAD_DIM), l)],
        out_specs=[row_spec(KV_W),
                   pl.BlockSpec((seqs_per_tile, KV_W, seq_len), lambda i: (i, 0, 0))],
        out_shape=[jax.ShapeDtypeStruct((rows, KV_W), BF16),
                   jax.ShapeDtypeStruct((rows // seq_len, KV_W, seq_len), BF16)],
        compiler_params=_params(1),
        name="ctx_kv",
    )(x2d, mod, norm_g, w_in, k_g)


def _attn_kernel(*refs, lengths, tq, with_adaln):
    n_sources = len(lengths)
    q_ref, g_ref = refs[0], refs[1]
    k_refs = refs[2:2 + 2 * n_sources:2]
    vt_refs = refs[3:3 + 2 * n_sources:2]
    n_in = 2 + 2 * n_sources
    o_ref, s_ref, p_ref, vta_ref = refs[-4:]
    if with_adaln:
        _adaln_kernel(*refs[n_in:n_in + 4])
    offsets = [sum(lengths[:i]) for i in range(n_sources)]
    total = sum(lengths)
    n_tiles = q_ref.shape[0] // tq
    key_chunks = [(k_ref, off, r0, min(KEY_CHUNK, length))
                  for k_ref, off, length in zip(k_refs, offsets, lengths)
                  for r0 in range(0, length, KEY_CHUNK)]
    head_cols = [slice(hh * HEAD_DIM, (hh + 1) * HEAD_DIM) for hh in range(GQA_GROUP)]

    def q_rows(tile):
        return pl.ds(pl.multiple_of(tile * tq, tq), tq)

    def fold_rows(x, op):
        return op(x.reshape(x.shape[0] // SUBLANES, SUBLANES, tq), axis=0)

    def score_chunk(tile, hh, slot, chunk, col_max):
        k_ref, off, r0, size = key_chunks[chunk]
        s = _dot_nt(k_ref[r0:r0 + size, :], q_ref[q_rows(tile), head_cols[hh]])
        s_ref[slot, off + r0:off + r0 + size, :] = s
        cm = fold_rows(s, jnp.max)
        return cm if col_max is None else jnp.maximum(col_max, cm)

    def prob_chunk(slot, chunk, col_max):
        _, off, start, size = key_chunks[chunk]
        for r0 in range(off + start, off + start + size, EXP_CHUNK):
            p = jnp.exp2(s_ref[slot, r0:r0 + EXP_CHUNK, :] - col_max)
            p_ref[slot, r0:r0 + EXP_CHUNK, :] = p.astype(BF16)

    def finish(tile, hh, slot):
        acc = _dot(vta_ref[...], p_ref[slot])
        out = (acc[:HEAD_DIM] / acc[HEAD_DIM:HEAD_DIM + 1]).T
        gate = _silu(g_ref[q_rows(tile), head_cols[hh]])
        o_ref[q_rows(tile), head_cols[hh]] = (out * gate).astype(BF16)

    for vt_ref, off, length in zip(vt_refs, offsets, lengths):
        vta_ref[0:HEAD_DIM, off:off + length] = vt_ref[...]
    vta_ref[HEAD_DIM:HEAD_DIM + ONES_ROWS, :] = jnp.ones((ONES_ROWS, total), BF16)

    n_chunks = len(key_chunks)
    first_max = None
    for chunk in range(n_chunks):
        first_max = score_chunk(0, 0, 0, chunk, first_max)
    first_max = jnp.max(first_max, axis=0, keepdims=True)

    def tile_body(tile, cur_max):
        for hh in range(GQA_GROUP):
            slot, next_slot = hh % 2, (hh + 1) % 2
            next_hh = (hh + 1) % GQA_GROUP
            next_tile = tile if next_hh else jnp.minimum(tile + 1, n_tiles - 1)
            next_max = None
            for chunk in range(n_chunks):
                next_max = score_chunk(next_tile, next_hh, next_slot, chunk, next_max)
                prob_chunk(slot, chunk, cur_max)
            finish(tile, hh, slot)
            cur_max = jnp.max(next_max, axis=0, keepdims=True)
        return cur_max

    lax.fori_loop(0, n_tiles, tile_body, first_max)


def _attention(q, g, sources, *, batch, q_len, tq, adaln_next=None):
    q_spec = pl.BlockSpec((q_len, GQA_GROUP * HEAD_DIM), lambda b, h: (b, h))
    in_specs = [q_spec, q_spec]
    args = [q, g]
    for k, vt, length in sources:
        in_specs += [pl.BlockSpec((length, HEAD_DIM), lambda b, h: (b, h)),
                     pl.BlockSpec((None, HEAD_DIM, length), lambda b, h: (b, h, 0))]
        args += [k, vt]
    out_specs = [q_spec]
    out_shape = [jax.ShapeDtypeStruct((batch * q_len, ATTN_W), BF16)]
    if adaln_next is not None:
        cond, ada_w, ada_b, layer = adaln_next
        slab = 3 * D_MODEL // (batch * N_KV_HEADS)
        side_in, side_out, side_shape = _adaln_specs(layer, slab, lambda b, h: b * N_KV_HEADS + h)
        in_specs += side_in
        args += [cond, ada_w, ada_b]
        out_specs = [side_out] + out_specs
        out_shape = [side_shape] + out_shape
    lengths = tuple(length for _, _, length in sources)
    total = sum(lengths)
    outs = pl.pallas_call(
        functools.partial(_attn_kernel, lengths=lengths, tq=tq, with_adaln=adaln_next is not None),
        grid=(batch, N_KV_HEADS),
        in_specs=in_specs,
        out_specs=out_specs,
        out_shape=out_shape,
        scratch_shapes=[pltpu.VMEM((2, total, tq), F32), pltpu.VMEM((2, total, tq), BF16),
                        pltpu.VMEM((HEAD_DIM + ONES_ROWS, total), BF16)],
        compiler_params=_params(2),
        name="attention",
    )(*args)
    return outs[-1], (outs[0] if adaln_next is not None else None)


SHORT_ATTN_SEQS = 2


def _short_attn_kernel(q_ref, g_ref, k_ref, vt_ref, o_ref):
    n_seqs, _, length = vt_ref.shape
    for seq in range(n_seqs):
        rows = slice(seq * length, (seq + 1) * length)
        for kvh in range(N_KV_HEADS):
            kv_cols = slice(kvh * HEAD_DIM, (kvh + 1) * HEAD_DIM)
            heads = [slice((kvh * GQA_GROUP + hh) * HEAD_DIM, (kvh * GQA_GROUP + hh + 1) * HEAD_DIM)
                     for hh in range(GQA_GROUP)]
            q_rows = jnp.concatenate([q_ref[rows, cols] for cols in heads], axis=0)
            s = _dot_nt(k_ref[rows, kv_cols], q_rows)
            p = jnp.exp2(s - jnp.max(s, axis=0, keepdims=True)).astype(BF16)
            vt_aug = jnp.concatenate([vt_ref[seq, kv_cols, :],
                                      jnp.ones((ONES_ROWS, length), BF16)], axis=0)
            acc = _dot(vt_aug, p)
            for hh, cols in enumerate(heads):
                span = slice(hh * length, (hh + 1) * length)
                out = (acc[:HEAD_DIM, span] / acc[HEAD_DIM:HEAD_DIM + 1, span]).T
                o_ref[rows, cols] = (out * _silu(g_ref[rows, cols])).astype(BF16)


def _short_attention(q, g, k, vt, *, batch, length):
    seqs = SHORT_ATTN_SEQS
    row_spec = lambda w: pl.BlockSpec((seqs * length, w), lambda i: (i, 0))
    return pl.pallas_call(
        _short_attn_kernel,
        grid=(batch // seqs,),
        in_specs=[row_spec(ATTN_W), row_spec(ATTN_W), row_spec(KV_W),
                  pl.BlockSpec((seqs, KV_W, length), lambda i: (i, 0, 0))],
        out_specs=row_spec(ATTN_W),
        out_shape=jax.ShapeDtypeStruct((batch * length, ATTN_W), BF16),
        compiler_params=_params(1),
        name="short_attention",
    )(q, g, k, vt)


POOL_CHUNK = 256
OUT_CHUNK = 256


def _pool_steps(up_ref, prev_ref, next_ref, g_ref, pw_ref, ps_ref, dst_ref,
                pad_ref, s2_ref, s4_ref, s8_ref, *, tile_in_seq, tiles_per_seq, seq_len):
    tm = up_ref.shape[0]
    halo = POOL_HALO
    chunk = min(tm, POOL_CHUNK)

    def forward_sum(dst, src, shift, length, lane0):
        for r0 in range(0, length, chunk):
            rows = min(chunk, length - r0)
            dst[r0:r0 + rows, :] = (src[r0:r0 + rows, lane0:]
                                    + src[r0 + shift:r0 + shift + rows, lane0:])

    def window_sums():
        pad_ref[0:halo, :] = jnp.where(tile_in_seq > 0, prev_ref[...], 0.0)
        pad_ref[halo:halo + tm, :] = up_ref[...]
        pad_ref[halo + tm:2 * halo + tm, :] = jnp.where(tile_in_seq < tiles_per_seq - 1,
                                                        next_ref[...], 0.0)
        pad_ref[2 * halo + tm:4 * halo + tm, :] = jnp.zeros((2 * halo, POOL_W), F32)
        forward_sum(s2_ref, pad_ref, 1, tm + 3 * halo, 0)
        forward_sum(s4_ref, s2_ref, 2, tm + 2 * halo, GROUP_C)
        forward_sum(s8_ref, s4_ref, 4, tm + halo, GROUP_C)

    def window_sum(gi, r0):
        if gi == 0:
            return s2_ref[halo - 1 + r0:halo - 1 + r0 + chunk, 0:GROUP_C]
        if gi == 1:
            return s4_ref[halo - 2 + r0:halo - 2 + r0 + chunk, 0:GROUP_C]
        if gi == 2:
            return s8_ref[halo - 4 + r0:halo - 4 + r0 + chunk, 0:GROUP_C]
        return (s8_ref[r0:r0 + chunk, GROUP_C:] + s8_ref[halo + r0:halo + r0 + chunk, GROUP_C:])

    def edge_inverse_count(win, local_row):
        before, after = win // 2, win - win // 2 - 1
        t = tile_in_seq * tm + local_row + lax.broadcasted_iota(jnp.int32, (halo, GROUP_C), 0)
        cnt = jnp.minimum(t + after, seq_len - 1) - jnp.maximum(t - before, 0) + 1
        return 1.0 / cnt.astype(F32)

    def pooled_block(gi, win, r0):
        cols = slice(gi * GROUP_C, (gi + 1) * GROUP_C)
        pieces, inner = [], chunk
        if r0 == 0:
            pieces.append(edge_inverse_count(win, 0))
            inner -= halo
        tail = [edge_inverse_count(win, tm - halo)] if r0 + chunk == tm else []
        inner -= halo * len(tail)
        pieces.append(jnp.full((inner, GROUP_C), 1.0 / win, F32))
        inv_cnt = jnp.concatenate(pieces + tail, axis=0)
        pooled = window_sum(gi, r0) * inv_cnt - pad_ref[halo + r0:halo + r0 + chunk, cols]
        y = _dot(pooled.astype(BF16), pw_ref[gi].astype(BF16)) * ps_ref[:, cols]
        dst_ref[r0:r0 + chunk, cols] = (y * _silu(g_ref[r0:r0 + chunk, cols])).astype(BF16)

    return [window_sums] + [functools.partial(pooled_block, gi, win, r0)
                            for gi, win in enumerate(POOL_WINDOWS)
                            for r0 in range(0, tm, chunk)]


def _pool_scratch(tm):
    return [pltpu.VMEM((tm + 4 * POOL_HALO, POOL_W), F32),
            pltpu.VMEM((tm + 3 * POOL_HALO, POOL_W), F32),
            pltpu.VMEM((tm + 2 * POOL_HALO, POOL_W - GROUP_C), F32),
            pltpu.VMEM((tm + POOL_HALO, POOL_W - 2 * GROUP_C), F32),
            pltpu.VMEM((tm, POOL_W), BF16)]


def _fourier_kernel(u_ref, g_ref, fw_ref, cc_ref, sc_ref, cn_ref, sn_ref, o_ref, ua_ref, ub_ref,
                    *, n, chunk):
    for gi in range(MIX_GROUPS):
        cols = slice(gi * GROUP_C, (gi + 1) * GROUP_C)
        w = fw_ref[gi].astype(BF16)
        a = _dot(cc_ref[...], w).astype(BF16)
        b = _dot(sc_ref[...], w).astype(BF16)
        ug = u_ref[:, cols].astype(BF16)
        ua_ref[:, cols] = _dot(ug, a).astype(BF16)
        ub_ref[:, cols] = _dot(ug, b).astype(BF16)
    for r0 in range(0, n, chunk):
        rows = slice(r0, r0 + chunk)
        y = _dot(cn_ref[rows, :], ua_ref[...]) - _dot(sn_ref[rows, :], ub_ref[...])
        o_ref[rows, :] = (y * _silu(g_ref[rows, :])).astype(BF16)


def _fourier(uf, g, fourier_w, tabs, l, *, batch, n):
    cc, sc, cn, sn = tabs
    seq_spec = lambda col: pl.BlockSpec((n, FOURIER_W), lambda b: (b, col))
    return pl.pallas_call(
        functools.partial(_fourier_kernel, n=n, chunk=min(n, 512)),
        grid=(batch,),
        in_specs=[seq_spec(0), seq_spec((ATTN_W + POOL_W) // FOURIER_W),
                  _layer((MIX_GROUPS, GROUP_C, GROUP_C), l),
                  _resident((GROUP_C, GROUP_C)), _resident((GROUP_C, GROUP_C)),
                  _resident((n, n)), _resident((n, n))],
        out_specs=seq_spec(0),
        out_shape=jax.ShapeDtypeStruct((batch * n, FOURIER_W), BF16),
        scratch_shapes=[pltpu.VMEM((n, FOURIER_W), BF16), pltpu.VMEM((n, FOURIER_W), BF16)],
        compiler_params=_params(1),
        name="fourier",
    )(uf, g, fourier_w, cc, sc, cn, sn)


RADIX = 8
COMBINE_ROWS = 16
SQRT_HALF = math.sqrt(0.5)


def _radix8_real(yr, yn):
    sr = [yr[n] + yr[n + 4] for n in range(4)]
    dr = [yr[n] - yr[n + 4] for n in range(4)]
    sn = {n: yn[n] + yn[n + 4] for n in (1, 3)}
    dn = {n: yn[n] - yn[n + 4] for n in (1, 2, 3)}
    t1r, t1i = (dr[1] - dn[1]) * SQRT_HALF, (dr[1] + dn[1]) * -SQRT_HALF
    t3r, t3i = (dr[3] + dn[3]) * -SQRT_HALF, (dn[3] - dr[3]) * SQRT_HALF
    out = [None] * RADIX
    ea, eb = sr[0] + sr[2], sr[1] + sr[3]
    ec, ed = sr[0] - sr[2], sn[3] - sn[1]
    out[0], out[4] = ea + eb, ea - eb
    out[2], out[6] = ec + ed, ec - ed
    oa, ob = dr[0] - dn[2], t1r + t3r
    oc, od = dr[0] + dn[2], t1i - t3i
    out[1], out[5] = oa + ob, oa - ob
    out[3], out[7] = oc + od, oc - od
    return out


def _fourier_radix_kernel(*refs, n1):
    u_refs = refs[:MIX_GROUPS]
    g_ref, fw_ref, cc_ref, sc_ref, cs_ref, o_ref, ub_ref = refs[MIX_GROUPS:MIX_GROUPS + 7]
    scratch = refs[MIX_GROUPS + 7:]
    pq_refs, yr_refs, yn_refs = (scratch[i * MIX_GROUPS:(i + 1) * MIX_GROUPS] for i in range(3))
    group_lanes = [slice(gi * GROUP_C, (gi + 1) * GROUP_C) for gi in range(MIX_GROUPS)]

    for n2 in range(RADIX):
        for gi, lanes in enumerate(group_lanes):
            ub_ref[n2, :, lanes] = u_refs[gi][pl.ds(n2, n1, stride=RADIX), :].astype(BF16)
        pq = _dot(cs_ref[n2], ub_ref[n2])
        rows = slice(n2 * n1, (n2 + 1) * n1)
        for gi, lanes in enumerate(group_lanes):
            pq_refs[gi][rows, :GROUP_C] = pq[:n1, lanes].astype(BF16)
            pq_refs[gi][rows, GROUP_C:] = pq[n1:, lanes].astype(BF16)

    for gi, lanes in enumerate(group_lanes):
        w = fw_ref[gi].astype(BF16)
        a, b = _dot(cc_ref[...], w), _dot(sc_ref[...], w)
        mix = jnp.concatenate([jnp.concatenate([a, b], axis=1),
                               jnp.concatenate([-b, a], axis=1)], axis=0).astype(BF16)
        y = _dot(pq_refs[gi][...], mix)
        for n2 in range(RADIX):
            yr_refs[gi][n2] = y[n2 * n1:(n2 + 1) * n1, :GROUP_C]
            yn_refs[gi][n2] = y[n2 * n1:(n2 + 1) * n1, GROUP_C:]

        for r0 in range(0, n1, COMBINE_ROWS):
            rows = slice(r0, r0 + COMBINE_ROWS)
            yr = [yr_refs[gi][n2, rows, :] for n2 in range(RADIX)]
            yn = [yn_refs[gi][n2, rows, :] for n2 in range(RADIX)]
            for k2, val in enumerate(_radix8_real(yr, yn)):
                out_rows = slice(k2 * n1 + r0, k2 * n1 + r0 + COMBINE_ROWS)
                o_ref[out_rows, lanes] = (val * _silu(g_ref[out_rows, lanes])).astype(BF16)


def _fourier_radix(uf, g, fourier_w, tabs, l, *, batch, n):
    cc, sc, cs = tabs
    n1 = n // RADIX
    seq_spec = lambda col: pl.BlockSpec((n, FOURIER_W), lambda b: (b, col))
    group_specs = [pl.BlockSpec((n, GROUP_C), lambda b, gi=gi: (b, gi)) for gi in range(MIX_GROUPS)]
    return pl.pallas_call(
        functools.partial(_fourier_radix_kernel, n1=n1),
        grid=(batch,),
        in_specs=group_specs + [
            seq_spec((ATTN_W + POOL_W) // FOURIER_W),
            _layer((MIX_GROUPS, GROUP_C, GROUP_C), l),
            _resident((GROUP_C, GROUP_C)), _resident((GROUP_C, GROUP_C)),
            _resident((RADIX, 2 * n1, n1))],
        out_specs=seq_spec(0),
        out_shape=jax.ShapeDtypeStruct((batch * n, FOURIER_W), BF16),
        scratch_shapes=([pltpu.VMEM((RADIX, n1, FOURIER_W), BF16)]
                        + [pltpu.VMEM((n, 2 * GROUP_C), BF16)] * MIX_GROUPS
                        + [pltpu.VMEM((RADIX, n1, GROUP_C), F32)] * (2 * MIX_GROUPS)),
        compiler_params=_params(1),
        name="fourier_radix",
    )(*([uf] * MIX_GROUPS), g, fourier_w, cc, sc, cs)


def _outproj_kernel(*refs, tiles_per_seq, seq_len, per_batch_mod, final):
    (att_ref, up_ref, prev_ref, next_ref, gp_ref, four_ref, w_ref, pw_ref, ps_ref,
     x_ref, mod_ref) = refs[:11]
    fg_ref = refs[11] if final else None
    o_ref, pad_ref, s2_ref, s4_ref, s8_ref, pooled_ref = refs[-6:]
    step = pl.program_id(0)
    pool_steps = _pool_steps(up_ref, prev_ref, next_ref, gp_ref, pw_ref, ps_ref, pooled_ref,
                             pad_ref, s2_ref, s4_ref, s8_ref, tile_in_seq=step % tiles_per_seq,
                             tiles_per_seq=tiles_per_seq, seq_len=seq_len)
    col_chunks = [slice(c0, c0 + OUT_CHUNK) for c0 in range(0, D_MODEL, OUT_CHUNK)]

    pool_steps.pop(0)()
    for cols in col_chunks:
        o_ref[:, cols] = (_dot(att_ref[...], w_ref[0:ATTN_W, cols])
                          + _dot(four_ref[...], w_ref[ATTN_W + POOL_W:D_MODEL, cols]))
        if pool_steps:
            pool_steps.pop(0)()
    for pool_step in pool_steps:
        pool_step()

    row = step // tiles_per_seq if per_batch_mod else CTX_MOD_ROW
    sum_sq = None
    for cols in col_chunks:
        gate = mod_ref[pl.ds(row, 1), 2 * D_MODEL + cols.start:2 * D_MODEL + cols.stop]
        out = o_ref[:, cols] + _dot(pooled_ref[...], w_ref[ATTN_W:ATTN_W + POOL_W, cols])
        y = x_ref[:, cols] + gate * out
        o_ref[:, cols] = y
        if final:
            part = jnp.sum(y * y, axis=-1, keepdims=True)
            sum_sq = part if sum_sq is None else sum_sq + part
    if final:
        inv_rms = lax.rsqrt(sum_sq * (1.0 / D_MODEL) + EPS)
        o_ref[...] = (o_ref[...] * inv_rms) * fg_ref[...]


def _outproj(att, up, g, four, w_out, pool_w, pool_scale, x2d, mod, final_g, l,
             *, tm, seq_len, per_batch_mod):
    rows = x2d.shape[0]
    final = final_g is not None
    tiles_per_seq = seq_len // tm
    halo_blocks = tm // POOL_HALO
    row_spec = lambda w: pl.BlockSpec((tm, w), lambda i: (i, 0))
    halo_spec = lambda index: pl.BlockSpec((POOL_HALO, POOL_W), lambda i: (index(i), 0))
    in_specs = [row_spec(ATTN_W), row_spec(POOL_W),
                halo_spec(lambda i: jnp.maximum(i * halo_blocks - 1, 0)),
                halo_spec(lambda i: jnp.minimum((i + 1) * halo_blocks, rows // POOL_HALO - 1)),
                pl.BlockSpec((tm, POOL_W), lambda i: (i, ATTN_W // POOL_W)), row_spec(FOURIER_W),
                _layer((D_MODEL, D_MODEL), 0), _layer((MIX_GROUPS, GROUP_C, GROUP_C), l),
                _layer((1, POOL_W), l), row_spec(D_MODEL), _layer((MOD_ROWS, 3 * D_MODEL), 0)]
    args = [att, up, up, up, g, four, w_out, pool_w, pool_scale, x2d, mod]
    if final:
        in_specs.append(_resident((1, D_MODEL)))
        args.append(final_g)
    return pl.pallas_call(
        functools.partial(_outproj_kernel, tiles_per_seq=tiles_per_seq, seq_len=seq_len,
                          per_batch_mod=per_batch_mod, final=final),
        grid=(rows // tm,),
        in_specs=in_specs,
        out_specs=row_spec(D_MODEL),
        out_shape=jax.ShapeDtypeStruct((rows, D_MODEL), F32),
        scratch_shapes=_pool_scratch(tm),
        compiler_params=_params(1),
        name="outproj_final" if final else "outproj",
    )(*args)


def _rope_tables(n):
    pos = jnp.arange(n, dtype=jnp.int32)
    row = (pos // GRID_W).astype(F32)
    col = (pos % GRID_W).astype(F32)
    inv = ROPE_THETA ** (-jnp.arange(0, AXIS_ROT, 2, dtype=F32) / AXIS_ROT)
    ang_r = row[:, None] * inv
    ang_c = col[:, None] * inv
    cos = jnp.concatenate([jnp.cos(ang_r), jnp.cos(ang_r), jnp.cos(ang_c), jnp.cos(ang_c)], axis=1)
    sin = jnp.concatenate([-jnp.sin(ang_r), jnp.sin(ang_r), -jnp.sin(ang_c), jnp.sin(ang_c)], axis=1)
    return cos, sin


def _dft_cos_sin(n):
    k = jnp.arange(n, dtype=jnp.int32)
    ang = ((k[:, None] * k[None, :]) % n).astype(F32) * (2.0 * math.pi / n)
    return jnp.cos(ang), jnp.sin(ang)


def _fourier_tables(n):
    cc, sc = _dft_cos_sin(GROUP_C)
    cn, sn = _dft_cos_sin(n)
    norm = 1.0 / math.sqrt(n * GROUP_C)
    return (cc * norm).astype(BF16), (sc * norm).astype(BF16), cn.astype(BF16), sn.astype(BF16)


def _fourier_radix_tables(n):
    n1 = n // RADIX
    cc, sc = _dft_cos_sin(GROUP_C)
    norm = 1.0 / math.sqrt(n * GROUP_C)
    c1, s1 = _dft_cos_sin(n1)
    k1 = jnp.arange(n1, dtype=jnp.int32)[None, :]
    n2 = jnp.arange(RADIX, dtype=jnp.int32)[:, None]
    twiddle = (k1 * n2).astype(F32) * (2.0 * math.pi / n)
    ct, st = jnp.cos(twiddle)[:, :, None], jnp.sin(twiddle)[:, :, None]
    cs = jnp.concatenate([c1[None] * ct - s1[None] * st, s1[None] * ct + c1[None] * st], axis=1)
    return (cc * norm).astype(BF16), (sc * norm).astype(BF16), cs.astype(BF16)


def kernel(x, c, ctx, c_ctx, ada_w, ada_b, norm_g, w_in, q_norm_g, k_norm_g, pool_w, pool_scale,
           fourier_w, w_out, final_norm_g):
    B, N, _ = x.shape
    C = ctx.shape[1]
    tm = 512
    tm_ctx = 256

    cond = jnp.concatenate([c, c_ctx[None, :], jnp.zeros((MOD_ROWS - B - 1, D_MODEL), F32)], axis=0)
    ada_b3 = ada_b.reshape(DEPTH, 1, 3 * D_MODEL)
    mod = _adaln(cond, ada_w, ada_b3, 0)

    rope_tabs = _rope_tables(N)
    four_tabs = _fourier_radix_tables(N)
    four_tabs_ctx = _fourier_tables(C)
    w_in_b = w_in[0:1].astype(BF16)
    w_out_b = None
    ng = norm_g[:, None, :]
    qg = q_norm_g[:, None, :]
    kg = k_norm_g[:, None, :]
    ps = pool_scale[:, None, :]

    xl = x.reshape(B * N, D_MODEL)
    xc = ctx.reshape(B * C, D_MODEL)
    for l in range(DEPTH):
        last = l == DEPTH - 1
        if last:
            kc, vct = _ctx_kv(xc, mod, ng, w_in_b, kg, l, seqs_per_tile=2, seq_len=C)
        else:
            (qc, kc, vct, upc, ufc, gc), _ = _inproj(xc, mod, ng, w_in_b, qg, kg, None, l,
                                                     tm=tm_ctx, seq_len=C, per_batch_mod=False)
        casts = [] if w_out_b is not None else [(w_out, l)]
        casts += [] if last else [(w_in, l + 1), (w_out, l + 1)]
        (q, k, vt, up, uf, g), cast_weights = _inproj(
            xl, mod, ng, w_in_b, qg, kg, rope_tabs, l, tm=tm, seq_len=N, per_batch_mod=True,
            casts=casts)
        if w_out_b is None:
            w_out_b, cast_weights = cast_weights[0], cast_weights[1:]

        att, next_mod = _attention(q, g, [(kc, vct, C), (k, vt, N)], batch=B, q_len=N, tq=256,
                                   adaln_next=None if last else (cond, ada_w, ada_b3, l + 1))
        four = _fourier_radix(uf, g, fourier_w, four_tabs, l, batch=B, n=N)
        xl_new = _outproj(att, up, g, four, w_out_b, pool_w, ps, xl, mod,
                          final_norm_g[None, :] if last else None, l,
                          tm=tm, seq_len=N, per_batch_mod=True)

        if not last:
            attc = _short_attention(qc, gc, kc, vct, batch=B, length=C)
            fourc = _fourier(ufc, gc, fourier_w, four_tabs_ctx, l, batch=B, n=C)
            xc = _outproj(attc, upc, gc, fourc, w_out_b, pool_w, ps, xc, mod, None, l,
                          tm=tm_ctx, seq_len=C, per_batch_mod=False)
            w_in_b, w_out_b = cast_weights
            mod = next_mod
        xl = xl_new
    return xl.reshape(B, N, D_MODEL)
```

```python
import functools
import math

import jax
import jax.numpy as jnp
from jax import lax
from jax.experimental import pallas as pl
from jax.experimental.pallas import tpu as pltpu

D_MODEL = 2048
DEPTH = 2
GRID_W = 64
HEAD_DIM = 128
ATTN_W = 1024
N_HEADS = 8
N_KV_HEADS = 2
GQA_GROUP = 4
KV_W = 256
POOL_W = 512
POOL_WINDOWS = (2, 4, 8, 16)
FOURIER_W = 512
MIX_GROUPS = 4
GROUP_C = 128
OFF_K = ATTN_W
OFF_V = OFF_K + KV_W
OFF_POOL = OFF_V + KV_W
OFF_FOURIER = OFF_POOL + POOL_W
OFF_GATE = OFF_FOURIER + FOURIER_W
IN_W = OFF_GATE + D_MODEL
ROPE_THETA = 10000.0
AXIS_ROT = HEAD_DIM // 2
EPS = 1e-6

MOD_ROWS = 16
CTX_MOD_ROW = 8
SUBLANES = 8
ROW_PARTS = 2
NORM_PIECES = 8
POOL_HALO = 8
KEY_CHUNK = 256
EXP_CHUNK = 256
ONES_ROWS = 16
V7X_VMEM_BYTES = 64 * 1024 * 1024
VMEM_LIMIT = V7X_VMEM_BYTES - 8 * 1024 * 1024

BF16 = jnp.bfloat16
F32 = jnp.float32


def _silu(x):
    return x * jax.nn.sigmoid(x)


def _dot(a, b):
    return jnp.dot(a, b, preferred_element_type=F32)


def _dot_nt(a, b):
    return lax.dot_general(a, b, (((1,), (1,)), ((), ())), preferred_element_type=F32)


def _resident(shape):
    zeros = (0,) * len(shape)
    return pl.BlockSpec(shape, lambda *_: zeros, pipeline_mode=pl.Buffered(1))


def _layer(shape, l, tail=None):
    index = (l,) + (tail or (0,) * len(shape))
    return pl.BlockSpec((None,) + tuple(shape), lambda *_: index, pipeline_mode=pl.Buffered(1))


def _params(n_grid):
    return pltpu.CompilerParams(dimension_semantics=("arbitrary",) * n_grid,
                                vmem_limit_bytes=VMEM_LIMIT)


def _adaln_kernel(c_ref, w_ref, b_ref, o_ref):
    a = _silu(c_ref[...]).astype(BF16)
    o_ref[...] = _dot(a, w_ref[...].astype(BF16)) + b_ref[...]


def _adaln_specs(l, tn, step):
    in_specs = [pl.BlockSpec((MOD_ROWS, D_MODEL), lambda *i: (0, 0)),
                pl.BlockSpec((None, D_MODEL, tn), lambda *i: (l, 0, step(*i))),
                pl.BlockSpec((None, 1, tn), lambda *i: (l, 0, step(*i)))]
    out_spec = pl.BlockSpec((None, MOD_ROWS, tn), lambda *i: (0, 0, step(*i)))
    return in_specs, out_spec, jax.ShapeDtypeStruct((1, MOD_ROWS, 3 * D_MODEL), F32)


def _adaln(cond, ada_w, ada_b, l):
    tn = 1024
    in_specs, out_spec, out_shape = _adaln_specs(l, tn, lambda j: j)
    return pl.pallas_call(
        _adaln_kernel,
        grid=(3 * D_MODEL // tn,),
        in_specs=in_specs,
        out_specs=out_spec,
        out_shape=out_shape,
        compiler_params=_params(1),
        name="adaln",
    )(cond, ada_w, ada_b)


def _modulated_norm(x_ref, mod_ref, ng_ref, row, rows=slice(None)):
    x = x_ref[rows, :]
    ms = jnp.mean(x * x, axis=-1, keepdims=True)
    shift = mod_ref[pl.ds(row, 1), 0:D_MODEL]
    scale = mod_ref[pl.ds(row, 1), D_MODEL:2 * D_MODEL]
    y = (x * lax.rsqrt(ms + EPS)) * ng_ref[...]
    return (y * (1.0 + scale) + shift).astype(BF16)


def _head_norm(p, gain):
    ms = jnp.mean(p * p, axis=-1, keepdims=True)
    return (p * lax.rsqrt(ms + EPS)) * gain


def _rope(y, cos, sin, swap_lo):
    partner = jnp.where(swap_lo, pltpu.roll(y, 96, 1), pltpu.roll(y, 32, 1))
    return y * cos + partner * sin


def _inproj_kernel(*refs, rope, tiles_per_batch, n_convert):
    n_in = 8 if rope else 6
    x_ref, mod_ref, ng_ref, w_ref, qg_ref, kg_ref = refs[:6]
    cos_ref, sin_ref = refs[6:8] if rope else (None, None)
    f32_weight_refs = refs[n_in:n_in + n_convert]
    q_ref, k_ref, vt_ref, up_ref, uf_ref, g_ref = refs[n_in + n_convert:n_in + n_convert + 6]
    bf16_weight_refs = refs[n_in + n_convert + 6:n_in + 2 * n_convert + 6]
    h_ref = refs[-1]
    for src_ref, dst_ref in zip(f32_weight_refs, bf16_weight_refs):
        dst_ref[...] = src_ref[...].astype(BF16)
    tm = x_ref.shape[0]
    row = CTX_MOD_ROW if tiles_per_batch is None else pl.program_id(0) // tiles_per_batch
    q_gain = qg_ref[...] * (HEAD_DIM ** -0.5 * math.log2(math.e))
    k_gain = kg_ref[...]
    chunk = GQA_GROUP * HEAD_DIM

    def norm_steps(rows):
        piece_rows = (rows.stop - rows.start) // NORM_PIECES
        pieces = [slice(r0, r0 + piece_rows) for r0 in range(rows.start, rows.stop, piece_rows)]

        def norm(piece):
            h_ref[piece, :] = _modulated_norm(x_ref, mod_ref, ng_ref, row, piece)

        return [functools.partial(norm, piece) for piece in pieces]

    def matmul_steps(rows):
        n_rows = rows.stop - rows.start
        if rope:
            cos, sin = cos_ref[rows, :], sin_ref[rows, :]
            lane = lax.broadcasted_iota(jnp.int32, (n_rows, HEAD_DIM), 1)
            swap_lo = (lane & (AXIS_ROT // 2)) == 0

        def finish_head(p, gain):
            y = _head_norm(p, gain)
            if rope:
                y = _rope(y, cos, sin, swap_lo)
            return y.astype(BF16)

        def project(c0, width):
            return _dot(h_ref[rows, :], w_ref[:, c0:c0 + width])

        def q_chunk(c0):
            p = project(c0, chunk)
            for hh in range(GQA_GROUP):
                lo = hh * HEAD_DIM
                q_ref[rows, c0 + lo:c0 + lo + HEAD_DIM] = finish_head(p[:, lo:lo + HEAD_DIM], q_gain)

        def kv_chunk():
            p = project(OFF_K, 2 * KV_W)
            for hh in range(N_KV_HEADS):
                lo = hh * HEAD_DIM
                k_ref[rows, lo:lo + HEAD_DIM] = finish_head(p[:, lo:lo + HEAD_DIM], k_gain)
            vt_ref[:, rows] = p[:, KV_W:2 * KV_W].T.astype(BF16)

        def up_chunk():
            up_ref[rows, :] = project(OFF_POOL, POOL_W)

        def uf_chunk():
            uf_ref[rows, :] = project(OFF_FOURIER, FOURIER_W)

        def gate_chunk(c0):
            g_ref[rows, c0:c0 + chunk] = project(OFF_GATE + c0, chunk)

        return ([functools.partial(q_chunk, c0) for c0 in range(0, ATTN_W, chunk)]
                + [kv_chunk, up_chunk, uf_chunk]
                + [functools.partial(gate_chunk, c0) for c0 in range(0, D_MODEL, chunk)])

    part_rows = tm // ROW_PARTS
    parts = [slice(r0, r0 + part_rows) for r0 in range(0, tm, part_rows)]
    for step in norm_steps(parts[0]):
        step()
    for index, rows in enumerate(parts):
        fill = norm_steps(parts[index + 1]) if index + 1 < len(parts) else []
        for step in matmul_steps(rows):
            step()
            if fill:
                fill.pop(0)()
        for step in fill:
            step()


def _vt_spec(tm, tiles_per_seq):
    return pl.BlockSpec((None, KV_W, tm), lambda i: (i // tiles_per_seq, 0, i % tiles_per_seq))


def _inproj(x2d, mod, norm_g, w_in, q_g, k_g, rope_tabs, l, *, tm, seq_len, per_batch_mod,
            casts=()):
    rows = x2d.shape[0]
    steps = rows // tm
    rope = rope_tabs is not None
    tiles_per_seq = seq_len // tm
    row_spec = lambda w: pl.BlockSpec((tm, w), lambda i: (i, 0))
    in_specs = [row_spec(D_MODEL), _layer((MOD_ROWS, 3 * D_MODEL), 0), _layer((1, D_MODEL), l),
                _layer((D_MODEL, IN_W), 0), _layer((1, HEAD_DIM), l), _layer((1, HEAD_DIM), l)]
    args = [x2d, mod, norm_g, w_in, q_g, k_g]
    if rope:
        tab_spec = pl.BlockSpec((tm, HEAD_DIM), lambda i: (i % tiles_per_seq, 0))
        in_specs += [tab_spec, tab_spec]
        args += list(rope_tabs)
    row_out = lambda w, dt: jax.ShapeDtypeStruct((rows, w), dt)
    out_specs = [row_spec(ATTN_W), row_spec(KV_W), _vt_spec(tm, tiles_per_seq),
                 row_spec(POOL_W), row_spec(FOURIER_W), row_spec(D_MODEL)]
    out_shape = [row_out(ATTN_W, BF16), row_out(KV_W, BF16),
                 jax.ShapeDtypeStruct((rows // seq_len, KV_W, seq_len), BF16),
                 row_out(POOL_W, F32), row_out(FOURIER_W, F32), row_out(D_MODEL, F32)]
    for param, layer in casts:
        _, p_rows, p_cols = param.shape
        slab = (None, p_rows // steps, p_cols)
        in_specs.append(pl.BlockSpec(slab, lambda i, layer=layer: (layer, i, 0)))
        out_specs.append(pl.BlockSpec(slab, lambda i: (0, i, 0)))
        out_shape.append(jax.ShapeDtypeStruct((1, p_rows, p_cols), BF16))
        args.append(param)
    outs = pl.pallas_call(
        functools.partial(_inproj_kernel, rope=rope, n_convert=len(casts),
                          tiles_per_batch=tiles_per_seq if per_batch_mod else None),
        grid=(steps,),
        in_specs=in_specs,
        out_specs=out_specs,
        out_shape=out_shape,
        scratch_shapes=[pltpu.VMEM((tm, D_MODEL), BF16)],
        compiler_params=_params(1),
        name="inproj_rope" if rope else "inproj_ctx",
    )(*args)
    return outs[:6], outs[6:]


def _ctx_kv_kernel(x_ref, mod_ref, ng_ref, w_ref, kg_ref, k_ref, vt_ref):
    h = _modulated_norm(x_ref, mod_ref, ng_ref, CTX_MOD_ROW)
    p = _dot(h, w_ref[...])
    k_gain = kg_ref[...]
    for hh in range(N_KV_HEADS):
        lo = hh * HEAD_DIM
        k_ref[:, lo:lo + HEAD_DIM] = _head_norm(p[:, lo:lo + HEAD_DIM], k_gain).astype(BF16)
    n_seqs, _, seq_len = vt_ref.shape
    for seq in range(n_seqs):
        vt_ref[seq] = p[seq * seq_len:(seq + 1) * seq_len, KV_W:2 * KV_W].T.astype(BF16)


def _ctx_kv(x2d, mod, norm_g, w_in, k_g, l, *, seqs_per_tile, seq_len):
    rows = x2d.shape[0]
    tm = seqs_per_tile * seq_len
    row_spec = lambda w: pl.BlockSpec((tm, w), lambda i: (i, 0))
    kv_cols = (0, OFF_K // (2 * KV_W))
    return pl.pallas_call(
        _ctx_kv_kernel,
        grid=(rows // tm,),
        in_specs=[row_spec(D_MODEL), _layer((MOD_ROWS, 3 * D_MODEL), 0), _layer((1, D_MODEL), l),
                  _layer((D_MODEL, 2 * KV_W), 0, kv_cols), _layer((1, HEAD_DIM), l)],
        out_specs=[row_spec(KV_W),
                   pl.BlockSpec((seqs_per_tile, KV_W, seq_len), lambda i: (i, 0, 0))],
        out_shape=[jax.ShapeDtypeStruct((rows, KV_W), BF16),
                   jax.ShapeDtypeStruct((rows // seq_len, KV_W, seq_len), BF16)],
        compiler_params=_params(1),
        name="ctx_kv",
    )(x2d, mod, norm_g, w_in, k_g)


def _attn_kernel(*refs, lengths, tq, with_adaln):
    n_sources = len(lengths)
    q_ref, g_ref = refs[0], refs[1]
    k_refs = refs[2:2 + 2 * n_sources:2]
    vt_refs = refs[3:3 + 2 * n_sources:2]
    n_in = 2 + 2 * n_sources
    o_ref, s_ref, p_ref, vta_ref = refs[-4:]
    if with_adaln:
        _adaln_kernel(*refs[n_in:n_in + 4])
    offsets = [sum(lengths[:i]) for i in range(n_sources)]
    total = sum(lengths)
    n_tiles = q_ref.shape[0] // tq
    key_chunks = [(k_ref, off, r0, min(KEY_CHUNK, length))
                  for k_ref, off, length in zip(k_refs, offsets, lengths)
                  for r0 in range(0, length, KEY_CHUNK)]
    head_cols = [slice(hh * HEAD_DIM, (hh + 1) * HEAD_DIM) for hh in range(GQA_GROUP)]

    def q_rows(tile):
        return pl.ds(pl.multiple_of(tile * tq, tq), tq)

    def fold_rows(x, op):
        return op(x.reshape(x.shape[0] // SUBLANES, SUBLANES, tq), axis=0)

    def score_chunk(tile, hh, slot, chunk, col_max):
        k_ref, off, r0, size = key_chunks[chunk]
        s = _dot_nt(k_ref[r0:r0 + size, :], q_ref[q_rows(tile), head_cols[hh]])
        s_ref[slot, off + r0:off + r0 + size, :] = s
        cm = fold_rows(s, jnp.max)
        return cm if col_max is None else jnp.maximum(col_max, cm)

    def prob_chunk(slot, chunk, col_max):
        _, off, start, size = key_chunks[chunk]
        for r0 in range(off + start, off + start + size, EXP_CHUNK):
            p = jnp.exp2(s_ref[slot, r0:r0 + EXP_CHUNK, :] - col_max)
            p_ref[slot, r0:r0 + EXP_CHUNK, :] = p.astype(BF16)

    def finish(tile, hh, slot):
        acc = _dot(vta_ref[...], p_ref[slot])
        out = (acc[:HEAD_DIM] / acc[HEAD_DIM:HEAD_DIM + 1]).T
        gate = _silu(g_ref[q_rows(tile), head_cols[hh]])
        o_ref[q_rows(tile), head_cols[hh]] = (out * gate).astype(BF16)

    for vt_ref, off, length in zip(vt_refs, offsets, lengths):
        vta_ref[0:HEAD_DIM, off:off + length] = vt_ref[...]
    vta_ref[HEAD_DIM:HEAD_DIM + ONES_ROWS, :] = jnp.ones((ONES_ROWS, total), BF16)

    n_chunks = len(key_chunks)
    first_max = None
    for chunk in range(n_chunks):
        first_max = score_chunk(0, 0, 0, chunk, first_max)
    first_max = jnp.max(first_max, axis=0, keepdims=True)

    def tile_body(tile, cur_max):
        for hh in range(GQA_GROUP):
            slot, next_slot = hh % 2, (hh + 1) % 2
            next_hh = (hh + 1) % GQA_GROUP
            next_tile = tile if next_hh else jnp.minimum(tile + 1, n_tiles - 1)
            next_max = None
            for chunk in range(n_chunks):
                next_max = score_chunk(next_tile, next_hh, next_slot, chunk, next_max)
                prob_chunk(slot, chunk, cur_max)
            finish(tile, hh, slot)
            cur_max = jnp.max(next_max, axis=0, keepdims=True)
        return cur_max

    lax.fori_loop(0, n_tiles, tile_body, first_max)


def _attention(q, g, sources, *, batch, q_len, tq, adaln_next=None):
    q_spec = pl.BlockSpec((q_len, GQA_GROUP * HEAD_DIM), lambda b, h: (b, h))
    in_specs = [q_spec, q_spec]
    args = [q, g]
    for k, vt, length in sources:
        in_specs += [pl.BlockSpec((length, HEAD_DIM), lambda b, h: (b, h)),
                     pl.BlockSpec((None, HEAD_DIM, length), lambda b, h: (b, h, 0))]
        args += [k, vt]
    out_specs = [q_spec]
    out_shape = [jax.ShapeDtypeStruct((batch * q_len, ATTN_W), BF16)]
    if adaln_next is not None:
        cond, ada_w, ada_b, layer = adaln_next
        slab = 3 * D_MODEL // (batch * N_KV_HEADS)
        side_in, side_out, side_shape = _adaln_specs(layer, slab, lambda b, h: b * N_KV_HEADS + h)
        in_specs += side_in
        args += [cond, ada_w, ada_b]
        out_specs = [side_out] + out_specs
        out_shape = [side_shape] + out_shape
    lengths = tuple(length for _, _, length in sources)
    total = sum(lengths)
    outs = pl.pallas_call(
        functools.partial(_attn_kernel, lengths=lengths, tq=tq, with_adaln=adaln_next is not None),
        grid=(batch, N_KV_HEADS),
        in_specs=in_specs,
        out_specs=out_specs,
        out_shape=out_shape,
        scratch_shapes=[pltpu.VMEM((2, total, tq), F32), pltpu.VMEM((2, total, tq), BF16),
                        pltpu.VMEM((HEAD_DIM + ONES_ROWS, total), BF16)],
        compiler_params=_params(2),
        name="attention",
    )(*args)
    return outs[-1], (outs[0] if adaln_next is not None else None)


SHORT_ATTN_SEQS = 2


def _short_attn_kernel(q_ref, g_ref, k_ref, vt_ref, o_ref):
    n_seqs, _, length = vt_ref.shape
    for seq in range(n_seqs):
        rows = slice(seq * length, (seq + 1) * length)
        for kvh in range(N_KV_HEADS):
            kv_cols = slice(kvh * HEAD_DIM, (kvh + 1) * HEAD_DIM)
            heads = [slice((kvh * GQA_GROUP + hh) * HEAD_DIM, (kvh * GQA_GROUP + hh + 1) * HEAD_DIM)
                     for hh in range(GQA_GROUP)]
            q_rows = jnp.concatenate([q_ref[rows, cols] for cols in heads], axis=0)
            s = _dot_nt(k_ref[rows, kv_cols], q_rows)
            p = jnp.exp2(s - jnp.max(s, axis=0, keepdims=True)).astype(BF16)
            vt_aug = jnp.concatenate([vt_ref[seq, kv_cols, :],
                                      jnp.ones((ONES_ROWS, length), BF16)], axis=0)
            acc = _dot(vt_aug, p)
            for hh, cols in enumerate(heads):
                span = slice(hh * length, (hh + 1) * length)
                out = (acc[:HEAD_DIM, span] / acc[HEAD_DIM:HEAD_DIM + 1, span]).T
                o_ref[rows, cols] = (out * _silu(g_ref[rows, cols])).astype(BF16)


def _short_attention(q, g, k, vt, *, batch, length):
    seqs = SHORT_ATTN_SEQS
    row_spec = lambda w: pl.BlockSpec((seqs * length, w), lambda i: (i, 0))
    return pl.pallas_call(
        _short_attn_kernel,
        grid=(batch // seqs,),
        in_specs=[row_spec(ATTN_W), row_spec(ATTN_W), row_spec(KV_W),
                  pl.BlockSpec((seqs, KV_W, length), lambda i: (i, 0, 0))],
        out_specs=row_spec(ATTN_W),
        out_shape=jax.ShapeDtypeStruct((batch * length, ATTN_W), BF16),
        compiler_params=_params(1),
        name="short_attention",
    )(q, g, k, vt)


POOL_CHUNK = 256
OUT_CHUNK = 256


def _pool_steps(up_ref, prev_ref, next_ref, g_ref, pw_ref, ps_ref, dst_ref,
                pad_ref, s2_ref, s4_ref, s8_ref, *, tile_in_seq, tiles_per_seq, seq_len):
    tm = up_ref.shape[0]
    halo = POOL_HALO
    chunk = min(tm, POOL_CHUNK)

    def forward_sum(dst, src, shift, length, lane0):
        for r0 in range(0, length, chunk):
            rows = min(chunk, length - r0)
            dst[r0:r0 + rows, :] = (src[r0:r0 + rows, lane0:]
                                    + src[r0 + shift:r0 + shift + rows, lane0:])

    def window_sums():
        pad_ref[0:halo, :] = jnp.where(tile_in_seq > 0, prev_ref[...], 0.0)
        pad_ref[halo:halo + tm, :] = up_ref[...]
        pad_ref[halo + tm:2 * halo + tm, :] = jnp.where(tile_in_seq < tiles_per_seq - 1,
                                                        next_ref[...], 0.0)
        pad_ref[2 * halo + tm:4 * halo + tm, :] = jnp.zeros((2 * halo, POOL_W), F32)
        forward_sum(s2_ref, pad_ref, 1, tm + 3 * halo, 0)
        forward_sum(s4_ref, s2_ref, 2, tm + 2 * halo, GROUP_C)
        forward_sum(s8_ref, s4_ref, 4, tm + halo, GROUP_C)

    def window_sum(gi, r0):
        if gi == 0:
            return s2_ref[halo - 1 + r0:halo - 1 + r0 + chunk, 0:GROUP_C]
        if gi == 1:
            return s4_ref[halo - 2 + r0:halo - 2 + r0 + chunk, 0:GROUP_C]
        if gi == 2:
            return s8_ref[halo - 4 + r0:halo - 4 + r0 + chunk, 0:GROUP_C]
        return (s8_ref[r0:r0 + chunk, GROUP_C:] + s8_ref[halo + r0:halo + r0 + chunk, GROUP_C:])

    def edge_inverse_count(win, local_row):
        before, after = win // 2, win - win // 2 - 1
        t = tile_in_seq * tm + local_row + lax.broadcasted_iota(jnp.int32, (halo, GROUP_C), 0)
        cnt = jnp.minimum(t + after, seq_len - 1) - jnp.maximum(t - before, 0) + 1
        return 1.0 / cnt.astype(F32)

    def pooled_block(gi, win, r0):
        cols = slice(gi * GROUP_C, (gi + 1) * GROUP_C)
        pieces, inner = [], chunk
        if r0 == 0:
            pieces.append(edge_inverse_count(win, 0))
            inner -= halo
        tail = [edge_inverse_count(win, tm - halo)] if r0 + chunk == tm else []
        inner -= halo * len(tail)
        pieces.append(jnp.full((inner, GROUP_C), 1.0 / win, F32))
        inv_cnt = jnp.concatenate(pieces + tail, axis=0)
        pooled = window_sum(gi, r0) * inv_cnt - pad_ref[halo + r0:halo + r0 + chunk, cols]
        y = _dot(pooled.astype(BF16), pw_ref[gi].astype(BF16)) * ps_ref[:, cols]
        dst_ref[r0:r0 + chunk, cols] = (y * _silu(g_ref[r0:r0 + chunk, cols])).astype(BF16)

    return [window_sums] + [functools.partial(pooled_block, gi, win, r0)
                            for gi, win in enumerate(POOL_WINDOWS)
                            for r0 in range(0, tm, chunk)]


def _pool_scratch(tm):
    return [pltpu.VMEM((tm + 4 * POOL_HALO, POOL_W), F32),
            pltpu.VMEM((tm + 3 * POOL_HALO, POOL_W), F32),
            pltpu.VMEM((tm + 2 * POOL_HALO, POOL_W - GROUP_C), F32),
            pltpu.VMEM((tm + POOL_HALO, POOL_W - 2 * GROUP_C), F32),
            pltpu.VMEM((tm, POOL_W), BF16)]


def _fourier_kernel(u_ref, g_ref, fw_ref, cc_ref, sc_ref, cn_ref, sn_ref, o_ref, ua_ref, ub_ref,
                    *, n, chunk):
    for gi in range(MIX_GROUPS):
        cols = slice(gi * GROUP_C, (gi + 1) * GROUP_C)
        w = fw_ref[gi].astype(BF16)
        a = _dot(cc_ref[...], w).astype(BF16)
        b = _dot(sc_ref[...], w).astype(BF16)
        ug = u_ref[:, cols].astype(BF16)
        ua_ref[:, cols] = _dot(ug, a).astype(BF16)
        ub_ref[:, cols] = _dot(ug, b).astype(BF16)
    for r0 in range(0, n, chunk):
        rows = slice(r0, r0 + chunk)
        y = _dot(cn_ref[rows, :], ua_ref[...]) - _dot(sn_ref[rows, :], ub_ref[...])
        o_ref[rows, :] = (y * _silu(g_ref[rows, :])).astype(BF16)


def _fourier(uf, g, fourier_w, tabs, l, *, batch, n):
    cc, sc, cn, sn = tabs
    seq_spec = lambda col: pl.BlockSpec((n, FOURIER_W), lambda b: (b, col))
    return pl.pallas_call(
        functools.partial(_fourier_kernel, n=n, chunk=min(n, 512)),
        grid=(batch,),
        in_specs=[seq_spec(0), seq_spec((ATTN_W + POOL_W) // FOURIER_W),
                  _layer((MIX_GROUPS, GROUP_C, GROUP_C), l),
                  _resident((GROUP_C, GROUP_C)), _resident((GROUP_C, GROUP_C)),
                  _resident((n, n)), _resident((n, n))],
        out_specs=seq_spec(0),
        out_shape=jax.ShapeDtypeStruct((batch * n, FOURIER_W), BF16),
        scratch_shapes=[pltpu.VMEM((n, FOURIER_W), BF16), pltpu.VMEM((n, FOURIER_W), BF16)],
        compiler_params=_params(1),
        name="fourier",
    )(uf, g, fourier_w, cc, sc, cn, sn)


RADIX = 8
COMBINE_ROWS = 16
SQRT_HALF = math.sqrt(0.5)


def _radix8_real(yr, yn):
    sr = [yr[n] + yr[n + 4] for n in range(4)]
    dr = [yr[n] - yr[n + 4] for n in range(4)]
    sn = {n: yn[n] + yn[n + 4] for n in (1, 3)}
    dn = {n: yn[n] - yn[n + 4] for n in (1, 2, 3)}
    t1r, t1i = (dr[1] - dn[1]) * SQRT_HALF, (dr[1] + dn[1]) * -SQRT_HALF
    t3r, t3i = (dr[3] + dn[3]) * -SQRT_HALF, (dn[3] - dr[3]) * SQRT_HALF
    out = [None] * RADIX
    ea, eb = sr[0] + sr[2], sr[1] + sr[3]
    ec, ed = sr[0] - sr[2], sn[3] - sn[1]
    out[0], out[4] = ea + eb, ea - eb
    out[2], out[6] = ec + ed, ec - ed
    oa, ob = dr[0] - dn[2], t1r + t3r
    oc, od = dr[0] + dn[2], t1i - t3i
    out[1], out[5] = oa + ob, oa - ob
    out[3], out[7] = oc + od, oc - od
    return out


def _fourier_radix_kernel(*refs, n1):
    u_refs = refs[:MIX_GROUPS]
    g_ref, fw_ref, cc_ref, sc_ref, cs_ref, o_ref, ub_ref = refs[MIX_GROUPS:MIX_GROUPS + 7]
    scratch = refs[MIX_GROUPS + 7:]
    pq_refs, yr_refs, yn_refs = (scratch[i * MIX_GROUPS:(i + 1) * MIX_GROUPS] for i in range(3))
    group_lanes = [slice(gi * GROUP_C, (gi + 1) * GROUP_C) for gi in range(MIX_GROUPS)]

    for n2 in range(RADIX):
        for gi, lanes in enumerate(group_lanes):
            ub_ref[n2, :, lanes] = u_refs[gi][pl.ds(n2, n1, stride=RADIX), :].astype(BF16)
        pq = _dot(cs_ref[n2], ub_ref[n2])
        rows = slice(n2 * n1, (n2 + 1) * n1)
        for gi, lanes in enumerate(group_lanes):
            pq_refs[gi][rows, :GROUP_C] = pq[:n1, lanes].astype(BF16)
            pq_refs[gi][rows, GROUP_C:] = pq[n1:, lanes].astype(BF16)

    for gi, lanes in enumerate(group_lanes):
        w = fw_ref[gi].astype(BF16)
        a, b = _dot(cc_ref[...], w), _dot(sc_ref[...], w)
        mix = jnp.concatenate([jnp.concatenate([a, b], axis=1),
                               jnp.concatenate([-b, a], axis=1)], axis=0).astype(BF16)
        y = _dot(pq_refs[gi][...], mix)
        for n2 in range(RADIX):
            yr_refs[gi][n2] = y[n2 * n1:(n2 + 1) * n1, :GROUP_C]
            yn_refs[gi][n2] = y[n2 * n1:(n2 + 1) * n1, GROUP_C:]

        for r0 in range(0, n1, COMBINE_ROWS):
            rows = slice(r0, r0 + COMBINE_ROWS)
            yr = [yr_refs[gi][n2, rows, :] for n2 in range(RADIX)]
            yn = [yn_refs[gi][n2, rows, :] for n2 in range(RADIX)]
            for k2, val in enumerate(_radix8_real(yr, yn)):
                out_rows = slice(k2 * n1 + r0, k2 * n1 + r0 + COMBINE_ROWS)
                o_ref[out_rows, lanes] = (val * _silu(g_ref[out_rows, lanes])).astype(BF16)


def _fourier_radix(uf, g, fourier_w, tabs, l, *, batch, n):
    cc, sc, cs = tabs
    n1 = n // RADIX
    seq_spec = lambda col: pl.BlockSpec((n, FOURIER_W), lambda b: (b, col))
    group_specs = [pl.BlockSpec((n, GROUP_C), lambda b, gi=gi: (b, gi)) for gi in range(MIX_GROUPS)]
    return pl.pallas_call(
        functools.partial(_fourier_radix_kernel, n1=n1),
        grid=(batch,),
        in_specs=group_specs + [
            seq_spec((ATTN_W + POOL_W) // FOURIER_W),
            _layer((MIX_GROUPS, GROUP_C, GROUP_C), l),
            _resident((GROUP_C, GROUP_C)), _resident((GROUP_C, GROUP_C)),
            _resident((RADIX, 2 * n1, n1))],
        out_specs=seq_spec(0),
        out_shape=jax.ShapeDtypeStruct((batch * n, FOURIER_W), BF16),
        scratch_shapes=([pltpu.VMEM((RADIX, n1, FOURIER_W), BF16)]
                        + [pltpu.VMEM((n, 2 * GROUP_C), BF16)] * MIX_GROUPS
                        + [pltpu.VMEM((RADIX, n1, GROUP_C), F32)] * (2 * MIX_GROUPS)),
        compiler_params=_params(1),
        name="fourier_radix",
    )(*([uf] * MIX_GROUPS), g, fourier_w, cc, sc, cs)


def _outproj_kernel(*refs, tiles_per_seq, seq_len, per_batch_mod, final):
    (att_ref, up_ref, prev_ref, next_ref, gp_ref, four_ref, w_ref, pw_ref, ps_ref,
     x_ref, mod_ref) = refs[:11]
    fg_ref = refs[11] if final else None
    o_ref, pad_ref, s2_ref, s4_ref, s8_ref, pooled_ref = refs[-6:]
    step = pl.program_id(0)
    pool_steps = _pool_steps(up_ref, prev_ref, next_ref, gp_ref, pw_ref, ps_ref, pooled_ref,
                             pad_ref, s2_ref, s4_ref, s8_ref, tile_in_seq=step % tiles_per_seq,
                             tiles_per_seq=tiles_per_seq, seq_len=seq_len)
    col_chunks = [slice(c0, c0 + OUT_CHUNK) for c0 in range(0, D_MODEL, OUT_CHUNK)]

    pool_steps.pop(0)()
    for cols in col_chunks:
        o_ref[:, cols] = (_dot(att_ref[...], w_ref[0:ATTN_W, cols])
                          + _dot(four_ref[...], w_ref[ATTN_W + POOL_W:D_MODEL, cols]))
        if pool_steps:
            pool_steps.pop(0)()
    for pool_step in pool_steps:
        pool_step()

    row = step // tiles_per_seq if per_batch_mod else CTX_MOD_ROW
    sum_sq = None
    for cols in col_chunks:
        gate = mod_ref[pl.ds(row, 1), 2 * D_MODEL + cols.start:2 * D_MODEL + cols.stop]
        out = o_ref[:, cols] + _dot(pooled_ref[...], w_ref[ATTN_W:ATTN_W + POOL_W, cols])
        y = x_ref[:, cols] + gate * out
        o_ref[:, cols] = y
        if final:
            part = jnp.sum(y * y, axis=-1, keepdims=True)
            sum_sq = part if sum_sq is None else sum_sq + part
    if final:
        inv_rms = lax.rsqrt(sum_sq * (1.0 / D_MODEL) + EPS)
        o_ref[...] = (o_ref[...] * inv_rms) * fg_ref[...]


def _outproj(att, up, g, four, w_out, pool_w, pool_scale, x2d, mod, final_g, l,
             *, tm, seq_len, per_batch_mod):
    rows = x2d.shape[0]
    final = final_g is not None
    tiles_per_seq = seq_len // tm
    halo_blocks = tm // POOL_HALO
    row_spec = lambda w: pl.BlockSpec((tm, w), lambda i: (i, 0))
    halo_spec = lambda index: pl.BlockSpec((POOL_HALO, POOL_W), lambda i: (index(i), 0))
    in_specs = [row_spec(ATTN_W), row_spec(POOL_W),
                halo_spec(lambda i: jnp.maximum(i * halo_blocks - 1, 0)),
                halo_spec(lambda i: jnp.minimum((i + 1) * halo_blocks, rows // POOL_HALO - 1)),
                pl.BlockSpec((tm, POOL_W), lambda i: (i, ATTN_W // POOL_W)), row_spec(FOURIER_W),
                _layer((D_MODEL, D_MODEL), 0), _layer((MIX_GROUPS, GROUP_C, GROUP_C), l),
                _layer((1, POOL_W), l), row_spec(D_MODEL), _layer((MOD_ROWS, 3 * D_MODEL), 0)]
    args = [att, up, up, up, g, four, w_out, pool_w, pool_scale, x2d, mod]
    if final:
        in_specs.append(_resident((1, D_MODEL)))
        args.append(final_g)
    return pl.pallas_call(
        functools.partial(_outproj_kernel, tiles_per_seq=tiles_per_seq, seq_len=seq_len,
                          per_batch_mod=per_batch_mod, final=final),
        grid=(rows // tm,),
        in_specs=in_specs,
        out_specs=row_spec(D_MODEL),
        out_shape=jax.ShapeDtypeStruct((rows, D_MODEL), F32),
        scratch_shapes=_pool_scratch(tm),
        compiler_params=_params(1),
        name="outproj_final" if final else "outproj",
    )(*args)


def _rope_tables(n):
    grid_h = n // GRID_W
    inv = ROPE_THETA ** (-jnp.arange(0, AXIS_ROT, 2, dtype=F32) / AXIS_ROT)
    ang_r = jnp.arange(grid_h, dtype=F32)[:, None] * inv
    ang_c = jnp.arange(GRID_W, dtype=F32)[:, None] * inv
    by_row = lambda t: jnp.repeat(t, GRID_W, axis=0)
    by_col = lambda t: jnp.tile(t, (grid_h, 1))
    cos_r, sin_r = by_row(jnp.cos(ang_r)), by_row(jnp.sin(ang_r))
    cos_c, sin_c = by_col(jnp.cos(ang_c)), by_col(jnp.sin(ang_c))
    cos = jnp.concatenate([cos_r, cos_r, cos_c, cos_c], axis=1)
    sin = jnp.concatenate([-sin_r, sin_r, -sin_c, sin_c], axis=1)
    return cos, sin


def _dft_cos_sin(n):
    k = jnp.arange(n, dtype=jnp.int32)
    ang = ((k[:, None] * k[None, :]) % n).astype(F32) * (2.0 * math.pi / n)
    return jnp.cos(ang), jnp.sin(ang)


def _fourier_tables(n):
    cc, sc = _dft_cos_sin(GROUP_C)
    cn, sn = _dft_cos_sin(n)
    norm = 1.0 / math.sqrt(n * GROUP_C)
    return (cc * norm).astype(BF16), (sc * norm).astype(BF16), cn.astype(BF16), sn.astype(BF16)


def _fourier_radix_tables(n):
    n1 = n // RADIX
    cc, sc = _dft_cos_sin(GROUP_C)
    norm = 1.0 / math.sqrt(n * GROUP_C)
    c1, s1 = _dft_cos_sin(n1)
    k1 = jnp.arange(n1, dtype=jnp.int32)[None, :]
    n2 = jnp.arange(RADIX, dtype=jnp.int32)[:, None]
    twiddle = (k1 * n2).astype(F32) * (2.0 * math.pi / n)
    ct, st = jnp.cos(twiddle)[:, :, None], jnp.sin(twiddle)[:, :, None]
    cs = jnp.concatenate([c1[None] * ct - s1[None] * st, s1[None] * ct + c1[None] * st], axis=1)
    return (cc * norm).astype(BF16), (sc * norm).astype(BF16), cs.astype(BF16)


def kernel(x, c, ctx, c_ctx, ada_w, ada_b, norm_g, w_in, q_norm_g, k_norm_g, pool_w, pool_scale,
           fourier_w, w_out, final_norm_g):
    B, N, _ = x.shape
    C = ctx.shape[1]
    tm = 512
    tm_ctx = 256

    cond = jnp.concatenate([c, c_ctx[None, :], jnp.zeros((MOD_ROWS - B - 1, D_MODEL), F32)], axis=0)
    ada_b3 = ada_b.reshape(DEPTH, 1, 3 * D_MODEL)
    mod = _adaln(cond, ada_w, ada_b3, 0)

    rope_tabs = _rope_tables(N)
    four_tabs = _fourier_radix_tables(N)
    four_tabs_ctx = _fourier_tables(C)
    w_in_b = w_in[0:1].astype(BF16)
    w_out_b = None
    ng = norm_g[:, None, :]
    qg = q_norm_g[:, None, :]
    kg = k_norm_g[:, None, :]
    ps = pool_scale[:, None, :]

    xl = x.reshape(B * N, D_MODEL)
    xc = ctx.reshape(B * C, D_MODEL)
    for l in range(DEPTH):
        last = l == DEPTH - 1
        if last:
            kc, vct = _ctx_kv(xc, mod, ng, w_in_b, kg, l, seqs_per_tile=2, seq_len=C)
        else:
            (qc, kc, vct, upc, ufc, gc), _ = _inproj(xc, mod, ng, w_in_b, qg, kg, None, l,
                                                     tm=tm_ctx, seq_len=C, per_batch_mod=False)
        casts = [] if w_out_b is not None else [(w_out, l)]
        casts += [] if last else [(w_in, l + 1), (w_out, l + 1)]
        (q, k, vt, up, uf, g), cast_weights = _inproj(
            xl, mod, ng, w_in_b, qg, kg, rope_tabs, l, tm=tm, seq_len=N, per_batch_mod=True,
            casts=casts)
        if w_out_b is None:
            w_out_b, cast_weights = cast_weights[0], cast_weights[1:]

        att, next_mod = _attention(q, g, [(kc, vct, C), (k, vt, N)], batch=B, q_len=N, tq=256,
                                   adaln_next=None if last else (cond, ada_w, ada_b3, l + 1))
        four = _fourier_radix(uf, g, fourier_w, four_tabs, l, batch=B, n=N)
        xl_new = _outproj(att, up, g, four, w_out_b, pool_w, ps, xl, mod,
                          final_norm_g[None, :] if last else None, l,
                          tm=tm, seq_len=N, per_batch_mod=True)

        if not last:
            attc = _short_attention(qc, gc, kc, vct, batch=B, length=C)
            fourc = _fourier(ufc, gc, fourier_w, four_tabs_ctx, l, batch=B, n=C)
            xc = _outproj(attc, upc, gc, fourc, w_out_b, pool_w, ps, xc, mod, None, l,
                          tm=tm_ctx, seq_len=C, per_batch_mod=False)
            w_in_b, w_out_b = cast_weights
            mod = next_mod
        xl = xl_new
    return xl.reshape(B, N, D_MODEL)
```

```python
import functools
import math

import jax
import jax.numpy as jnp
from jax import lax
from jax.experimental import pallas as pl
from jax.experimental.pallas import tpu as pltpu

D_MODEL = 2048
DEPTH = 2
GRID_W = 64
HEAD_DIM = 128
ATTN_W = 1024
N_HEADS = 8
N_KV_HEADS = 2
GQA_GROUP = 4
KV_W = 256
POOL_W = 512
POOL_WINDOWS = (2, 4, 8, 16)
FOURIER_W = 512
MIX_GROUPS = 4
GROUP_C = 128
OFF_K = ATTN_W
OFF_V = OFF_K + KV_W
OFF_POOL = OFF_V + KV_W
OFF_FOURIER = OFF_POOL + POOL_W
OFF_GATE = OFF_FOURIER + FOURIER_W
IN_W = OFF_GATE + D_MODEL
ROPE_THETA = 10000.0
AXIS_ROT = HEAD_DIM // 2
EPS = 1e-6

MOD_ROWS = 16
CTX_MOD_ROW = 8
SUBLANES = 8
ROW_PARTS = 2
NORM_PIECES = 8
POOL_HALO = 8
KEY_CHUNK = 256
EXP_CHUNK = 256
ONES_ROWS = 16
V7X_VMEM_BYTES = 64 * 1024 * 1024
VMEM_LIMIT = V7X_VMEM_BYTES - 8 * 1024 * 1024

BF16 = jnp.bfloat16
F32 = jnp.float32


def _silu(x):
    return x * jax.nn.sigmoid(x)


def _dot(a, b):
    return jnp.dot(a, b, preferred_element_type=F32)


def _dot_nt(a, b):
    return lax.dot_general(a, b, (((1,), (1,)), ((), ())), preferred_element_type=F32)


def _resident(shape):
    zeros = (0,) * len(shape)
    return pl.BlockSpec(shape, lambda *_: zeros, pipeline_mode=pl.Buffered(1))


def _layer(shape, l, tail=None):
    index = (l,) + (tail or (0,) * len(shape))
    return pl.BlockSpec((None,) + tuple(shape), lambda *_: index, pipeline_mode=pl.Buffered(1))


def _params(n_grid):
    return pltpu.CompilerParams(dimension_semantics=("arbitrary",) * n_grid,
                                vmem_limit_bytes=VMEM_LIMIT)


def _adaln_kernel(c_ref, w_ref, b_ref, o_ref):
    a = _silu(c_ref[...]).astype(BF16)
    o_ref[...] = _dot(a, w_ref[...].astype(BF16)) + b_ref[...]


def _adaln_specs(l, tn, step):
    in_specs = [pl.BlockSpec((MOD_ROWS, D_MODEL), lambda *i: (0, 0)),
                pl.BlockSpec((None, D_MODEL, tn), lambda *i: (l, 0, step(*i))),
                pl.BlockSpec((None, 1, tn), lambda *i: (l, 0, step(*i)))]
    out_spec = pl.BlockSpec((None, MOD_ROWS, tn), lambda *i: (0, 0, step(*i)))
    return in_specs, out_spec, jax.ShapeDtypeStruct((1, MOD_ROWS, 3 * D_MODEL), F32)


def _adaln(cond, ada_w, ada_b, l):
    tn = 1024
    in_specs, out_spec, out_shape = _adaln_specs(l, tn, lambda j: j)
    return pl.pallas_call(
        _adaln_kernel,
        grid=(3 * D_MODEL // tn,),
        in_specs=in_specs,
        out_specs=out_spec,
        out_shape=out_shape,
        compiler_params=_params(1),
        name="adaln",
    )(cond, ada_w, ada_b)


def _modulated_norm(x_ref, mod_ref, ng_ref, row, rows=slice(None)):
    x = x_ref[rows, :]
    ms = jnp.mean(x * x, axis=-1, keepdims=True)
    shift = mod_ref[pl.ds(row, 1), 0:D_MODEL]
    scale = mod_ref[pl.ds(row, 1), D_MODEL:2 * D_MODEL]
    y = (x * lax.rsqrt(ms + EPS)) * ng_ref[...]
    return (y * (1.0 + scale) + shift).astype(BF16)


def _head_norm(p, gain):
    ms = jnp.mean(p * p, axis=-1, keepdims=True)
    return (p * lax.rsqrt(ms + EPS)) * gain


def _rope(y, cos, sin, swap_lo):
    partner = jnp.where(swap_lo, pltpu.roll(y, 96, 1), pltpu.roll(y, 32, 1))
    return y * cos + partner * sin


def _inproj_kernel(*refs, rope, tiles_per_batch, n_convert):
    n_in = 8 if rope else 6
    x_ref, mod_ref, ng_ref, w_ref, qg_ref, kg_ref = refs[:6]
    cos_ref, sin_ref = refs[6:8] if rope else (None, None)
    f32_weight_refs = refs[n_in:n_in + n_convert]
    q_ref, k_ref, vt_ref, up_ref, uf_ref, g_ref = refs[n_in + n_convert:n_in + n_convert + 6]
    bf16_weight_refs = refs[n_in + n_convert + 6:n_in + 2 * n_convert + 6]
    h_ref = refs[-1]
    for src_ref, dst_ref in zip(f32_weight_refs, bf16_weight_refs):
        dst_ref[...] = src_ref[...].astype(BF16)
    tm = x_ref.shape[0]
    row = CTX_MOD_ROW if tiles_per_batch is None else pl.program_id(0) // tiles_per_batch
    q_gain = qg_ref[...] * (HEAD_DIM ** -0.5 * math.log2(math.e))
    k_gain = kg_ref[...]
    chunk = GQA_GROUP * HEAD_DIM

    def norm_steps(rows):
        piece_rows = (rows.stop - rows.start) // NORM_PIECES
        pieces = [slice(r0, r0 + piece_rows) for r0 in range(rows.start, rows.stop, piece_rows)]

        def norm(piece):
            h_ref[piece, :] = _modulated_norm(x_ref, mod_ref, ng_ref, row, piece)

        return [functools.partial(norm, piece) for piece in pieces]

    def matmul_steps(rows):
        n_rows = rows.stop - rows.start
        if rope:
            cos, sin = cos_ref[rows, :], sin_ref[rows, :]
            lane = lax.broadcasted_iota(jnp.int32, (n_rows, HEAD_DIM), 1)
            swap_lo = (lane & (AXIS_ROT // 2)) == 0

        def finish_head(p, gain):
            y = _head_norm(p, gain)
            if rope:
                y = _rope(y, cos, sin, swap_lo)
            return y.astype(BF16)

        def project(c0, width):
            return _dot(h_ref[rows, :], w_ref[:, c0:c0 + width])

        def q_chunk(c0):
            p = project(c0, chunk)
            for hh in range(GQA_GROUP):
                lo = hh * HEAD_DIM
                q_ref[rows, c0 + lo:c0 + lo + HEAD_DIM] = finish_head(p[:, lo:lo + HEAD_DIM], q_gain)

        def kv_chunk():
            p = project(OFF_K, 2 * KV_W)
            for hh in range(N_KV_HEADS):
                lo = hh * HEAD_DIM
                k_ref[rows, lo:lo + HEAD_DIM] = finish_head(p[:, lo:lo + HEAD_DIM], k_gain)
            vt_ref[:, rows] = p[:, KV_W:2 * KV_W].T.astype(BF16)

        def up_chunk():
            up_ref[rows, :] = project(OFF_POOL, POOL_W)

        def uf_chunk():
            uf_ref[rows, :] = project(OFF_FOURIER, FOURIER_W)

        def gate_chunk(c0):
            g_ref[rows, c0:c0 + chunk] = project(OFF_GATE + c0, chunk)

        return ([functools.partial(q_chunk, c0) for c0 in range(0, ATTN_W, chunk)]
                + [kv_chunk, up_chunk, uf_chunk]
                + [functools.partial(gate_chunk, c0) for c0 in range(0, D_MODEL, chunk)])

    part_rows = tm // ROW_PARTS
    parts = [slice(r0, r0 + part_rows) for r0 in range(0, tm, part_rows)]
    for step in norm_steps(parts[0]):
        step()
    for index, rows in enumerate(parts):
        fill = norm_steps(parts[index + 1]) if index + 1 < len(parts) else []
        for step in matmul_steps(rows):
            step()
            if fill:
                fill.pop(0)()
        for step in fill:
            step()


def _vt_spec(tm, tiles_per_seq):
    return pl.BlockSpec((None, KV_W, tm), lambda i: (i // tiles_per_seq, 0, i % tiles_per_seq))


def _inproj(x2d, mod, norm_g, w_in, q_g, k_g, rope_tabs, l, *, tm, seq_len, per_batch_mod,
            casts=()):
    rows = x2d.shape[0]
    steps = rows // tm
    rope = rope_tabs is not None
    tiles_per_seq = seq_len // tm
    row_spec = lambda w: pl.BlockSpec((tm, w), lambda i: (i, 0))
    in_specs = [row_spec(D_MODEL), _layer((MOD_ROWS, 3 * D_MODEL), 0), _layer((1, D_MODEL), l),
                _layer((D_MODEL, IN_W), 0), _layer((1, HEAD_DIM), l), _layer((1, HEAD_DIM), l)]
    args = [x2d, mod, norm_g, w_in, q_g, k_g]
    if rope:
        tab_spec = pl.BlockSpec((tm, HEAD_DIM), lambda i: (i % tiles_per_seq, 0))
        in_specs += [tab_spec, tab_spec]
        args += list(rope_tabs)
    row_out = lambda w, dt: jax.ShapeDtypeStruct((rows, w), dt)
    out_specs = [row_spec(ATTN_W), row_spec(KV_W), _vt_spec(tm, tiles_per_seq),
                 row_spec(POOL_W), row_spec(FOURIER_W), row_spec(D_MODEL)]
    out_shape = [row_out(ATTN_W, BF16), row_out(KV_W, BF16),
                 jax.ShapeDtypeStruct((rows // seq_len, KV_W, seq_len), BF16),
                 row_out(POOL_W, F32), row_out(FOURIER_W, F32), row_out(D_MODEL, F32)]
    for param, layer in casts:
        _, p_rows, p_cols = param.shape
        slab = (None, p_rows // steps, p_cols)
        in_specs.append(pl.BlockSpec(slab, lambda i, layer=layer: (layer, i, 0)))
        out_specs.append(pl.BlockSpec(slab, lambda i: (0, i, 0)))
        out_shape.append(jax.ShapeDtypeStruct((1, p_rows, p_cols), BF16))
        args.append(param)
    outs = pl.pallas_call(
        functools.partial(_inproj_kernel, rope=rope, n_convert=len(casts),
                          tiles_per_batch=tiles_per_seq if per_batch_mod else None),
        grid=(steps,),
        in_specs=in_specs,
        out_specs=out_specs,
        out_shape=out_shape,
        scratch_shapes=[pltpu.VMEM((tm, D_MODEL), BF16)],
        compiler_params=_params(1),
        name="inproj_rope" if rope else "inproj_ctx",
    )(*args)
    return outs[:6], outs[6:]


def _ctx_kv_kernel(x_ref, mod_ref, ng_ref, w_ref, kg_ref, k_ref, vt_ref):
    h = _modulated_norm(x_ref, mod_ref, ng_ref, CTX_MOD_ROW)
    p = _dot(h, w_ref[...])
    k_gain = kg_ref[...]
    for hh in range(N_KV_HEADS):
        lo = hh * HEAD_DIM
        k_ref[:, lo:lo + HEAD_DIM] = _head_norm(p[:, lo:lo + HEAD_DIM], k_gain).astype(BF16)
    n_seqs, _, seq_len = vt_ref.shape
    for seq in range(n_seqs):
        vt_ref[seq] = p[seq * seq_len:(seq + 1) * seq_len, KV_W:2 * KV_W].T.astype(BF16)


def _ctx_kv(x2d, mod, norm_g, w_in, k_g, l, *, seqs_per_tile, seq_len):
    rows = x2d.shape[0]
    tm = seqs_per_tile * seq_len
    row_spec = lambda w: pl.BlockSpec((tm, w), lambda i: (i, 0))
    kv_cols = (0, OFF_K // (2 * KV_W))
    return pl.pallas_call(
        _ctx_kv_kernel,
        grid=(rows // tm,),
        in_specs=[row_spec(D_MODEL), _layer((MOD_ROWS, 3 * D_MODEL), 0), _layer((1, D_MODEL), l),
                  _layer((D_MODEL, 2 * KV_W), 0, kv_cols), _layer((1, HEAD_DIM), l)],
        out_specs=[row_spec(KV_W),
                   pl.BlockSpec((seqs_per_tile, KV_W, seq_len), lambda i: (i, 0, 0))],
        out_shape=[jax.ShapeDtypeStruct((rows, KV_W), BF16),
                   jax.ShapeDtypeStruct((rows // seq_len, KV_W, seq_len), BF16)],
        compiler_params=_params(1),
        name="ctx_kv",
    )(x2d, mod, norm_g, w_in, k_g)


def _attn_kernel(*refs, lengths, tq, with_adaln):
    n_sources = len(lengths)
    q_ref, g_ref = refs[0], refs[1]
    k_refs = refs[2:2 + 2 * n_sources:2]
    vt_refs = refs[3:3 + 2 * n_sources:2]
    n_in = 2 + 2 * n_sources
    o_ref, s_ref, p_ref, vta_ref = refs[-4:]
    if with_adaln:
        _adaln_kernel(*refs[n_in:n_in + 4])
    offsets = [sum(lengths[:i]) for i in range(n_sources)]
    total = sum(lengths)
    n_tiles = q_ref.shape[0] // tq
    key_chunks = [(k_ref, off, r0, min(KEY_CHUNK, length))
                  for k_ref, off, length in zip(k_refs, offsets, lengths)
                  for r0 in range(0, length, KEY_CHUNK)]
    head_cols = [slice(hh * HEAD_DIM, (hh + 1) * HEAD_DIM) for hh in range(GQA_GROUP)]

    def q_rows(tile):
        return pl.ds(pl.multiple_of(tile * tq, tq), tq)

    def fold_rows(x, op):
        return op(x.reshape(x.shape[0] // SUBLANES, SUBLANES, tq), axis=0)

    def score_chunk(tile, hh, slot, chunk, col_max):
        k_ref, off, r0, size = key_chunks[chunk]
        s = _dot_nt(k_ref[r0:r0 + size, :], q_ref[q_rows(tile), head_cols[hh]])
        s_ref[slot, off + r0:off + r0 + size, :] = s
        cm = fold_rows(s, jnp.max)
        return cm if col_max is None else jnp.maximum(col_max, cm)

    def prob_chunk(slot, chunk, col_max):
        _, off, start, size = key_chunks[chunk]
        for r0 in range(off + start, off + start + size, EXP_CHUNK):
            p = jnp.exp2(s_ref[slot, r0:r0 + EXP_CHUNK, :] - col_max)
            p_ref[slot, r0:r0 + EXP_CHUNK, :] = p.astype(BF16)

    def finish(tile, hh, slot):
        acc = _dot(vta_ref[...], p_ref[slot])
        out = (acc[:HEAD_DIM] / acc[HEAD_DIM:HEAD_DIM + 1]).T
        gate = _silu(g_ref[q_rows(tile), head_cols[hh]])
        o_ref[q_rows(tile), head_cols[hh]] = (out * gate).astype(BF16)

    for vt_ref, off, length in zip(vt_refs, offsets, lengths):
        vta_ref[0:HEAD_DIM, off:off + length] = vt_ref[...]
    vta_ref[HEAD_DIM:HEAD_DIM + ONES_ROWS, :] = jnp.ones((ONES_ROWS, total), BF16)

    n_chunks = len(key_chunks)
    first_max = None
    for chunk in range(n_chunks):
        first_max = score_chunk(0, 0, 0, chunk, first_max)
    first_max = jnp.max(first_max, axis=0, keepdims=True)

    def tile_body(tile, cur_max):
        for hh in range(GQA_GROUP):
            slot, next_slot = hh % 2, (hh + 1) % 2
            next_hh = (hh + 1) % GQA_GROUP
            next_tile = tile if next_hh else jnp.minimum(tile + 1, n_tiles - 1)
            next_max = None
            for chunk in range(n_chunks):
                next_max = score_chunk(next_tile, next_hh, next_slot, chunk, next_max)
                prob_chunk(slot, chunk, cur_max)
            finish(tile, hh, slot)
            cur_max = jnp.max(next_max, axis=0, keepdims=True)
        return cur_max

    lax.fori_loop(0, n_tiles, tile_body, first_max)


def _attention(q, g, sources, *, batch, q_len, tq, adaln_next=None):
    q_spec = pl.BlockSpec((q_len, GQA_GROUP * HEAD_DIM), lambda b, h: (b, h))
    in_specs = [q_spec, q_spec]
    args = [q, g]
    for k, vt, length in sources:
        in_specs += [pl.BlockSpec((length, HEAD_DIM), lambda b, h: (b, h)),
                     pl.BlockSpec((None, HEAD_DIM, length), lambda b, h: (b, h, 0))]
        args += [k, vt]
    out_specs = [q_spec]
    out_shape = [jax.ShapeDtypeStruct((batch * q_len, ATTN_W), BF16)]
    if adaln_next is not None:
        cond, ada_w, ada_b, layer = adaln_next
        slab = 3 * D_MODEL // (batch * N_KV_HEADS)
        side_in, side_out, side_shape = _adaln_specs(layer, slab, lambda b, h: b * N_KV_HEADS + h)
        in_specs += side_in
        args += [cond, ada_w, ada_b]
        out_specs = [side_out] + out_specs
        out_shape = [side_shape] + out_shape
    lengths = tuple(length for _, _, length in sources)
    total = sum(lengths)
    outs = pl.pallas_call(
        functools.partial(_attn_kernel, lengths=lengths, tq=tq, with_adaln=adaln_next is not None),
        grid=(batch, N_KV_HEADS),
        in_specs=in_specs,
        out_specs=out_specs,
        out_shape=out_shape,
        scratch_shapes=[pltpu.VMEM((2, total, tq), F32), pltpu.VMEM((2, total, tq), BF16),
                        pltpu.VMEM((HEAD_DIM + ONES_ROWS, total), BF16)],
        compiler_params=_params(2),
        name="attention",
    )(*args)
    return outs[-1], (outs[0] if adaln_next is not None else None)


SHORT_ATTN_SEQS = 2


def _short_attn_kernel(q_ref, g_ref, k_ref, vt_ref, o_ref):
    n_seqs, _, length = vt_ref.shape
    for seq in range(n_seqs):
        rows = slice(seq * length, (seq + 1) * length)
        for kvh in range(N_KV_HEADS):
            kv_cols = slice(kvh * HEAD_DIM, (kvh + 1) * HEAD_DIM)
            heads = [slice((kvh * GQA_GROUP + hh) * HEAD_DIM, (kvh * GQA_GROUP + hh + 1) * HEAD_DIM)
                     for hh in range(GQA_GROUP)]
            q_rows = jnp.concatenate([q_ref[rows, cols] for cols in heads], axis=0)
            s = _dot_nt(k_ref[rows, kv_cols], q_rows)
            p = jnp.exp2(s - jnp.max(s, axis=0, keepdims=True)).astype(BF16)
            vt_aug = jnp.concatenate([vt_ref[seq, kv_cols, :],
                                      jnp.ones((ONES_ROWS, length), BF16)], axis=0)
            acc = _dot(vt_aug, p)
            for hh, cols in enumerate(heads):
                span = slice(hh * length, (hh + 1) * length)
                out = (acc[:HEAD_DIM, span] / acc[HEAD_DIM:HEAD_DIM + 1, span]).T
                o_ref[rows, cols] = (out * _silu(g_ref[rows, cols])).astype(BF16)


def _short_attention(q, g, k, vt, *, batch, length):
    seqs = SHORT_ATTN_SEQS
    row_spec = lambda w: pl.BlockSpec((seqs * length, w), lambda i: (i, 0))
    return pl.pallas_call(
        _short_attn_kernel,
        grid=(batch // seqs,),
        in_specs=[row_spec(ATTN_W), row_spec(ATTN_W), row_spec(KV_W),
                  pl.BlockSpec((seqs, KV_W, length), lambda i: (i, 0, 0))],
        out_specs=row_spec(ATTN_W),
        out_shape=jax.ShapeDtypeStruct((batch * length, ATTN_W), BF16),
        compiler_params=_params(1),
        name="short_attention",
    )(q, g, k, vt)


POOL_CHUNK = 256
OUT_CHUNK = 256


def _pool_steps(up_ref, prev_ref, next_ref, g_ref, pw_ref, ps_ref, dst_ref,
                pad_ref, s2_ref, s4_ref, s8_ref, *, tile_in_seq, tiles_per_seq, seq_len):
    tm = up_ref.shape[0]
    halo = POOL_HALO
    chunk = min(tm, POOL_CHUNK)

    def forward_sum(dst, src, shift, length, lane0):
        for r0 in range(0, length, chunk):
            rows = min(chunk, length - r0)
            dst[r0:r0 + rows, :] = (src[r0:r0 + rows, lane0:]
                                    + src[r0 + shift:r0 + shift + rows, lane0:])

    def window_sums():
        pad_ref[0:halo, :] = jnp.where(tile_in_seq > 0, prev_ref[...], 0.0)
        pad_ref[halo:halo + tm, :] = up_ref[...]
        pad_ref[halo + tm:2 * halo + tm, :] = jnp.where(tile_in_seq < tiles_per_seq - 1,
                                                        next_ref[...], 0.0)
        pad_ref[2 * halo + tm:4 * halo + tm, :] = jnp.zeros((2 * halo, POOL_W), F32)
        forward_sum(s2_ref, pad_ref, 1, tm + 3 * halo, 0)
        forward_sum(s4_ref, s2_ref, 2, tm + 2 * halo, GROUP_C)
        forward_sum(s8_ref, s4_ref, 4, tm + halo, GROUP_C)

    def window_sum(gi, r0):
        if gi == 0:
            return s2_ref[halo - 1 + r0:halo - 1 + r0 + chunk, 0:GROUP_C]
        if gi == 1:
            return s4_ref[halo - 2 + r0:halo - 2 + r0 + chunk, 0:GROUP_C]
        if gi == 2:
            return s8_ref[halo - 4 + r0:halo - 4 + r0 + chunk, 0:GROUP_C]
        return (s8_ref[r0:r0 + chunk, GROUP_C:] + s8_ref[halo + r0:halo + r0 + chunk, GROUP_C:])

    def edge_inverse_count(win, local_row):
        before, after = win // 2, win - win // 2 - 1
        t = tile_in_seq * tm + local_row + lax.broadcasted_iota(jnp.int32, (halo, GROUP_C), 0)
        cnt = jnp.minimum(t + after, seq_len - 1) - jnp.maximum(t - before, 0) + 1
        return 1.0 / cnt.astype(F32)

    def pooled_block(gi, win, r0):
        cols = slice(gi * GROUP_C, (gi + 1) * GROUP_C)
        pieces, inner = [], chunk
        if r0 == 0:
            pieces.append(edge_inverse_count(win, 0))
            inner -= halo
        tail = [edge_inverse_count(win, tm - halo)] if r0 + chunk == tm else []
        inner -= halo * len(tail)
        pieces.append(jnp.full((inner, GROUP_C), 1.0 / win, F32))
        inv_cnt = jnp.concatenate(pieces + tail, axis=0)
        pooled = window_sum(gi, r0) * inv_cnt - pad_ref[halo + r0:halo + r0 + chunk, cols]
        y = _dot(pooled.astype(BF16), pw_ref[gi].astype(BF16)) * ps_ref[:, cols]
        dst_ref[r0:r0 + chunk, cols] = (y * _silu(g_ref[r0:r0 + chunk, cols])).astype(BF16)

    return [window_sums] + [functools.partial(pooled_block, gi, win, r0)
                            for gi, win in enumerate(POOL_WINDOWS)
                            for r0 in range(0, tm, chunk)]


def _pool_scratch(tm):
    return [pltpu.VMEM((tm + 4 * POOL_HALO, POOL_W), F32),
            pltpu.VMEM((tm + 3 * POOL_HALO, POOL_W), F32),
            pltpu.VMEM((tm + 2 * POOL_HALO, POOL_W - GROUP_C), F32),
            pltpu.VMEM((tm + POOL_HALO, POOL_W - 2 * GROUP_C), F32),
            pltpu.VMEM((tm, POOL_W), BF16)]


def _fourier_kernel(u_ref, g_ref, fw_ref, cc_ref, sc_ref, cn_ref, sn_ref, o_ref, ua_ref, ub_ref,
                    *, n, chunk):
    for gi in range(MIX_GROUPS):
        cols = slice(gi * GROUP_C, (gi + 1) * GROUP_C)
        w = fw_ref[gi].astype(BF16)
        a = _dot(cc_ref[...], w).astype(BF16)
        b = _dot(sc_ref[...], w).astype(BF16)
        ug = u_ref[:, cols].astype(BF16)
        ua_ref[:, cols] = _dot(ug, a).astype(BF16)
        ub_ref[:, cols] = _dot(ug, b).astype(BF16)
    for r0 in range(0, n, chunk):
        rows = slice(r0, r0 + chunk)
        y = _dot(cn_ref[rows, :], ua_ref[...]) - _dot(sn_ref[rows, :], ub_ref[...])
        o_ref[rows, :] = (y * _silu(g_ref[rows, :])).astype(BF16)


def _fourier(uf, g, fourier_w, tabs, l, *, batch, n):
    cc, sc, cn, sn = tabs
    seq_spec = lambda col: pl.BlockSpec((n, FOURIER_W), lambda b: (b, col))
    return pl.pallas_call(
        functools.partial(_fourier_kernel, n=n, chunk=min(n, 512)),
        grid=(batch,),
        in_specs=[seq_spec(0), seq_spec((ATTN_W + POOL_W) // FOURIER_W),
                  _layer((MIX_GROUPS, GROUP_C, GROUP_C), l),
                  _resident((GROUP_C, GROUP_C)), _resident((GROUP_C, GROUP_C)),
                  _resident((n, n)), _resident((n, n))],
        out_specs=seq_spec(0),
        out_shape=jax.ShapeDtypeStruct((batch * n, FOURIER_W), BF16),
        scratch_shapes=[pltpu.VMEM((n, FOURIER_W), BF16), pltpu.VMEM((n, FOURIER_W), BF16)],
        compiler_params=_params(1),
        name="fourier",
    )(uf, g, fourier_w, cc, sc, cn, sn)


RADIX = 8
COMBINE_ROWS = 16
SQRT_HALF = math.sqrt(0.5)


def _radix8_real(yr, yn):
    sr = [yr[n] + yr[n + 4] for n in range(4)]
    dr = [yr[n] - yr[n + 4] for n in range(4)]
    sn = {n: yn[n] + yn[n + 4] for n in (1, 3)}
    dn = {n: yn[n] - yn[n + 4] for n in (1, 2, 3)}
    t1r, t1i = (dr[1] - dn[1]) * SQRT_HALF, (dr[1] + dn[1]) * -SQRT_HALF
    t3r, t3i = (dr[3] + dn[3]) * -SQRT_HALF, (dn[3] - dr[3]) * SQRT_HALF
    out = [None] * RADIX
    ea, eb = sr[0] + sr[2], sr[1] + sr[3]
    ec, ed = sr[0] - sr[2], sn[3] - sn[1]
    out[0], out[4] = ea + eb, ea - eb
    out[2], out[6] = ec + ed, ec - ed
    oa, ob = dr[0] - dn[2], t1r + t3r
    oc, od = dr[0] + dn[2], t1i - t3i
    out[1], out[5] = oa + ob, oa - ob
    out[3], out[7] = oc + od, oc - od
    return out


def _fourier_radix_kernel(*refs, n1):
    u_ref, g_ref, fw_ref, cc_ref, sc_ref, cs_ref, o_ref, us_ref, ub_ref = refs[:9]
    scratch = refs[9:]
    pq_refs, yr_refs, yn_refs = (scratch[i * MIX_GROUPS:(i + 1) * MIX_GROUPS] for i in range(3))
    group_lanes = [slice(gi * GROUP_C, (gi + 1) * GROUP_C) for gi in range(MIX_GROUPS)]

    for gi, lanes in enumerate(group_lanes):
        us_ref[gi] = u_ref[:, lanes]

    for n2 in range(RADIX):
        for gi, lanes in enumerate(group_lanes):
            ub_ref[n2, :, lanes] = us_ref[gi, pl.ds(n2, n1, stride=RADIX), :].astype(BF16)
        pq = _dot(cs_ref[n2], ub_ref[n2])
        rows = slice(n2 * n1, (n2 + 1) * n1)
        for gi, lanes in enumerate(group_lanes):
            pq_refs[gi][rows, :GROUP_C] = pq[:n1, lanes].astype(BF16)
            pq_refs[gi][rows, GROUP_C:] = pq[n1:, lanes].astype(BF16)

    for gi, lanes in enumerate(group_lanes):
        w = fw_ref[gi].astype(BF16)
        a, b = _dot(cc_ref[...], w), _dot(sc_ref[...], w)
        mix = jnp.concatenate([jnp.concatenate([a, b], axis=1),
                               jnp.concatenate([-b, a], axis=1)], axis=0).astype(BF16)
        y = _dot(pq_refs[gi][...], mix)
        for n2 in range(RADIX):
            yr_refs[gi][n2] = y[n2 * n1:(n2 + 1) * n1, :GROUP_C]
            yn_refs[gi][n2] = y[n2 * n1:(n2 + 1) * n1, GROUP_C:]

        for r0 in range(0, n1, COMBINE_ROWS):
            rows = slice(r0, r0 + COMBINE_ROWS)
            yr = [yr_refs[gi][n2, rows, :] for n2 in range(RADIX)]
            yn = [yn_refs[gi][n2, rows, :] for n2 in range(RADIX)]
            for k2, val in enumerate(_radix8_real(yr, yn)):
                out_rows = slice(k2 * n1 + r0, k2 * n1 + r0 + COMBINE_ROWS)
                o_ref[out_rows, lanes] = (val * _silu(g_ref[out_rows, lanes])).astype(BF16)


def _fourier_radix(uf, g, fourier_w, tabs, l, *, batch, n):
    cc, sc, cs = tabs
    n1 = n // RADIX
    seq_spec = lambda col: pl.BlockSpec((n, FOURIER_W), lambda b: (b, col))
    return pl.pallas_call(
        functools.partial(_fourier_radix_kernel, n1=n1),
        grid=(batch,),
        in_specs=[seq_spec(0), seq_spec((ATTN_W + POOL_W) // FOURIER_W),
                  _layer((MIX_GROUPS, GROUP_C, GROUP_C), l),
                  _resident((GROUP_C, GROUP_C)), _resident((GROUP_C, GROUP_C)),
                  _resident((RADIX, 2 * n1, n1))],
        out_specs=seq_spec(0),
        out_shape=jax.ShapeDtypeStruct((batch * n, FOURIER_W), BF16),
        scratch_shapes=([pltpu.VMEM((MIX_GROUPS, n, GROUP_C), F32),
                         pltpu.VMEM((RADIX, n1, FOURIER_W), BF16)]
                        + [pltpu.VMEM((n, 2 * GROUP_C), BF16)] * MIX_GROUPS
                        + [pltpu.VMEM((RADIX, n1, GROUP_C), F32)] * (2 * MIX_GROUPS)),
        compiler_params=_params(1),
        name="fourier_radix",
    )(uf, g, fourier_w, cc, sc, cs)


def _outproj_kernel(*refs, tiles_per_seq, seq_len, per_batch_mod, final):
    (att_ref, up_ref, prev_ref, next_ref, gp_ref, four_ref, w_ref, pw_ref, ps_ref,
     x_ref, mod_ref) = refs[:11]
    fg_ref = refs[11] if final else None
    o_ref, pad_ref, s2_ref, s4_ref, s8_ref, pooled_ref = refs[-6:]
    step = pl.program_id(0)
    pool_steps = _pool_steps(up_ref, prev_ref, next_ref, gp_ref, pw_ref, ps_ref, pooled_ref,
                             pad_ref, s2_ref, s4_ref, s8_ref, tile_in_seq=step % tiles_per_seq,
                             tiles_per_seq=tiles_per_seq, seq_len=seq_len)
    col_chunks = [slice(c0, c0 + OUT_CHUNK) for c0 in range(0, D_MODEL, OUT_CHUNK)]

    pool_steps.pop(0)()
    for cols in col_chunks:
        o_ref[:, cols] = (_dot(att_ref[...], w_ref[0:ATTN_W, cols])
                          + _dot(four_ref[...], w_ref[ATTN_W + POOL_W:D_MODEL, cols]))
        if pool_steps:
            pool_steps.pop(0)()
    for pool_step in pool_steps:
        pool_step()

    row = step // tiles_per_seq if per_batch_mod else CTX_MOD_ROW
    sum_sq = None
    for cols in col_chunks:
        gate = mod_ref[pl.ds(row, 1), 2 * D_MODEL + cols.start:2 * D_MODEL + cols.stop]
        out = o_ref[:, cols] + _dot(pooled_ref[...], w_ref[ATTN_W:ATTN_W + POOL_W, cols])
        y = x_ref[:, cols] + gate * out
        o_ref[:, cols] = y
        if final:
            part = jnp.sum(y * y, axis=-1, keepdims=True)
            sum_sq = part if sum_sq is None else sum_sq + part
    if final:
        inv_rms = lax.rsqrt(sum_sq * (1.0 / D_MODEL) + EPS)
        o_ref[...] = (o_ref[...] * inv_rms) * fg_ref[...]


def _outproj(att, up, g, four, w_out, pool_w, pool_scale, x2d, mod, final_g, l,
             *, tm, seq_len, per_batch_mod):
    rows = x2d.shape[0]
    final = final_g is not None
    tiles_per_seq = seq_len // tm
    halo_blocks = tm // POOL_HALO
    row_spec = lambda w: pl.BlockSpec((tm, w), lambda i: (i, 0))
    halo_spec = lambda index: pl.BlockSpec((POOL_HALO, POOL_W), lambda i: (index(i), 0))
    in_specs = [row_spec(ATTN_W), row_spec(POOL_W),
                halo_spec(lambda i: jnp.maximum(i * halo_blocks - 1, 0)),
                halo_spec(lambda i: jnp.minimum((i + 1) * halo_blocks, rows // POOL_HALO - 1)),
                pl.BlockSpec((tm, POOL_W), lambda i: (i, ATTN_W // POOL_W)), row_spec(FOURIER_W),
                _layer((D_MODEL, D_MODEL), 0), _layer((MIX_GROUPS, GROUP_C, GROUP_C), l),
                _layer((1, POOL_W), l), row_spec(D_MODEL), _layer((MOD_ROWS, 3 * D_MODEL), 0)]
    args = [att, up, up, up, g, four, w_out, pool_w, pool_scale, x2d, mod]
    if final:
        in_specs.append(_resident((1, D_MODEL)))
        args.append(final_g)
    return pl.pallas_call(
        functools.partial(_outproj_kernel, tiles_per_seq=tiles_per_seq, seq_len=seq_len,
                          per_batch_mod=per_batch_mod, final=final),
        grid=(rows // tm,),
        in_specs=in_specs,
        out_specs=row_spec(D_MODEL),
        out_shape=jax.ShapeDtypeStruct((rows, D_MODEL), F32),
        scratch_shapes=_pool_scratch(tm),
        compiler_params=_params(1),
        name="outproj_final" if final else "outproj",
    )(*args)


def _rope_tables(n):
    grid_h = n // GRID_W
    inv = ROPE_THETA ** (-jnp.arange(0, AXIS_ROT, 2, dtype=F32) / AXIS_ROT)
    ang_r = jnp.arange(grid_h, dtype=F32)[:, None] * inv
    ang_c = jnp.arange(GRID_W, dtype=F32)[:, None] * inv
    by_row = lambda t: jnp.repeat(t, GRID_W, axis=0)
    by_col = lambda t: jnp.tile(t, (grid_h, 1))
    cos_r, sin_r = by_row(jnp.cos(ang_r)), by_row(jnp.sin(ang_r))
    cos_c, sin_c = by_col(jnp.cos(ang_c)), by_col(jnp.sin(ang_c))
    cos = jnp.concatenate([cos_r, cos_r, cos_c, cos_c], axis=1)
    sin = jnp.concatenate([-sin_r, sin_r, -sin_c, sin_c], axis=1)
    return cos, sin


def _dft_cos_sin(n):
    k = jnp.arange(n, dtype=jnp.int32)
    ang = ((k[:, None] * k[None, :]) % n).astype(F32) * (2.0 * math.pi / n)
    return jnp.cos(ang), jnp.sin(ang)


def _fourier_tables(n):
    cc, sc = _dft_cos_sin(GROUP_C)
    cn, sn = _dft_cos_sin(n)
    norm = 1.0 / math.sqrt(n * GROUP_C)
    return (cc * norm).astype(BF16), (sc * norm).astype(BF16), cn.astype(BF16), sn.astype(BF16)


def _fourier_radix_tables(n):
    n1 = n // RADIX
    cc, sc = _dft_cos_sin(GROUP_C)
    norm = 1.0 / math.sqrt(n * GROUP_C)
    c1, s1 = _dft_cos_sin(n1)
    k1 = jnp.arange(n1, dtype=jnp.int32)[None, :]
    n2 = jnp.arange(RADIX, dtype=jnp.int32)[:, None]
    twiddle = (k1 * n2).astype(F32) * (2.0 * math.pi / n)
    ct, st = jnp.cos(twiddle)[:, :, None], jnp.sin(twiddle)[:, :, None]
    cs = jnp.concatenate([c1[None] * ct - s1[None] * st, s1[None] * ct + c1[None] * st], axis=1)
    return (cc * norm).astype(BF16), (sc * norm).astype(BF16), cs.astype(BF16)


def kernel(x, c, ctx, c_ctx, ada_w, ada_b, norm_g, w_in, q_norm_g, k_norm_g, pool_w, pool_scale,
           fourier_w, w_out, final_norm_g):
    B, N, _ = x.shape
    C = ctx.shape[1]
    tm = 512
    tm_ctx = 256

    cond = jnp.concatenate([c, c_ctx[None, :], jnp.zeros((MOD_ROWS - B - 1, D_MODEL), F32)], axis=0)
    ada_b3 = ada_b.reshape(DEPTH, 1, 3 * D_MODEL)
    mod = _adaln(cond, ada_w, ada_b3, 0)

    rope_tabs = _rope_tables(N)
    four_tabs = _fourier_radix_tables(N)
    four_tabs_ctx = _fourier_tables(C)
    w_in_b = w_in[0:1].astype(BF16)
    w_out_b = None
    ng = norm_g[:, None, :]
    qg = q_norm_g[:, None, :]
    kg = k_norm_g[:, None, :]
    ps = pool_scale[:, None, :]

    xl = x.reshape(B * N, D_MODEL)
    xc = ctx.reshape(B * C, D_MODEL)
    for l in range(DEPTH):
        last = l == DEPTH - 1
        if last:
            kc, vct = _ctx_kv(xc, mod, ng, w_in_b, kg, l, seqs_per_tile=2, seq_len=C)
        else:
            (qc, kc, vct, upc, ufc, gc), _ = _inproj(xc, mod, ng, w_in_b, qg, kg, None, l,
                                                     tm=tm_ctx, seq_len=C, per_batch_mod=False)
        casts = [] if w_out_b is not None else [(w_out, l)]
        casts += [] if last else [(w_in, l + 1), (w_out, l + 1)]
        (q, k, vt, up, uf, g), cast_weights = _inproj(
            xl, mod, ng, w_in_b, qg, kg, rope_tabs, l, tm=tm, seq_len=N, per_batch_mod=True,
            casts=casts)
        if w_out_b is None:
            w_out_b, cast_weights = cast_weights[0], cast_weights[1:]

        att, next_mod = _attention(q, g, [(kc, vct, C), (k, vt, N)], batch=B, q_len=N, tq=256,
                                   adaln_next=None if last else (cond, ada_w, ada_b3, l + 1))
        four = _fourier_radix(uf, g, fourier_w, four_tabs, l, batch=B, n=N)
        xl_new = _outproj(att, up, g, four, w_out_b, pool_w, ps, xl, mod,
                          final_norm_g[None, :] if last else None, l,
                          tm=tm, seq_len=N, per_batch_mod=True)

        if not last:
            attc = _short_attention(qc, gc, kc, vct, batch=B, length=C)
            fourc = _fourier(ufc, gc, fourier_w, four_tabs_ctx, l, batch=B, n=C)
            xc = _outproj(attc, upc, gc, fourc, w_out_b, pool_w, ps, xc, mod, None, l,
                          tm=tm_ctx, seq_len=C, per_batch_mod=False)
            w_in_b, w_out_b = cast_weights
            mod = next_mod
        xl = xl_new
    return xl.reshape(B, N, D_MODEL)
```

```python
import functools
import math

import jax
import jax.numpy as jnp
from jax import lax
from jax.experimental import pallas as pl
from jax.experimental.pallas import tpu as pltpu

D_MODEL = 2048
DEPTH = 2
GRID_W = 64
HEAD_DIM = 128
ATTN_W = 1024
N_HEADS = 8
N_KV_HEADS = 2
GQA_GROUP = 4
KV_W = 256
POOL_W = 512
POOL_WINDOWS = (2, 4, 8, 16)
FOURIER_W = 512
MIX_GROUPS = 4
GROUP_C = 128
OFF_K = ATTN_W
OFF_V = OFF_K + KV_W
OFF_POOL = OFF_V + KV_W
OFF_FOURIER = OFF_POOL + POOL_W
OFF_GATE = OFF_FOURIER + FOURIER_W
IN_W = OFF_GATE + D_MODEL
ROPE_THETA = 10000.0
AXIS_ROT = HEAD_DIM // 2
EPS = 1e-6

MOD_ROWS = 16
CTX_MOD_ROW = 8
SUBLANES = 8
ROW_PARTS = 2
NORM_PIECES = 8
POOL_HALO = 8
KEY_CHUNK = 256
EXP_CHUNK = 256
ONES_ROWS = 16
V7X_VMEM_BYTES = 64 * 1024 * 1024
VMEM_LIMIT = V7X_VMEM_BYTES - 8 * 1024 * 1024

BF16 = jnp.bfloat16
F32 = jnp.float32


def _silu(x):
    return x * jax.nn.sigmoid(x)


def _dot(a, b):
    return jnp.dot(a, b, preferred_element_type=F32)


def _dot_nt(a, b):
    return lax.dot_general(a, b, (((1,), (1,)), ((), ())), preferred_element_type=F32)


def _resident(shape):
    zeros = (0,) * len(shape)
    return pl.BlockSpec(shape, lambda *_: zeros, pipeline_mode=pl.Buffered(1))


def _layer(shape, l, tail=None):
    index = (l,) + (tail or (0,) * len(shape))
    return pl.BlockSpec((None,) + tuple(shape), lambda *_: index, pipeline_mode=pl.Buffered(1))


def _params(n_grid):
    return pltpu.CompilerParams(dimension_semantics=("arbitrary",) * n_grid,
                                vmem_limit_bytes=VMEM_LIMIT)


def _adaln_kernel(c_ref, w_ref, b_ref, o_ref):
    a = _silu(c_ref[...]).astype(BF16)
    o_ref[...] = _dot(a, w_ref[...].astype(BF16)) + b_ref[...]


def _adaln_specs(l, tn, step):
    in_specs = [pl.BlockSpec((MOD_ROWS, D_MODEL), lambda *i: (0, 0)),
                pl.BlockSpec((None, D_MODEL, tn), lambda *i: (l, 0, step(*i))),
                pl.BlockSpec((None, 1, tn), lambda *i: (l, 0, step(*i)))]
    out_spec = pl.BlockSpec((None, MOD_ROWS, tn), lambda *i: (0, 0, step(*i)))
    return in_specs, out_spec, jax.ShapeDtypeStruct((1, MOD_ROWS, 3 * D_MODEL), F32)


def _adaln(cond, ada_w, ada_b, l):
    tn = 1024
    in_specs, out_spec, out_shape = _adaln_specs(l, tn, lambda j: j)
    return pl.pallas_call(
        _adaln_kernel,
        grid=(3 * D_MODEL // tn,),
        in_specs=in_specs,
        out_specs=out_spec,
        out_shape=out_shape,
        compiler_params=_params(1),
        name="adaln",
    )(cond, ada_w, ada_b)


def _modulated_norm(x_ref, mod_ref, ng_ref, row, rows=slice(None)):
    x = x_ref[rows, :]
    ms = jnp.mean(x * x, axis=-1, keepdims=True)
    shift = mod_ref[pl.ds(row, 1), 0:D_MODEL]
    scale = mod_ref[pl.ds(row, 1), D_MODEL:2 * D_MODEL]
    y = (x * lax.rsqrt(ms + EPS)) * ng_ref[...]
    return (y * (1.0 + scale) + shift).astype(BF16)


def _head_norm(p, gain):
    ms = jnp.mean(p * p, axis=-1, keepdims=True)
    return (p * lax.rsqrt(ms + EPS)) * gain


def _rope(y, cos, sin, swap_lo):
    partner = jnp.where(swap_lo, pltpu.roll(y, 96, 1), pltpu.roll(y, 32, 1))
    return y * cos + partner * sin


def _inproj_kernel(*refs, rope, tiles_per_batch, n_convert):
    n_in = 8 if rope else 6
    x_ref, mod_ref, ng_ref, w_ref, qg_ref, kg_ref = refs[:6]
    cos_ref, sin_ref = refs[6:8] if rope else (None, None)
    f32_weight_refs = refs[n_in:n_in + n_convert]
    q_ref, k_ref, vt_ref, up_ref, uf_ref, g_ref = refs[n_in + n_convert:n_in + n_convert + 6]
    bf16_weight_refs = refs[n_in + n_convert + 6:n_in + 2 * n_convert + 6]
    h_ref = refs[-1]
    for src_ref, dst_ref in zip(f32_weight_refs, bf16_weight_refs):
        dst_ref[...] = src_ref[...].astype(BF16)
    tm = x_ref.shape[0]
    row = CTX_MOD_ROW if tiles_per_batch is None else pl.program_id(0) // tiles_per_batch
    q_gain = qg_ref[...] * (HEAD_DIM ** -0.5 * math.log2(math.e))
    k_gain = kg_ref[...]
    chunk = GQA_GROUP * HEAD_DIM

    def norm_steps(rows):
        piece_rows = (rows.stop - rows.start) // NORM_PIECES
        pieces = [slice(r0, r0 + piece_rows) for r0 in range(rows.start, rows.stop, piece_rows)]

        def norm(piece):
            h_ref[piece, :] = _modulated_norm(x_ref, mod_ref, ng_ref, row, piece)

        return [functools.partial(norm, piece) for piece in pieces]

    def matmul_steps(rows):
        n_rows = rows.stop - rows.start
        if rope:
            cos, sin = cos_ref[rows, :], sin_ref[rows, :]
            lane = lax.broadcasted_iota(jnp.int32, (n_rows, HEAD_DIM), 1)
            swap_lo = (lane & (AXIS_ROT // 2)) == 0

        def finish_head(p, gain):
            y = _head_norm(p, gain)
            if rope:
                y = _rope(y, cos, sin, swap_lo)
            return y.astype(BF16)

        def project(c0, width):
            return _dot(h_ref[rows, :], w_ref[:, c0:c0 + width])

        def q_chunk(c0):
            p = project(c0, chunk)
            for hh in range(GQA_GROUP):
                lo = hh * HEAD_DIM
                q_ref[rows, c0 + lo:c0 + lo + HEAD_DIM] = finish_head(p[:, lo:lo + HEAD_DIM], q_gain)

        def kv_chunk():
            p = project(OFF_K, 2 * KV_W)
            for hh in range(N_KV_HEADS):
                lo = hh * HEAD_DIM
                k_ref[rows, lo:lo + HEAD_DIM] = finish_head(p[:, lo:lo + HEAD_DIM], k_gain)
            vt_ref[:, rows] = p[:, KV_W:2 * KV_W].T.astype(BF16)

        def up_chunk():
            up_ref[rows, :] = project(OFF_POOL, POOL_W)

        def uf_chunk():
            uf_ref[rows, :] = project(OFF_FOURIER, FOURIER_W)

        def gate_chunk(c0):
            g_ref[rows, c0:c0 + chunk] = project(OFF_GATE + c0, chunk)

        return ([functools.partial(q_chunk, c0) for c0 in range(0, ATTN_W, chunk)]
                + [kv_chunk, up_chunk, uf_chunk]
                + [functools.partial(gate_chunk, c0) for c0 in range(0, D_MODEL, chunk)])

    part_rows = tm // ROW_PARTS
    parts = [slice(r0, r0 + part_rows) for r0 in range(0, tm, part_rows)]
    for step in norm_steps(parts[0]):
        step()
    for index, rows in enumerate(parts):
        fill = norm_steps(parts[index + 1]) if index + 1 < len(parts) else []
        for step in matmul_steps(rows):
            step()
            if fill:
                fill.pop(0)()
        for step in fill:
            step()


def _vt_spec(tm, tiles_per_seq):
    return pl.BlockSpec((None, KV_W, tm), lambda i: (i // tiles_per_seq, 0, i % tiles_per_seq))


def _inproj(x2d, mod, norm_g, w_in, q_g, k_g, rope_tabs, l, *, tm, seq_len, per_batch_mod,
            casts=()):
    rows = x2d.shape[0]
    steps = rows // tm
    rope = rope_tabs is not None
    tiles_per_seq = seq_len // tm
    row_spec = lambda w: pl.BlockSpec((tm, w), lambda i: (i, 0))
    in_specs = [row_spec(D_MODEL), _layer((MOD_ROWS, 3 * D_MODEL), 0), _layer((1, D_MODEL), l),
                _layer((D_MODEL, IN_W), 0), _layer((1, HEAD_DIM), l), _layer((1, HEAD_DIM), l)]
    args = [x2d, mod, norm_g, w_in, q_g, k_g]
    if rope:
        tab_spec = pl.BlockSpec((tm, HEAD_DIM), lambda i: (i % tiles_per_seq, 0))
        in_specs += [tab_spec, tab_spec]
        args += list(rope_tabs)
    row_out = lambda w, dt: jax.ShapeDtypeStruct((rows, w), dt)
    out_specs = [row_spec(ATTN_W), row_spec(KV_W), _vt_spec(tm, tiles_per_seq),
                 row_spec(POOL_W), row_spec(FOURIER_W), row_spec(D_MODEL)]
    out_shape = [row_out(ATTN_W, BF16), row_out(KV_W, BF16),
                 jax.ShapeDtypeStruct((rows // seq_len, KV_W, seq_len), BF16),
                 row_out(POOL_W, F32), row_out(FOURIER_W, F32), row_out(D_MODEL, F32)]
    for param, layer in casts:
        _, p_rows, p_cols = param.shape
        slab = (None, p_rows // steps, p_cols)
        in_specs.append(pl.BlockSpec(slab, lambda i, layer=layer: (layer, i, 0)))
        out_specs.append(pl.BlockSpec(slab, lambda i: (0, i, 0)))
        out_shape.append(jax.ShapeDtypeStruct((1, p_rows, p_cols), BF16))
        args.append(param)
    outs = pl.pallas_call(
        functools.partial(_inproj_kernel, rope=rope, n_convert=len(casts),
                          tiles_per_batch=tiles_per_seq if per_batch_mod else None),
        grid=(steps,),
        in_specs=in_specs,
        out_specs=out_specs,
        out_shape=out_shape,
        scratch_shapes=[pltpu.VMEM((tm, D_MODEL), BF16)],
        compiler_params=_params(1),
        name="inproj_rope" if rope else "inproj_ctx",
    )(*args)
    return outs[:6], outs[6:]


def _ctx_kv_kernel(x_ref, mod_ref, ng_ref, w_ref, kg_ref, k_ref, vt_ref):
    h = _modulated_norm(x_ref, mod_ref, ng_ref, CTX_MOD_ROW)
    p = _dot(h, w_ref[...])
    k_gain = kg_ref[...]
    for hh in range(N_KV_HEADS):
        lo = hh * HEAD_DIM
        k_ref[:, lo:lo + HEAD_DIM] = _head_norm(p[:, lo:lo + HEAD_DIM], k_gain).astype(BF16)
    n_seqs, _, seq_len = vt_ref.shape
    for seq in range(n_seqs):
        vt_ref[seq] = p[seq * seq_len:(seq + 1) * seq_len, KV_W:2 * KV_W].T.astype(BF16)


def _ctx_kv(x2d, mod, norm_g, w_in, k_g, l, *, seqs_per_tile, seq_len):
    rows = x2d.shape[0]
    tm = seqs_per_tile * seq_len
    row_spec = lambda w: pl.BlockSpec((tm, w), lambda i: (i, 0))
    kv_cols = (0, OFF_K // (2 * KV_W))
    return pl.pallas_call(
        _ctx_kv_kernel,
        grid=(rows // tm,),
        in_specs=[row_spec(D_MODEL), _layer((MOD_ROWS, 3 * D_MODEL), 0), _layer((1, D_MODEL), l),
                  _layer((D_MODEL, 2 * KV_W), 0, kv_cols), _layer((1, HEAD_DIM), l)],
        out_specs=[row_spec(KV_W),
                   pl.BlockSpec((seqs_per_tile, KV_W, seq_len), lambda i: (i, 0, 0))],
        out_shape=[jax.ShapeDtypeStruct((rows, KV_W), BF16),
                   jax.ShapeDtypeStruct((rows // seq_len, KV_W, seq_len), BF16)],
        compiler_params=_params(1),
        name="ctx_kv",
    )(x2d, mod, norm_g, w_in, k_g)


def _attn_kernel(*refs, lengths, tq, with_adaln):
    n_sources = len(lengths)
    q_ref, g_ref = refs[0], refs[1]
    k_refs = refs[2:2 + 2 * n_sources:2]
    vt_refs = refs[3:3 + 2 * n_sources:2]
    n_in = 2 + 2 * n_sources
    o_ref, s_ref, p_ref, vta_ref = refs[-4:]
    if with_adaln:
        _adaln_kernel(*refs[n_in:n_in + 4])
    offsets = [sum(lengths[:i]) for i in range(n_sources)]
    total = sum(lengths)
    n_tiles = q_ref.shape[0] // tq
    key_chunks = [(k_ref, off, r0, min(KEY_CHUNK, length))
                  for k_ref, off, length in zip(k_refs, offsets, lengths)
                  for r0 in range(0, length, KEY_CHUNK)]
    head_cols = [slice(hh * HEAD_DIM, (hh + 1) * HEAD_DIM) for hh in range(GQA_GROUP)]

    def q_rows(tile):
        return pl.ds(pl.multiple_of(tile * tq, tq), tq)

    def fold_rows(x, op):
        return op(x.reshape(x.shape[0] // SUBLANES, SUBLANES, tq), axis=0)

    def score_chunk(tile, hh, slot, chunk, col_max):
        k_ref, off, r0, size = key_chunks[chunk]
        s = _dot_nt(k_ref[r0:r0 + size, :], q_ref[q_rows(tile), head_cols[hh]])
        s_ref[slot, off + r0:off + r0 + size, :] = s
        cm = fold_rows(s, jnp.max)
        return cm if col_max is None else jnp.maximum(col_max, cm)

    def prob_chunk(slot, chunk, col_max):
        _, off, start, size = key_chunks[chunk]
        for r0 in range(off + start, off + start + size, EXP_CHUNK):
            p = jnp.exp2(s_ref[slot, r0:r0 + EXP_CHUNK, :] - col_max)
            p_ref[slot, r0:r0 + EXP_CHUNK, :] = p.astype(BF16)

    def finish(tile, hh, slot):
        acc = _dot(vta_ref[...], p_ref[slot])
        out = (acc[:HEAD_DIM] / acc[HEAD_DIM:HEAD_DIM + 1]).T
        gate = _silu(g_ref[q_rows(tile), head_cols[hh]])
        o_ref[q_rows(tile), head_cols[hh]] = (out * gate).astype(BF16)

    for vt_ref, off, length in zip(vt_refs, offsets, lengths):
        vta_ref[0:HEAD_DIM, off:off + length] = vt_ref[...]
    vta_ref[HEAD_DIM:HEAD_DIM + ONES_ROWS, :] = jnp.ones((ONES_ROWS, total), BF16)

    n_chunks = len(key_chunks)
    first_max = None
    for chunk in range(n_chunks):
        first_max = score_chunk(0, 0, 0, chunk, first_max)
    first_max = jnp.max(first_max, axis=0, keepdims=True)

    def tile_body(tile, cur_max):
        for hh in range(GQA_GROUP):
            slot, next_slot = hh % 2, (hh + 1) % 2
            next_hh = (hh + 1) % GQA_GROUP
            next_tile = tile if next_hh else jnp.minimum(tile + 1, n_tiles - 1)
            next_max = None
            for chunk in range(n_chunks):
                next_max = score_chunk(next_tile, next_hh, next_slot, chunk, next_max)
                prob_chunk(slot, chunk, cur_max)
            finish(tile, hh, slot)
            cur_max = jnp.max(next_max, axis=0, keepdims=True)
        return cur_max

    lax.fori_loop(0, n_tiles, tile_body, first_max)


def _attention(q, g, sources, *, batch, q_len, tq, adaln_next=None):
    q_spec = pl.BlockSpec((q_len, GQA_GROUP * HEAD_DIM), lambda b, h: (b, h))
    in_specs = [q_spec, q_spec]
    args = [q, g]
    for k, vt, length in sources:
        in_specs += [pl.BlockSpec((length, HEAD_DIM), lambda b, h: (b, h)),
                     pl.BlockSpec((None, HEAD_DIM, length), lambda b, h: (b, h, 0))]
        args += [k, vt]
    out_specs = [q_spec]
    out_shape = [jax.ShapeDtypeStruct((batch * q_len, ATTN_W), BF16)]
    if adaln_next is not None:
        cond, ada_w, ada_b, layer = adaln_next
        slab = 3 * D_MODEL // (batch * N_KV_HEADS)
        side_in, side_out, side_shape = _adaln_specs(layer, slab, lambda b, h: b * N_KV_HEADS + h)
        in_specs += side_in
        args += [cond, ada_w, ada_b]
        out_specs = [side_out] + out_specs
        out_shape = [side_shape] + out_shape
    lengths = tuple(length for _, _, length in sources)
    total = sum(lengths)
    outs = pl.pallas_call(
        functools.partial(_attn_kernel, lengths=lengths, tq=tq, with_adaln=adaln_next is not None),
        grid=(batch, N_KV_HEADS),
        in_specs=in_specs,
        out_specs=out_specs,
        out_shape=out_shape,
        scratch_shapes=[pltpu.VMEM((2, total, tq), F32), pltpu.VMEM((2, total, tq), BF16),
                        pltpu.VMEM((HEAD_DIM + ONES_ROWS, total), BF16)],
        compiler_params=_params(2),
        name="attention",
    )(*args)
    return outs[-1], (outs[0] if adaln_next is not None else None)


SHORT_ATTN_SEQS = 2


def _short_attn_kernel(q_ref, g_ref, k_ref, vt_ref, o_ref):
    n_seqs, _, length = vt_ref.shape
    for seq in range(n_seqs):
        rows = slice(seq * length, (seq + 1) * length)
        for kvh in range(N_KV_HEADS):
            kv_cols = slice(kvh * HEAD_DIM, (kvh + 1) * HEAD_DIM)
            heads = [slice((kvh * GQA_GROUP + hh) * HEAD_DIM, (kvh * GQA_GROUP + hh + 1) * HEAD_DIM)
                     for hh in range(GQA_GROUP)]
            q_rows = jnp.concatenate([q_ref[rows, cols] for cols in heads], axis=0)
            s = _dot_nt(k_ref[rows, kv_cols], q_rows)
            p = jnp.exp2(s - jnp.max(s, axis=0, keepdims=True)).astype(BF16)
            vt_aug = jnp.concatenate([vt_ref[seq, kv_cols, :],
                                      jnp.ones((ONES_ROWS, length), BF16)], axis=0)
            acc = _dot(vt_aug, p)
            for hh, cols in enumerate(heads):
                span = slice(hh * length, (hh + 1) * length)
                out = (acc[:HEAD_DIM, span] / acc[HEAD_DIM:HEAD_DIM + 1, span]).T
                o_ref[rows, cols] = (out * _silu(g_ref[rows, cols])).astype(BF16)


def _short_attention(q, g, k, vt, *, batch, length):
    seqs = SHORT_ATTN_SEQS
    row_spec = lambda w: pl.BlockSpec((seqs * length, w), lambda i: (i, 0))
    return pl.pallas_call(
        _short_attn_kernel,
        grid=(batch // seqs,),
        in_specs=[row_spec(ATTN_W), row_spec(ATTN_W), row_spec(KV_W),
                  pl.BlockSpec((seqs, KV_W, length), lambda i: (i, 0, 0))],
        out_specs=row_spec(ATTN_W),
        out_shape=jax.ShapeDtypeStruct((batch * length, ATTN_W), BF16),
        compiler_params=_params(1),
        name="short_attention",
    )(q, g, k, vt)


POOL_CHUNK = 256
OUT_CHUNK = 256


def _pool_steps(up_ref, prev_ref, next_ref, g_ref, pw_ref, ps_ref, dst_ref,
                pad_ref, s2_ref, s4_ref, s8_ref, *, tile_in_seq, tiles_per_seq, seq_len):
    tm = up_ref.shape[0]
    halo = POOL_HALO
    chunk = min(tm, POOL_CHUNK)

    def forward_sum(dst, src, shift, length, lane0):
        for r0 in range(0, length, chunk):
            rows = min(chunk, length - r0)
            dst[r0:r0 + rows, :] = (src[r0:r0 + rows, lane0:]
                                    + src[r0 + shift:r0 + shift + rows, lane0:])

    def window_sums():
        pad_ref[0:halo, :] = jnp.where(tile_in_seq > 0, prev_ref[...], 0.0)
        pad_ref[halo:halo + tm, :] = up_ref[...]
        pad_ref[halo + tm:2 * halo + tm, :] = jnp.where(tile_in_seq < tiles_per_seq - 1,
                                                        next_ref[...], 0.0)
        pad_ref[2 * halo + tm:4 * halo + tm, :] = jnp.zeros((2 * halo, POOL_W), F32)
        forward_sum(s2_ref, pad_ref, 1, tm + 3 * halo, 0)
        forward_sum(s4_ref, s2_ref, 2, tm + 2 * halo, GROUP_C)
        forward_sum(s8_ref, s4_ref, 4, tm + halo, GROUP_C)

    def window_sum(gi, r0):
        if gi == 0:
            return s2_ref[halo - 1 + r0:halo - 1 + r0 + chunk, 0:GROUP_C]
        if gi == 1:
            return s4_ref[halo - 2 + r0:halo - 2 + r0 + chunk, 0:GROUP_C]
        if gi == 2:
            return s8_ref[halo - 4 + r0:halo - 4 + r0 + chunk, 0:GROUP_C]
        return (s8_ref[r0:r0 + chunk, GROUP_C:] + s8_ref[halo + r0:halo + r0 + chunk, GROUP_C:])

    def edge_inverse_count(win, local_row):
        before, after = win // 2, win - win // 2 - 1
        t = tile_in_seq * tm + local_row + lax.broadcasted_iota(jnp.int32, (halo, GROUP_C), 0)
        cnt = jnp.minimum(t + after, seq_len - 1) - jnp.maximum(t - before, 0) + 1
        return 1.0 / cnt.astype(F32)

    def pooled_block(gi, win, r0):
        cols = slice(gi * GROUP_C, (gi + 1) * GROUP_C)
        pieces, inner = [], chunk
        if r0 == 0:
            pieces.append(edge_inverse_count(win, 0))
            inner -= halo
        tail = [edge_inverse_count(win, tm - halo)] if r0 + chunk == tm else []
        inner -= halo * len(tail)
        pieces.append(jnp.full((inner, GROUP_C), 1.0 / win, F32))
        inv_cnt = jnp.concatenate(pieces + tail, axis=0)
        pooled = window_sum(gi, r0) * inv_cnt - pad_ref[halo + r0:halo + r0 + chunk, cols]
        y = _dot(pooled.astype(BF16), pw_ref[gi].astype(BF16)) * ps_ref[:, cols]
        dst_ref[r0:r0 + chunk, cols] = (y * _silu(g_ref[r0:r0 + chunk, cols])).astype(BF16)

    return [window_sums] + [functools.partial(pooled_block, gi, win, r0)
                            for gi, win in enumerate(POOL_WINDOWS)
                            for r0 in range(0, tm, chunk)]


def _pool_scratch(tm):
    return [pltpu.VMEM((tm + 4 * POOL_HALO, POOL_W), F32),
            pltpu.VMEM((tm + 3 * POOL_HALO, POOL_W), F32),
            pltpu.VMEM((tm + 2 * POOL_HALO, POOL_W - GROUP_C), F32),
            pltpu.VMEM((tm + POOL_HALO, POOL_W - 2 * GROUP_C), F32),
            pltpu.VMEM((tm, POOL_W), BF16)]


def _fourier_kernel(u_ref, g_ref, fw_ref, cc_ref, sc_ref, cn_ref, sn_ref, o_ref, ua_ref, ub_ref,
                    *, n, chunk):
    for gi in range(MIX_GROUPS):
        cols = slice(gi * GROUP_C, (gi + 1) * GROUP_C)
        w = fw_ref[gi].astype(BF16)
        a = _dot(cc_ref[...], w).astype(BF16)
        b = _dot(sc_ref[...], w).astype(BF16)
        ug = u_ref[:, cols].astype(BF16)
        ua_ref[:, cols] = _dot(ug, a).astype(BF16)
        ub_ref[:, cols] = _dot(ug, b).astype(BF16)
    for r0 in range(0, n, chunk):
        rows = slice(r0, r0 + chunk)
        y = _dot(cn_ref[rows, :], ua_ref[...]) - _dot(sn_ref[rows, :], ub_ref[...])
        o_ref[rows, :] = (y * _silu(g_ref[rows, :])).astype(BF16)


def _fourier(uf, g, fourier_w, tabs, l, *, batch, n):
    cc, sc, cn, sn = tabs
    seq_spec = lambda col: pl.BlockSpec((n, FOURIER_W), lambda b: (b, col))
    return pl.pallas_call(
        functools.partial(_fourier_kernel, n=n, chunk=min(n, 512)),
        grid=(batch,),
        in_specs=[seq_spec(0), seq_spec((ATTN_W + POOL_W) // FOURIER_W),
                  _layer((MIX_GROUPS, GROUP_C, GROUP_C), l),
                  _resident((GROUP_C, GROUP_C)), _resident((GROUP_C, GROUP_C)),
                  _resident((n, n)), _resident((n, n))],
        out_specs=seq_spec(0),
        out_shape=jax.ShapeDtypeStruct((batch * n, FOURIER_W), BF16),
        scratch_shapes=[pltpu.VMEM((n, FOURIER_W), BF16), pltpu.VMEM((n, FOURIER_W), BF16)],
        compiler_params=_params(1),
        name="fourier",
    )(uf, g, fourier_w, cc, sc, cn, sn)


RADIX = 8
COMBINE_ROWS = 16
SQRT_HALF = math.sqrt(0.5)


def _radix8_real(yr, yn):
    sr = [yr[n] + yr[n + 4] for n in range(4)]
    dr = [yr[n] - yr[n + 4] for n in range(4)]
    sn = {n: yn[n] + yn[n + 4] for n in (1, 3)}
    dn = {n: yn[n] - yn[n + 4] for n in (1, 2, 3)}
    t1r, t1i = (dr[1] - dn[1]) * SQRT_HALF, (dr[1] + dn[1]) * -SQRT_HALF
    t3r, t3i = (dr[3] + dn[3]) * -SQRT_HALF, (dn[3] - dr[3]) * SQRT_HALF
    out = [None] * RADIX
    ea, eb = sr[0] + sr[2], sr[1] + sr[3]
    ec, ed = sr[0] - sr[2], sn[3] - sn[1]
    out[0], out[4] = ea + eb, ea - eb
    out[2], out[6] = ec + ed, ec - ed
    oa, ob = dr[0] - dn[2], t1r + t3r
    oc, od = dr[0] + dn[2], t1i - t3i
    out[1], out[5] = oa + ob, oa - ob
    out[3], out[7] = oc + od, oc - od
    return out


def _fourier_radix_kernel(*refs, n1):
    u_refs = refs[:MIX_GROUPS]
    g_ref, fw_ref, cc_ref, sc_ref, cs_ref, o_ref, ub_ref = refs[MIX_GROUPS:MIX_GROUPS + 7]
    scratch = refs[MIX_GROUPS + 7:]
    pq_refs, yr_refs, yn_refs = (scratch[i * MIX_GROUPS:(i + 1) * MIX_GROUPS] for i in range(3))
    group_lanes = [slice(gi * GROUP_C, (gi + 1) * GROUP_C) for gi in range(MIX_GROUPS)]

    for n2 in range(RADIX):
        for gi, lanes in enumerate(group_lanes):
            ub_ref[n2, :, lanes] = u_refs[gi][pl.ds(n2, n1, stride=RADIX), :].astype(BF16)
        pq = _dot(cs_ref[n2], ub_ref[n2])
        rows = slice(n2 * n1, (n2 + 1) * n1)
        for gi, lanes in enumerate(group_lanes):
            pq_refs[gi][rows, :GROUP_C] = pq[:n1, lanes].astype(BF16)
            pq_refs[gi][rows, GROUP_C:] = pq[n1:, lanes].astype(BF16)

    for gi, lanes in enumerate(group_lanes):
        w = fw_ref[gi].astype(BF16)
        a, b = _dot(cc_ref[...], w), _dot(sc_ref[...], w)
        mix = jnp.concatenate([jnp.concatenate([a, b], axis=1),
                               jnp.concatenate([-b, a], axis=1)], axis=0).astype(BF16)
        y = _dot(pq_refs[gi][...], mix)
        for n2 in range(RADIX):
            yr_refs[gi][n2] = y[n2 * n1:(n2 + 1) * n1, :GROUP_C]
            yn_refs[gi][n2] = y[n2 * n1:(n2 + 1) * n1, GROUP_C:]

        for r0 in range(0, n1, COMBINE_ROWS):
            rows = slice(r0, r0 + COMBINE_ROWS)
            yr = [yr_refs[gi][n2, rows, :] for n2 in range(RADIX)]
            yn = [yn_refs[gi][n2, rows, :] for n2 in range(RADIX)]
            for k2, val in enumerate(_radix8_real(yr, yn)):
                out_rows = slice(k2 * n1 + r0, k2 * n1 + r0 + COMBINE_ROWS)
                o_ref[out_rows, lanes] = (val * _silu(g_ref[out_rows, lanes])).astype(BF16)


def _fourier_radix(uf, g, fourier_w, tabs, l, *, batch, n):
    cc, sc, cs = tabs
    n1 = n // RADIX
    seq_spec = lambda col: pl.BlockSpec((n, FOURIER_W), lambda b: (b, col))
    group_specs = [pl.BlockSpec((n, GROUP_C), lambda b, gi=gi: (b, gi)) for gi in range(MIX_GROUPS)]
    return pl.pallas_call(
        functools.partial(_fourier_radix_kernel, n1=n1),
        grid=(batch,),
        in_specs=group_specs + [
            seq_spec((ATTN_W + POOL_W) // FOURIER_W),
            _layer((MIX_GROUPS, GROUP_C, GROUP_C), l),
            _resident((GROUP_C, GROUP_C)), _resident((GROUP_C, GROUP_C)),
            _resident((RADIX, 2 * n1, n1))],
        out_specs=seq_spec(0),
        out_shape=jax.ShapeDtypeStruct((batch * n, FOURIER_W), BF16),
        scratch_shapes=([pltpu.VMEM((RADIX, n1, FOURIER_W), BF16)]
                        + [pltpu.VMEM((n, 2 * GROUP_C), BF16)] * MIX_GROUPS
                        + [pltpu.VMEM((RADIX, n1, GROUP_C), F32)] * (2 * MIX_GROUPS)),
        compiler_params=_params(1),
        name="fourier_radix",
    )(*([uf] * MIX_GROUPS), g, fourier_w, cc, sc, cs)


def _outproj_kernel(*refs, tiles_per_seq, seq_len, per_batch_mod, final):
    (att_ref, up_ref, prev_ref, next_ref, gp_ref, four_ref, w_ref, pw_ref, ps_ref,
     x_ref, mod_ref) = refs[:11]
    fg_ref = refs[11] if final else None
    o_ref, pad_ref, s2_ref, s4_ref, s8_ref, pooled_ref = refs[-6:]
    step = pl.program_id(0)
    pool_steps = _pool_steps(up_ref, prev_ref, next_ref, gp_ref, pw_ref, ps_ref, pooled_ref,
                             pad_ref, s2_ref, s4_ref, s8_ref, tile_in_seq=step % tiles_per_seq,
                             tiles_per_seq=tiles_per_seq, seq_len=seq_len)
    col_chunks = [slice(c0, c0 + OUT_CHUNK) for c0 in range(0, D_MODEL, OUT_CHUNK)]

    pool_steps.pop(0)()
    for cols in col_chunks:
        o_ref[:, cols] = (_dot(att_ref[...], w_ref[0:ATTN_W, cols])
                          + _dot(four_ref[...], w_ref[ATTN_W + POOL_W:D_MODEL, cols]))
        if pool_steps:
            pool_steps.pop(0)()
    for pool_step in pool_steps:
        pool_step()

    row = step // tiles_per_seq if per_batch_mod else CTX_MOD_ROW
    sum_sq = None
    for cols in col_chunks:
        gate = mod_ref[pl.ds(row, 1), 2 * D_MODEL + cols.start:2 * D_MODEL + cols.stop]
        out = o_ref[:, cols] + _dot(pooled_ref[...], w_ref[ATTN_W:ATTN_W + POOL_W, cols])
        y = x_ref[:, cols] + gate * out
        o_ref[:, cols] = y
        if final:
            part = jnp.sum(y * y, axis=-1, keepdims=True)
            sum_sq = part if sum_sq is None else sum_sq + part
    if final:
        inv_rms = lax.rsqrt(sum_sq * (1.0 / D_MODEL) + EPS)
        o_ref[...] = (o_ref[...] * inv_rms) * fg_ref[...]


def _outproj(att, up, g, four, w_out, pool_w, pool_scale, x2d, mod, final_g, l,
             *, tm, seq_len, per_batch_mod):
    rows = x2d.shape[0]
    final = final_g is not None
    tiles_per_seq = seq_len // tm
    halo_blocks = tm // POOL_HALO
    row_spec = lambda w: pl.BlockSpec((tm, w), lambda i: (i, 0))
    halo_spec = lambda index: pl.BlockSpec((POOL_HALO, POOL_W), lambda i: (index(i), 0))
    in_specs = [row_spec(ATTN_W), row_spec(POOL_W),
                halo_spec(lambda i: jnp.maximum(i * halo_blocks - 1, 0)),
                halo_spec(lambda i: jnp.minimum((i + 1) * halo_blocks, rows // POOL_HALO - 1)),
                pl.BlockSpec((tm, POOL_W), lambda i: (i, ATTN_W // POOL_W)), row_spec(FOURIER_W),
                _layer((D_MODEL, D_MODEL), 0), _layer((MIX_GROUPS, GROUP_C, GROUP_C), l),
                _layer((1, POOL_W), l), row_spec(D_MODEL), _layer((MOD_ROWS, 3 * D_MODEL), 0)]
    args = [att, up, up, up, g, four, w_out, pool_w, pool_scale, x2d, mod]
    if final:
        in_specs.append(_resident((1, D_MODEL)))
        args.append(final_g)
    return pl.pallas_call(
        functools.partial(_outproj_kernel, tiles_per_seq=tiles_per_seq, seq_len=seq_len,
                          per_batch_mod=per_batch_mod, final=final),
        grid=(rows // tm,),
        in_specs=in_specs,
        out_specs=row_spec(D_MODEL),
        out_shape=jax.ShapeDtypeStruct((rows, D_MODEL), F32),
        scratch_shapes=_pool_scratch(tm),
        compiler_params=_params(1),
        name="outproj_final" if final else "outproj",
    )(*args)


def _rope_tables(n):
    grid_h = n // GRID_W
    inv = ROPE_THETA ** (-jnp.arange(0, AXIS_ROT, 2, dtype=F32) / AXIS_ROT)
    ang_r = jnp.arange(grid_h, dtype=F32)[:, None] * inv
    ang_c = jnp.arange(GRID_W, dtype=F32)[:, None] * inv
    by_row = lambda t: jnp.repeat(t, GRID_W, axis=0)
    by_col = lambda t: jnp.tile(t, (grid_h, 1))
    cos_r, sin_r = by_row(jnp.cos(ang_r)), by_row(jnp.sin(ang_r))
    cos_c, sin_c = by_col(jnp.cos(ang_c)), by_col(jnp.sin(ang_c))
    cos = jnp.concatenate([cos_r, cos_r, cos_c, cos_c], axis=1)
    sin = jnp.concatenate([-sin_r, sin_r, -sin_c, sin_c], axis=1)
    return cos, sin


def _dft_cos_sin(n):
    k = jnp.arange(n, dtype=jnp.int32)
    ang = ((k[:, None] * k[None, :]) % n).astype(F32) * (2.0 * math.pi / n)
    return jnp.cos(ang), jnp.sin(ang)


def _fourier_tables(n):
    cc, sc = _dft_cos_sin(GROUP_C)
    cn, sn = _dft_cos_sin(n)
    norm = 1.0 / math.sqrt(n * GROUP_C)
    return (cc * norm).astype(BF16), (sc * norm).astype(BF16), cn.astype(BF16), sn.astype(BF16)


def _fourier_radix_tables(n):
    n1 = n // RADIX
    cc, sc = _dft_cos_sin(GROUP_C)
    norm = 1.0 / math.sqrt(n * GROUP_C)
    c1, s1 = _dft_cos_sin(n1)
    k1 = jnp.arange(n1, dtype=jnp.int32)[None, :]
    n2 = jnp.arange(RADIX, dtype=jnp.int32)[:, None]
    twiddle = (k1 * n2).astype(F32) * (2.0 * math.pi / n)
    ct, st = jnp.cos(twiddle)[:, :, None], jnp.sin(twiddle)[:, :, None]
    cs = jnp.concatenate([c1[None] * ct - s1[None] * st, s1[None] * ct + c1[None] * st], axis=1)
    return (cc * norm).astype(BF16), (sc * norm).astype(BF16), cs.astype(BF16)


def kernel(x, c, ctx, c_ctx, ada_w, ada_b, norm_g, w_in, q_norm_g, k_norm_g, pool_w, pool_scale,
           fourier_w, w_out, final_norm_g):
    B, N, _ = x.shape
    C = ctx.shape[1]
    tm = 512
    tm_ctx = 256

    cond = jnp.concatenate([c, c_ctx[None, :], jnp.zeros((MOD_ROWS - B - 1, D_MODEL), F32)], axis=0)
    ada_b3 = ada_b.reshape(DEPTH, 1, 3 * D_MODEL)
    mod = _adaln(cond, ada_w, ada_b3, 0)

    rope_tabs = _rope_tables(N)
    four_tabs = _fourier_radix_tables(N)
    four_tabs_ctx = _fourier_tables(C)
    w_in_b = w_in[0:1].astype(BF16)
    w_out_b = None
    ng = norm_g[:, None, :]
    qg = q_norm_g[:, None, :]
    kg = k_norm_g[:, None, :]
    ps = pool_scale[:, None, :]

    xl = x.reshape(B * N, D_MODEL)
    xc = ctx.reshape(B * C, D_MODEL)
    for l in range(DEPTH):
        last = l == DEPTH - 1
        if last:
            kc, vct = _ctx_kv(xc, mod, ng, w_in_b, kg, l, seqs_per_tile=2, seq_len=C)
        else:
            (qc, kc, vct, upc, ufc, gc), _ = _inproj(xc, mod, ng, w_in_b, qg, kg, None, l,
                                                     tm=tm_ctx, seq_len=C, per_batch_mod=False)
        casts = [] if w_out_b is not None else [(w_out, l)]
        casts += [] if last else [(w_in, l + 1), (w_out, l + 1)]
        (q, k, vt, up, uf, g), cast_weights = _inproj(
            xl, mod, ng, w_in_b, qg, kg, rope_tabs, l, tm=tm, seq_len=N, per_batch_mod=True,
            casts=casts)
        if w_out_b is None:
            w_out_b, cast_weights = cast_weights[0], cast_weights[1:]

        att, next_mod = _attention(q, g, [(kc, vct, C), (k, vt, N)], batch=B, q_len=N, tq=256,
                                   adaln_next=None if last else (cond, ada_w, ada_b3, l + 1))
        four = _fourier_radix(uf, g, fourier_w, four_tabs, l, batch=B, n=N)
        xl_new = _outproj(att, up, g, four, w_out_b, pool_w, ps, xl, mod,
                          final_norm_g[None, :] if last else None, l,
                          tm=tm, seq_len=N, per_batch_mod=True)

        if not last:
            attc = _short_attention(qc, gc, kc, vct, batch=B, length=C)
            fourc = _fourier(ufc, gc, fourier_w, four_tabs_ctx, l, batch=B, n=C)
            xc = _outproj(attc, upc, gc, fourc, w_out_b, pool_w, ps, xc, mod, None, l,
                          tm=tm_ctx, seq_len=C, per_batch_mod=False)
            w_in_b, w_out_b = cast_weights
            mod = next_mod
        xl = xl_new
    return xl.reshape(B, N, D_MODEL)
```

```python
import functools
import math

import jax
import jax.numpy as jnp
from jax import lax
from jax.experimental import pallas as pl
from jax.experimental.pallas import tpu as pltpu

D_MODEL = 2048
DEPTH = 2
GRID_W = 64
HEAD_DIM = 128
ATTN_W = 1024
N_HEADS = 8
N_KV_HEADS = 2
GQA_GROUP = 4
KV_W = 256
POOL_W = 512
POOL_WINDOWS = (2, 4, 8, 16)
FOURIER_W = 512
MIX_GROUPS = 4
GROUP_C = 128
OFF_K = ATTN_W
OFF_V = OFF_K + KV_W
OFF_POOL = OFF_V + KV_W
OFF_FOURIER = OFF_POOL + POOL_W
OFF_GATE = OFF_FOURIER + FOURIER_W
IN_W = OFF_GATE + D_MODEL
ROPE_THETA = 10000.0
AXIS_ROT = HEAD_DIM // 2
EPS = 1e-6

MOD_ROWS = 16
CTX_MOD_ROW = 8
SUBLANES = 8
ROW_PARTS = 2
NORM_PIECES = 8
POOL_HALO = 8
KEY_CHUNK = 256
EXP_CHUNK = 256
ONES_ROWS = 16
V7X_VMEM_BYTES = 64 * 1024 * 1024
VMEM_LIMIT = V7X_VMEM_BYTES - 8 * 1024 * 1024

BF16 = jnp.bfloat16
F32 = jnp.float32


def _silu(x):
    return x * jax.nn.sigmoid(x)


def _dot(a, b):
    return jnp.dot(a, b, preferred_element_type=F32)


def _dot_nt(a, b):
    return lax.dot_general(a, b, (((1,), (1,)), ((), ())), preferred_element_type=F32)


def _resident(shape):
    zeros = (0,) * len(shape)
    return pl.BlockSpec(shape, lambda *_: zeros, pipeline_mode=pl.Buffered(1))


def _layer(shape, l, tail=None):
    index = (l,) + (tail or (0,) * len(shape))
    return pl.BlockSpec((None,) + tuple(shape), lambda *_: index, pipeline_mode=pl.Buffered(1))


def _params(n_grid):
    return pltpu.CompilerParams(dimension_semantics=("arbitrary",) * n_grid,
                                vmem_limit_bytes=VMEM_LIMIT)


def _adaln_kernel(c_ref, w_ref, b_ref, o_ref):
    a = _silu(c_ref[...]).astype(BF16)
    o_ref[...] = _dot(a, w_ref[...].astype(BF16)) + b_ref[...]


def _adaln_specs(l, tn, step):
    in_specs = [pl.BlockSpec((MOD_ROWS, D_MODEL), lambda *i: (0, 0)),
                pl.BlockSpec((None, D_MODEL, tn), lambda *i: (l, 0, step(*i))),
                pl.BlockSpec((None, 1, tn), lambda *i: (l, 0, step(*i)))]
    out_spec = pl.BlockSpec((None, MOD_ROWS, tn), lambda *i: (0, 0, step(*i)))
    return in_specs, out_spec, jax.ShapeDtypeStruct((1, MOD_ROWS, 3 * D_MODEL), F32)


def _adaln(cond, ada_w, ada_b, l):
    tn = 1024
    in_specs, out_spec, out_shape = _adaln_specs(l, tn, lambda j: j)
    return pl.pallas_call(
        _adaln_kernel,
        grid=(3 * D_MODEL // tn,),
        in_specs=in_specs,
        out_specs=out_spec,
        out_shape=out_shape,
        compiler_params=_params(1),
        name="adaln",
    )(cond, ada_w, ada_b)


def _modulated_norm(x_ref, mod_ref, ng_ref, row, rows=slice(None)):
    x = x_ref[rows, :]
    ms = jnp.mean(x * x, axis=-1, keepdims=True)
    shift = mod_ref[pl.ds(row, 1), 0:D_MODEL]
    scale = mod_ref[pl.ds(row, 1), D_MODEL:2 * D_MODEL]
    y = (x * lax.rsqrt(ms + EPS)) * ng_ref[...]
    return (y * (1.0 + scale) + shift).astype(BF16)


def _head_norm(p, gain):
    ms = jnp.mean(p * p, axis=-1, keepdims=True)
    return (p * lax.rsqrt(ms + EPS)) * gain


def _rope(y, cos, sin, swap_lo):
    partner = jnp.where(swap_lo, pltpu.roll(y, 96, 1), pltpu.roll(y, 32, 1))
    return y * cos + partner * sin


def _inproj_kernel(*refs, rope, tiles_per_batch, n_convert):
    n_in = 8 if rope else 6
    x_ref, mod_ref, ng_ref, w_ref, qg_ref, kg_ref = refs[:6]
    cos_ref, sin_ref = refs[6:8] if rope else (None, None)
    f32_weight_refs = refs[n_in:n_in + n_convert]
    q_ref, k_ref, vt_ref, up_ref, uf_ref, g_ref = refs[n_in + n_convert:n_in + n_convert + 6]
    bf16_weight_refs = refs[n_in + n_convert + 6:n_in + 2 * n_convert + 6]
    h_ref = refs[-1]
    for src_ref, dst_ref in zip(f32_weight_refs, bf16_weight_refs):
        dst_ref[...] = src_ref[...].astype(BF16)
    tm = x_ref.shape[0]
    row = CTX_MOD_ROW if tiles_per_batch is None else pl.program_id(0) // tiles_per_batch
    q_gain = qg_ref[...] * (HEAD_DIM ** -0.5 * math.log2(math.e))
    k_gain = kg_ref[...]
    chunk = GQA_GROUP * HEAD_DIM

    def norm_steps(rows):
        piece_rows = (rows.stop - rows.start) // NORM_PIECES
        pieces = [slice(r0, r0 + piece_rows) for r0 in range(rows.start, rows.stop, piece_rows)]

        def norm(piece):
            h_ref[piece, :] = _modulated_norm(x_ref, mod_ref, ng_ref, row, piece)

        return [functools.partial(norm, piece) for piece in pieces]

    def matmul_steps(rows):
        n_rows = rows.stop - rows.start
        if rope:
            cos, sin = cos_ref[rows, :], sin_ref[rows, :]
            lane = lax.broadcasted_iota(jnp.int32, (n_rows, HEAD_DIM), 1)
            swap_lo = (lane & (AXIS_ROT // 2)) == 0

        def finish_head(p, gain):
            y = _head_norm(p, gain)
            if rope:
                y = _rope(y, cos, sin, swap_lo)
            return y.astype(BF16)

        def project(c0, width):
            return _dot(h_ref[rows, :], w_ref[:, c0:c0 + width])

        def q_chunk(c0):
            p = project(c0, chunk)
            for hh in range(GQA_GROUP):
                lo = hh * HEAD_DIM
                q_ref[rows, c0 + lo:c0 + lo + HEAD_DIM] = finish_head(p[:, lo:lo + HEAD_DIM], q_gain)

        def kv_chunk():
            p = project(OFF_K, 2 * KV_W)
            for hh in range(N_KV_HEADS):
                lo = hh * HEAD_DIM
                k_ref[rows, lo:lo + HEAD_DIM] = finish_head(p[:, lo:lo + HEAD_DIM], k_gain)
            vt_ref[:, rows] = p[:, KV_W:2 * KV_W].T.astype(BF16)

        def up_chunk():
            up_ref[rows, :] = project(OFF_POOL, POOL_W)

        def uf_chunk():
            uf_ref[rows, :] = project(OFF_FOURIER, FOURIER_W)

        def gate_chunk(c0):
            g_ref[rows, c0:c0 + chunk] = project(OFF_GATE + c0, chunk)

        return ([functools.partial(q_chunk, c0) for c0 in range(0, ATTN_W, chunk)]
                + [kv_chunk, up_chunk, uf_chunk]
                + [functools.partial(gate_chunk, c0) for c0 in range(0, D_MODEL, chunk)])

    part_rows = tm // ROW_PARTS
    parts = [slice(r0, r0 + part_rows) for r0 in range(0, tm, part_rows)]
    for step in norm_steps(parts[0]):
        step()
    for index, rows in enumerate(parts):
        fill = norm_steps(parts[index + 1]) if index + 1 < len(parts) else []
        for step in matmul_steps(rows):
            step()
            if fill:
                fill.pop(0)()
        for step in fill:
            step()


def _vt_spec(tm, tiles_per_seq):
    return pl.BlockSpec((None, KV_W, tm), lambda i: (i // tiles_per_seq, 0, i % tiles_per_seq))


def _inproj(x2d, mod, norm_g, w_in, q_g, k_g, rope_tabs, l, *, tm, seq_len, per_batch_mod,
            casts=()):
    rows = x2d.shape[0]
    steps = rows // tm
    rope = rope_tabs is not None
    tiles_per_seq = seq_len // tm
    row_spec = lambda w: pl.BlockSpec((tm, w), lambda i: (i, 0))
    in_specs = [row_spec(D_MODEL), _layer((MOD_ROWS, 3 * D_MODEL), 0), _layer((1, D_MODEL), l),
                _layer((D_MODEL, IN_W), 0), _layer((1, HEAD_DIM), l), _layer((1, HEAD_DIM), l)]
    args = [x2d, mod, norm_g, w_in, q_g, k_g]
    if rope:
        tab_spec = pl.BlockSpec((tm, HEAD_DIM), lambda i: (i % tiles_per_seq, 0))
        in_specs += [tab_spec, tab_spec]
        args += list(rope_tabs)
    row_out = lambda w, dt: jax.ShapeDtypeStruct((rows, w), dt)
    out_specs = [row_spec(ATTN_W), row_spec(KV_W), _vt_spec(tm, tiles_per_seq),
                 row_spec(POOL_W), row_spec(FOURIER_W), row_spec(D_MODEL)]
    out_shape = [row_out(ATTN_W, BF16), row_out(KV_W, BF16),
                 jax.ShapeDtypeStruct((rows // seq_len, KV_W, seq_len), BF16),
                 row_out(POOL_W, F32), row_out(FOURIER_W, F32), row_out(D_MODEL, F32)]
    for param, layer in casts:
        _, p_rows, p_cols = param.shape
        slab = (None, p_rows // steps, p_cols)
        in_specs.append(pl.BlockSpec(slab, lambda i, layer=layer: (layer, i, 0)))
        out_specs.append(pl.BlockSpec(slab, lambda i: (0, i, 0)))
        out_shape.append(jax.ShapeDtypeStruct((1, p_rows, p_cols), BF16))
        args.append(param)
    outs = pl.pallas_call(
        functools.partial(_inproj_kernel, rope=rope, n_convert=len(casts),
                          tiles_per_batch=tiles_per_seq if per_batch_mod else None),
        grid=(steps,),
        in_specs=in_specs,
        out_specs=out_specs,
        out_shape=out_shape,
        scratch_shapes=[pltpu.VMEM((tm, D_MODEL), BF16)],
        compiler_params=_params(1),
        name="inproj_rope" if rope else "inproj_ctx",
    )(*args)
    return outs[:6], outs[6:]


def _ctx_kv_kernel(x_ref, mod_ref, ng_ref, w_ref, kg_ref, k_ref, vt_ref):
    h = _modulated_norm(x_ref, mod_ref, ng_ref, CTX_MOD_ROW)
    p = _dot(h, w_ref[...])
    k_gain = kg_ref[...]
    for hh in range(N_KV_HEADS):
        lo = hh * HEAD_DIM
        k_ref[:, lo:lo + HEAD_DIM] = _head_norm(p[:, lo:lo + HEAD_DIM], k_gain).astype(BF16)
    n_seqs, _, seq_len = vt_ref.shape
    for seq in range(n_seqs):
        vt_ref[seq] = p[seq * seq_len:(seq + 1) * seq_len, KV_W:2 * KV_W].T.astype(BF16)


def _ctx_kv(x2d, mod, norm_g, w_in, k_g, l, *, seqs_per_tile, seq_len):
    rows = x2d.shape[0]
    tm = seqs_per_tile * seq_len
    row_spec = lambda w: pl.BlockSpec((tm, w), lambda i: (i, 0))
    kv_cols = (0, OFF_K // (2 * KV_W))
    return pl.pallas_call(
        _ctx_kv_kernel,
        grid=(rows // tm,),
        in_specs=[row_spec(D_MODEL), _layer((MOD_ROWS, 3 * D_MODEL), 0), _layer((1, D_MODEL), l),
                  _layer((D_MODEL, 2 * KV_W), 0, kv_cols), _layer((1, HEAD_DIM), l)],
        out_specs=[row_spec(KV_W),
                   pl.BlockSpec((seqs_per_tile, KV_W, seq_len), lambda i: (i, 0, 0))],
        out_shape=[jax.ShapeDtypeStruct((rows, KV_W), BF16),
                   jax.ShapeDtypeStruct((rows // seq_len, KV_W, seq_len), BF16)],
        compiler_params=_params(1),
        name="ctx_kv",
    )(x2d, mod, norm_g, w_in, k_g)


def _attn_kernel(*refs, lengths, tq, with_adaln):
    n_sources = len(lengths)
    q_ref, g_ref = refs[0], refs[1]
    k_refs = refs[2:2 + 2 * n_sources:2]
    vt_refs = refs[3:3 + 2 * n_sources:2]
    n_in = 2 + 2 * n_sources
    o_ref, s_ref, p_ref, vta_ref = refs[-4:]
    if with_adaln:
        _adaln_kernel(*refs[n_in:n_in + 4])
    offsets = [sum(lengths[:i]) for i in range(n_sources)]
    total = sum(lengths)
    n_tiles = q_ref.shape[0] // tq
    key_chunks = [(k_ref, off, r0, min(KEY_CHUNK, length))
                  for k_ref, off, length in zip(k_refs, offsets, lengths)
                  for r0 in range(0, length, KEY_CHUNK)]
    head_cols = [slice(hh * HEAD_DIM, (hh + 1) * HEAD_DIM) for hh in range(GQA_GROUP)]

    def q_rows(tile):
        return pl.ds(pl.multiple_of(tile * tq, tq), tq)

    def fold_rows(x, op):
        return op(x.reshape(x.shape[0] // SUBLANES, SUBLANES, tq), axis=0)

    def score_chunk(tile, hh, slot, chunk, col_max):
        k_ref, off, r0, size = key_chunks[chunk]
        s = _dot_nt(k_ref[r0:r0 + size, :], q_ref[q_rows(tile), head_cols[hh]])
        s_ref[slot, off + r0:off + r0 + size, :] = s
        cm = fold_rows(s, jnp.max)
        return cm if col_max is None else jnp.maximum(col_max, cm)

    def prob_chunk(slot, chunk, col_max):
        _, off, start, size = key_chunks[chunk]
        for r0 in range(off + start, off + start + size, EXP_CHUNK):
            p = jnp.exp2(s_ref[slot, r0:r0 + EXP_CHUNK, :] - col_max)
            p_ref[slot, r0:r0 + EXP_CHUNK, :] = p.astype(BF16)

    def finish(tile, hh, slot):
        acc = _dot(vta_ref[...], p_ref[slot])
        out = (acc[:HEAD_DIM] / acc[HEAD_DIM:HEAD_DIM + 1]).T
        gate = _silu(g_ref[q_rows(tile), head_cols[hh]])
        o_ref[q_rows(tile), head_cols[hh]] = (out * gate).astype(BF16)

    for vt_ref, off, length in zip(vt_refs, offsets, lengths):
        vta_ref[0:HEAD_DIM, off:off + length] = vt_ref[...]
    vta_ref[HEAD_DIM:HEAD_DIM + ONES_ROWS, :] = jnp.ones((ONES_ROWS, total), BF16)

    n_chunks = len(key_chunks)
    first_max = None
    for chunk in range(n_chunks):
        first_max = score_chunk(0, 0, 0, chunk, first_max)
    first_max = jnp.max(first_max, axis=0, keepdims=True)

    def tile_body(tile, cur_max):
        for hh in range(GQA_GROUP):
            slot, next_slot = hh % 2, (hh + 1) % 2
            next_hh = (hh + 1) % GQA_GROUP
            next_tile = tile if next_hh else jnp.minimum(tile + 1, n_tiles - 1)
            next_max = None
            for chunk in range(n_chunks):
                next_max = score_chunk(next_tile, next_hh, next_slot, chunk, next_max)
                prob_chunk(slot, chunk, cur_max)
            finish(tile, hh, slot)
            cur_max = jnp.max(next_max, axis=0, keepdims=True)
        return cur_max

    lax.fori_loop(0, n_tiles, tile_body, first_max)


def _attention(q, g, sources, *, batch, q_len, tq, adaln_next=None):
    q_spec = pl.BlockSpec((q_len, GQA_GROUP * HEAD_DIM), lambda b, h: (b, h))
    in_specs = [q_spec, q_spec]
    args = [q, g]
    for k, vt, length in sources:
        in_specs += [pl.BlockSpec((length, HEAD_DIM), lambda b, h: (b, h)),
                     pl.BlockSpec((None, HEAD_DIM, length), lambda b, h: (b, h, 0))]
        args += [k, vt]
    out_specs = [q_spec]
    out_shape = [jax.ShapeDtypeStruct((batch * q_len, ATTN_W), BF16)]
    if adaln_next is not None:
        cond, ada_w, ada_b, layer = adaln_next
        slab = 3 * D_MODEL // (batch * N_KV_HEADS)
        side_in, side_out, side_shape = _adaln_specs(layer, slab, lambda b, h: b * N_KV_HEADS + h)
        in_specs += side_in
        args += [cond, ada_w, ada_b]
        out_specs = [side_out] + out_specs
        out_shape = [side_shape] + out_shape
    lengths = tuple(length for _, _, length in sources)
    total = sum(lengths)
    outs = pl.pallas_call(
        functools.partial(_attn_kernel, lengths=lengths, tq=tq, with_adaln=adaln_next is not None),
        grid=(batch, N_KV_HEADS),
        in_specs=in_specs,
        out_specs=out_specs,
        out_shape=out_shape,
        scratch_shapes=[pltpu.VMEM((2, total, tq), F32), pltpu.VMEM((2, total, tq), BF16),
                        pltpu.VMEM((HEAD_DIM + ONES_ROWS, total), BF16)],
        compiler_params=_params(2),
        name="attention",
    )(*args)
    return outs[-1], (outs[0] if adaln_next is not None else None)


SHORT_ATTN_SEQS = 2


def _short_attn_kernel(q_ref, g_ref, k_ref, vt_ref, o_ref):
    n_seqs, _, length = vt_ref.shape
    for seq in range(n_seqs):
        rows = slice(seq * length, (seq + 1) * length)
        for kvh in range(N_KV_HEADS):
            kv_cols = slice(kvh * HEAD_DIM, (kvh + 1) * HEAD_DIM)
            heads = [slice((kvh * GQA_GROUP + hh) * HEAD_DIM, (kvh * GQA_GROUP + hh + 1) * HEAD_DIM)
                     for hh in range(GQA_GROUP)]
            q_rows = jnp.concatenate([q_ref[rows, cols] for cols in heads], axis=0)
            s = _dot_nt(k_ref[rows, kv_cols], q_rows)
            p = jnp.exp2(s - jnp.max(s, axis=0, keepdims=True)).astype(BF16)
            vt_aug = jnp.concatenate([vt_ref[seq, kv_cols, :],
                                      jnp.ones((ONES_ROWS, length), BF16)], axis=0)
            acc = _dot(vt_aug, p)
            for hh, cols in enumerate(heads):
                span = slice(hh * length, (hh + 1) * length)
                out = (acc[:HEAD_DIM, span] / acc[HEAD_DIM:HEAD_DIM + 1, span]).T
                o_ref[rows, cols] = (out * _silu(g_ref[rows, cols])).astype(BF16)


def _short_attention(q, g, k, vt, *, batch, length):
    seqs = SHORT_ATTN_SEQS
    row_spec = lambda w: pl.BlockSpec((seqs * length, w), lambda i: (i, 0))
    return pl.pallas_call(
        _short_attn_kernel,
        grid=(batch // seqs,),
        in_specs=[row_spec(ATTN_W), row_spec(ATTN_W), row_spec(KV_W),
                  pl.BlockSpec((seqs, KV_W, length), lambda i: (i, 0, 0))],
        out_specs=row_spec(ATTN_W),
        out_shape=jax.ShapeDtypeStruct((batch * length, ATTN_W), BF16),
        compiler_params=_params(1),
        name="short_attention",
    )(q, g, k, vt)


POOL_CHUNK = 256
OUT_CHUNK = 256


def _pool_steps(up_ref, prev_ref, next_ref, g_ref, pw_ref, ps_ref, dst_ref,
                pad_ref, s2_ref, s4_ref, s8_ref, *, tile_in_seq, tiles_per_seq, seq_len):
    tm = up_ref.shape[0]
    halo = POOL_HALO
    chunk = min(tm, POOL_CHUNK)

    def forward_sum(dst, src, shift, length, lane0):
        for r0 in range(0, length, chunk):
            rows = min(chunk, length - r0)
            dst[r0:r0 + rows, :] = (src[r0:r0 + rows, lane0:]
                                    + src[r0 + shift:r0 + shift + rows, lane0:])

    def window_sums():
        pad_ref[0:halo, :] = jnp.where(tile_in_seq > 0, prev_ref[...], 0.0)
        pad_ref[halo:halo + tm, :] = up_ref[...]
        pad_ref[halo + tm:2 * halo + tm, :] = jnp.where(tile_in_seq < tiles_per_seq - 1,
                                                        next_ref[...], 0.0)
        pad_ref[2 * halo + tm:4 * halo + tm, :] = jnp.zeros((2 * halo, POOL_W), F32)
        forward_sum(s2_ref, pad_ref, 1, tm + 3 * halo, 0)
        forward_sum(s4_ref, s2_ref, 2, tm + 2 * halo, GROUP_C)
        forward_sum(s8_ref, s4_ref, 4, tm + halo, GROUP_C)

    def window_sum(gi, r0):
        if gi == 0:
            return s2_ref[halo - 1 + r0:halo - 1 + r0 + chunk, 0:GROUP_C]
        if gi == 1:
            return s4_ref[halo - 2 + r0:halo - 2 + r0 + chunk, 0:GROUP_C]
        if gi == 2:
            return s8_ref[halo - 4 + r0:halo - 4 + r0 + chunk, 0:GROUP_C]
        return (s8_ref[r0:r0 + chunk, GROUP_C:] + s8_ref[halo + r0:halo + r0 + chunk, GROUP_C:])

    def edge_inverse_count(win, local_row):
        before, after = win // 2, win - win // 2 - 1
        t = tile_in_seq * tm + local_row + lax.broadcasted_iota(jnp.int32, (halo, GROUP_C), 0)
        cnt = jnp.minimum(t + after, seq_len - 1) - jnp.maximum(t - before, 0) + 1
        return 1.0 / cnt.astype(F32)

    def pooled_block(gi, win, r0):
        cols = slice(gi * GROUP_C, (gi + 1) * GROUP_C)
        pieces, inner = [], chunk
        if r0 == 0:
            pieces.append(edge_inverse_count(win, 0))
            inner -= halo
        tail = [edge_inverse_count(win, tm - halo)] if r0 + chunk == tm else []
        inner -= halo * len(tail)
        pieces.append(jnp.full((inner, GROUP_C), 1.0 / win, F32))
        inv_cnt = jnp.concatenate(pieces + tail, axis=0)
        pooled = window_sum(gi, r0) * inv_cnt - pad_ref[halo + r0:halo + r0 + chunk, cols]
        y = _dot(pooled.astype(BF16), pw_ref[gi].astype(BF16)) * ps_ref[:, cols]
        dst_ref[r0:r0 + chunk, cols] = (y * _silu(g_ref[r0:r0 + chunk, cols])).astype(BF16)

    return [window_sums] + [functools.partial(pooled_block, gi, win, r0)
                            for gi, win in enumerate(POOL_WINDOWS)
                            for r0 in range(0, tm, chunk)]


def _pool_scratch(tm):
    return [pltpu.VMEM((tm + 4 * POOL_HALO, POOL_W), F32),
            pltpu.VMEM((tm + 3 * POOL_HALO, POOL_W), F32),
            pltpu.VMEM((tm + 2 * POOL_HALO, POOL_W - GROUP_C), F32),
            pltpu.VMEM((tm + POOL_HALO, POOL_W - 2 * GROUP_C), F32),
            pltpu.VMEM((tm, POOL_W), BF16)]


def _fourier_kernel(u_ref, g_ref, fw_ref, cc_ref, sc_ref, cn_ref, sn_ref, o_ref, ua_ref, ub_ref,
                    *, n, chunk):
    for gi in range(MIX_GROUPS):
        cols = slice(gi * GROUP_C, (gi + 1) * GROUP_C)
        w = fw_ref[gi].astype(BF16)
        a = _dot(cc_ref[...], w).astype(BF16)
        b = _dot(sc_ref[...], w).astype(BF16)
        ug = u_ref[:, cols].astype(BF16)
        ua_ref[:, cols] = _dot(ug, a).astype(BF16)
        ub_ref[:, cols] = _dot(ug, b).astype(BF16)
    for r0 in range(0, n, chunk):
        rows = slice(r0, r0 + chunk)
        y = _dot(cn_ref[rows, :], ua_ref[...]) - _dot(sn_ref[rows, :], ub_ref[...])
        o_ref[rows, :] = (y * _silu(g_ref[rows, :])).astype(BF16)


def _fourier(uf, g, fourier_w, tabs, l, *, batch, n):
    cc, sc, cn, sn = tabs
    seq_spec = lambda col: pl.BlockSpec((n, FOURIER_W), lambda b: (b, col))
    return pl.pallas_call(
        functools.partial(_fourier_kernel, n=n, chunk=min(n, 512)),
        grid=(batch,),
        in_specs=[seq_spec(0), seq_spec((ATTN_W + POOL_W) // FOURIER_W),
                  _layer((MIX_GROUPS, GROUP_C, GROUP_C), l),
                  _resident((GROUP_C, GROUP_C)), _resident((GROUP_C, GROUP_C)),
                  _resident((n, n)), _resident((n, n))],
        out_specs=seq_spec(0),
        out_shape=jax.ShapeDtypeStruct((batch * n, FOURIER_W), BF16),
        scratch_shapes=[pltpu.VMEM((n, FOURIER_W), BF16), pltpu.VMEM((n, FOURIER_W), BF16)],
        compiler_params=_params(1),
        name="fourier",
    )(uf, g, fourier_w, cc, sc, cn, sn)


RADIX = 8
COMBINE_ROWS = 16
FOURIER_IN_BUFFERS = 3
SQRT_HALF = math.sqrt(0.5)


def _radix8_real(yr, yn):
    sr = [yr[n] + yr[n + 4] for n in range(4)]
    dr = [yr[n] - yr[n + 4] for n in range(4)]
    sn = {n: yn[n] + yn[n + 4] for n in (1, 3)}
    dn = {n: yn[n] - yn[n + 4] for n in (1, 2, 3)}
    t1r, t1i = (dr[1] - dn[1]) * SQRT_HALF, (dr[1] + dn[1]) * -SQRT_HALF
    t3r, t3i = (dr[3] + dn[3]) * -SQRT_HALF, (dn[3] - dr[3]) * SQRT_HALF
    out = [None] * RADIX
    ea, eb = sr[0] + sr[2], sr[1] + sr[3]
    ec, ed = sr[0] - sr[2], sn[3] - sn[1]
    out[0], out[4] = ea + eb, ea - eb
    out[2], out[6] = ec + ed, ec - ed
    oa, ob = dr[0] - dn[2], t1r + t3r
    oc, od = dr[0] + dn[2], t1i - t3i
    out[1], out[5] = oa + ob, oa - ob
    out[3], out[7] = oc + od, oc - od
    return out


def _fourier_radix_kernel(*refs, n1):
    u_refs = refs[:MIX_GROUPS]
    g_ref, fw_ref, cc_ref, sc_ref, cs_ref, o_ref, ub_ref = refs[MIX_GROUPS:MIX_GROUPS + 7]
    scratch = refs[MIX_GROUPS + 7:]
    pq_refs, yr_refs, yn_refs = (scratch[i * MIX_GROUPS:(i + 1) * MIX_GROUPS] for i in range(3))
    group_lanes = [slice(gi * GROUP_C, (gi + 1) * GROUP_C) for gi in range(MIX_GROUPS)]

    for n2 in range(RADIX):
        for gi, lanes in enumerate(group_lanes):
            ub_ref[n2, :, lanes] = u_refs[gi][pl.ds(n2, n1, stride=RADIX), :].astype(BF16)
        pq = _dot(cs_ref[n2], ub_ref[n2])
        rows = slice(n2 * n1, (n2 + 1) * n1)
        for gi, lanes in enumerate(group_lanes):
            pq_refs[gi][rows, :GROUP_C] = pq[:n1, lanes].astype(BF16)
            pq_refs[gi][rows, GROUP_C:] = pq[n1:, lanes].astype(BF16)

    for gi, lanes in enumerate(group_lanes):
        w = fw_ref[gi].astype(BF16)
        a, b = _dot(cc_ref[...], w), _dot(sc_ref[...], w)
        mix = jnp.concatenate([jnp.concatenate([a, b], axis=1),
                               jnp.concatenate([-b, a], axis=1)], axis=0).astype(BF16)
        y = _dot(pq_refs[gi][...], mix)
        for n2 in range(RADIX):
            yr_refs[gi][n2] = y[n2 * n1:(n2 + 1) * n1, :GROUP_C]
            yn_refs[gi][n2] = y[n2 * n1:(n2 + 1) * n1, GROUP_C:]

        for r0 in range(0, n1, COMBINE_ROWS):
            rows = slice(r0, r0 + COMBINE_ROWS)
            yr = [yr_refs[gi][n2, rows, :] for n2 in range(RADIX)]
            yn = [yn_refs[gi][n2, rows, :] for n2 in range(RADIX)]
            for k2, val in enumerate(_radix8_real(yr, yn)):
                out_rows = slice(k2 * n1 + r0, k2 * n1 + r0 + COMBINE_ROWS)
                o_ref[out_rows, lanes] = (val * _silu(g_ref[out_rows, lanes])).astype(BF16)


def _fourier_radix(uf, g, fourier_w, tabs, l, *, batch, n):
    cc, sc, cs = tabs
    n1 = n // RADIX
    deep = pl.Buffered(FOURIER_IN_BUFFERS)
    group_specs = [pl.BlockSpec((n, GROUP_C), lambda b, gi=gi: (b, gi), pipeline_mode=deep)
                   for gi in range(MIX_GROUPS)]
    gate_spec = pl.BlockSpec((n, FOURIER_W), lambda b: (b, (ATTN_W + POOL_W) // FOURIER_W),
                             pipeline_mode=deep)
    out_spec = pl.BlockSpec((n, FOURIER_W), lambda b: (b, 0))

    def streamed(u_hbm, g_hbm, fw_ref, cc_ref, sc_ref, cs_ref, o_hbm, *scratch):
        def per_sequence(u0, u1, u2, u3, g_ref, o_ref):
            _fourier_radix_kernel(u0, u1, u2, u3, g_ref, fw_ref.at[l], cc_ref, sc_ref, cs_ref,
                                  o_ref, *scratch, n1=n1)

        pltpu.emit_pipeline(per_sequence, grid=(batch,), in_specs=group_specs + [gate_spec],
                            out_specs=[out_spec])(u_hbm, u_hbm, u_hbm, u_hbm, g_hbm, o_hbm)

    in_hbm = pl.BlockSpec(memory_space=pl.ANY)
    in_vmem = pl.BlockSpec(memory_space=pltpu.VMEM)
    return pl.pallas_call(
        streamed,
        in_specs=[in_hbm, in_hbm, in_vmem, in_vmem, in_vmem, in_vmem],
        out_specs=in_hbm,
        out_shape=jax.ShapeDtypeStruct((batch * n, FOURIER_W), BF16),
        scratch_shapes=([pltpu.VMEM((RADIX, n1, FOURIER_W), BF16)]
                        + [pltpu.VMEM((n, 2 * GROUP_C), BF16)] * MIX_GROUPS
                        + [pltpu.VMEM((RADIX, n1, GROUP_C), F32)] * (2 * MIX_GROUPS)),
        compiler_params=pltpu.CompilerParams(vmem_limit_bytes=VMEM_LIMIT),
        name="fourier_radix",
    )(uf, g, fourier_w, cc, sc, cs)


def _outproj_kernel(*refs, tiles_per_seq, seq_len, per_batch_mod, final):
    (att_ref, up_ref, prev_ref, next_ref, gp_ref, four_ref, w_ref, pw_ref, ps_ref,
     x_ref, mod_ref) = refs[:11]
    fg_ref = refs[11] if final else None
    o_ref, pad_ref, s2_ref, s4_ref, s8_ref, pooled_ref = refs[-6:]
    step = pl.program_id(0)
    pool_steps = _pool_steps(up_ref, prev_ref, next_ref, gp_ref, pw_ref, ps_ref, pooled_ref,
                             pad_ref, s2_ref, s4_ref, s8_ref, tile_in_seq=step % tiles_per_seq,
                             tiles_per_seq=tiles_per_seq, seq_len=seq_len)
    col_chunks = [slice(c0, c0 + OUT_CHUNK) for c0 in range(0, D_MODEL, OUT_CHUNK)]

    pool_steps.pop(0)()
    for cols in col_chunks:
        o_ref[:, cols] = (_dot(att_ref[...], w_ref[0:ATTN_W, cols])
                          + _dot(four_ref[...], w_ref[ATTN_W + POOL_W:D_MODEL, cols]))
        if pool_steps:
            pool_steps.pop(0)()
    for pool_step in pool_steps:
        pool_step()

    row = step // tiles_per_seq if per_batch_mod else CTX_MOD_ROW
    sum_sq = None
    for cols in col_chunks:
        gate = mod_ref[pl.ds(row, 1), 2 * D_MODEL + cols.start:2 * D_MODEL + cols.stop]
        out = o_ref[:, cols] + _dot(pooled_ref[...], w_ref[ATTN_W:ATTN_W + POOL_W, cols])
        y = x_ref[:, cols] + gate * out
        o_ref[:, cols] = y
        if final:
            part = jnp.sum(y * y, axis=-1, keepdims=True)
            sum_sq = part if sum_sq is None else sum_sq + part
    if final:
        inv_rms = lax.rsqrt(sum_sq * (1.0 / D_MODEL) + EPS)
        o_ref[...] = (o_ref[...] * inv_rms) * fg_ref[...]


def _outproj(att, up, g, four, w_out, pool_w, pool_scale, x2d, mod, final_g, l,
             *, tm, seq_len, per_batch_mod):
    rows = x2d.shape[0]
    final = final_g is not None
    tiles_per_seq = seq_len // tm
    halo_blocks = tm // POOL_HALO
    row_spec = lambda w: pl.BlockSpec((tm, w), lambda i: (i, 0))
    halo_spec = lambda index: pl.BlockSpec((POOL_HALO, POOL_W), lambda i: (index(i), 0))
    in_specs = [row_spec(ATTN_W), row_spec(POOL_W),
                halo_spec(lambda i: jnp.maximum(i * halo_blocks - 1, 0)),
                halo_spec(lambda i: jnp.minimum((i + 1) * halo_blocks, rows // POOL_HALO - 1)),
                pl.BlockSpec((tm, POOL_W), lambda i: (i, ATTN_W // POOL_W)), row_spec(FOURIER_W),
                _layer((D_MODEL, D_MODEL), 0), _layer((MIX_GROUPS, GROUP_C, GROUP_C), l),
                _layer((1, POOL_W), l), row_spec(D_MODEL), _layer((MOD_ROWS, 3 * D_MODEL), 0)]
    args = [att, up, up, up, g, four, w_out, pool_w, pool_scale, x2d, mod]
    if final:
        in_specs.append(_resident((1, D_MODEL)))
        args.append(final_g)
    return pl.pallas_call(
        functools.partial(_outproj_kernel, tiles_per_seq=tiles_per_seq, seq_len=seq_len,
                          per_batch_mod=per_batch_mod, final=final),
        grid=(rows // tm,),
        in_specs=in_specs,
        out_specs=row_spec(D_MODEL),
        out_shape=jax.ShapeDtypeStruct((rows, D_MODEL), F32),
        scratch_shapes=_pool_scratch(tm),
        compiler_params=_params(1),
        name="outproj_final" if final else "outproj",
    )(*args)


def _rope_tables(n):
    grid_h = n // GRID_W
    inv = ROPE_THETA ** (-jnp.arange(0, AXIS_ROT, 2, dtype=F32) / AXIS_ROT)
    ang_r = jnp.arange(grid_h, dtype=F32)[:, None] * inv
    ang_c = jnp.arange(GRID_W, dtype=F32)[:, None] * inv
    by_row = lambda t: jnp.repeat(t, GRID_W, axis=0)
    by_col = lambda t: jnp.tile(t, (grid_h, 1))
    cos_r, sin_r = by_row(jnp.cos(ang_r)), by_row(jnp.sin(ang_r))
    cos_c, sin_c = by_col(jnp.cos(ang_c)), by_col(jnp.sin(ang_c))
    cos = jnp.concatenate([cos_r, cos_r, cos_c, cos_c], axis=1)
    sin = jnp.concatenate([-sin_r, sin_r, -sin_c, sin_c], axis=1)
    return cos, sin


def _dft_cos_sin(n):
    k = jnp.arange(n, dtype=jnp.int32)
    ang = ((k[:, None] * k[None, :]) % n).astype(F32) * (2.0 * math.pi / n)
    return jnp.cos(ang), jnp.sin(ang)


def _fourier_tables(n):
    cc, sc = _dft_cos_sin(GROUP_C)
    cn, sn = _dft_cos_sin(n)
    norm = 1.0 / math.sqrt(n * GROUP_C)
    return (cc * norm).astype(BF16), (sc * norm).astype(BF16), cn.astype(BF16), sn.astype(BF16)


def _fourier_radix_tables(n):
    n1 = n // RADIX
    cc, sc = _dft_cos_sin(GROUP_C)
    norm = 1.0 / math.sqrt(n * GROUP_C)
    c1, s1 = _dft_cos_sin(n1)
    k1 = jnp.arange(n1, dtype=jnp.int32)[None, :]
    n2 = jnp.arange(RADIX, dtype=jnp.int32)[:, None]
    twiddle = (k1 * n2).astype(F32) * (2.0 * math.pi / n)
    ct, st = jnp.cos(twiddle)[:, :, None], jnp.sin(twiddle)[:, :, None]
    cs = jnp.concatenate([c1[None] * ct - s1[None] * st, s1[None] * ct + c1[None] * st], axis=1)
    return (cc * norm).astype(BF16), (sc * norm).astype(BF16), cs.astype(BF16)


def kernel(x, c, ctx, c_ctx, ada_w, ada_b, norm_g, w_in, q_norm_g, k_norm_g, pool_w, pool_scale,
           fourier_w, w_out, final_norm_g):
    B, N, _ = x.shape
    C = ctx.shape[1]
    tm = 512
    tm_ctx = 256

    cond = jnp.concatenate([c, c_ctx[None, :], jnp.zeros((MOD_ROWS - B - 1, D_MODEL), F32)], axis=0)
    ada_b3 = ada_b.reshape(DEPTH, 1, 3 * D_MODEL)
    mod = _adaln(cond, ada_w, ada_b3, 0)

    rope_tabs = _rope_tables(N)
    four_tabs = _fourier_radix_tables(N)
    four_tabs_ctx = _fourier_tables(C)
    w_in_b = w_in[0:1].astype(BF16)
    w_out_b = None
    ng = norm_g[:, None, :]
    qg = q_norm_g[:, None, :]
    kg = k_norm_g[:, None, :]
    ps = pool_scale[:, None, :]

    xl = x.reshape(B * N, D_MODEL)
    xc = ctx.reshape(B * C, D_MODEL)
    for l in range(DEPTH):
        last = l == DEPTH - 1
        if last:
            kc, vct = _ctx_kv(xc, mod, ng, w_in_b, kg, l, seqs_per_tile=2, seq_len=C)
        else:
            (qc, kc, vct, upc, ufc, gc), _ = _inproj(xc, mod, ng, w_in_b, qg, kg, None, l,
                                                     tm=tm_ctx, seq_len=C, per_batch_mod=False)
        casts = [] if w_out_b is not None else [(w_out, l)]
        casts += [] if last else [(w_in, l + 1), (w_out, l + 1)]
        (q, k, vt, up, uf, g), cast_weights = _inproj(
            xl, mod, ng, w_in_b, qg, kg, rope_tabs, l, tm=tm, seq_len=N, per_batch_mod=True,
            casts=casts)
        if w_out_b is None:
            w_out_b, cast_weights = cast_weights[0], cast_weights[1:]

        att, next_mod = _attention(q, g, [(kc, vct, C), (k, vt, N)], batch=B, q_len=N, tq=256,
                                   adaln_next=None if last else (cond, ada_w, ada_b3, l + 1))
        four = _fourier_radix(uf, g, fourier_w, four_tabs, l, batch=B, n=N)
        xl_new = _outproj(att, up, g, four, w_out_b, pool_w, ps, xl, mod,
                          final_norm_g[None, :] if last else None, l,
                          tm=tm, seq_len=N, per_batch_mod=True)

        if not last:
            attc = _short_attention(qc, gc, kc, vct, batch=B, length=C)
            fourc = _fourier(ufc, gc, fourier_w, four_tabs_ctx, l, batch=B, n=C)
            xc = _outproj(attc, upc, gc, fourc, w_out_b, pool_w, ps, xc, mod, None, l,
                          tm=tm_ctx, seq_len=C, per_batch_mod=False)
            w_in_b, w_out_b = cast_weights
            mod = next_mod
        xl = xl_new
    return xl.reshape(B, N, D_MODEL)
```

```python
import functools
import math

import jax
import jax.numpy as jnp
from jax import lax
from jax.experimental import pallas as pl
from jax.experimental.pallas import tpu as pltpu

D_MODEL = 2048
DEPTH = 2
GRID_W = 64
HEAD_DIM = 128
ATTN_W = 1024
N_HEADS = 8
N_KV_HEADS = 2
GQA_GROUP = 4
KV_W = 256
POOL_W = 512
POOL_WINDOWS = (2, 4, 8, 16)
FOURIER_W = 512
MIX_GROUPS = 4
GROUP_C = 128
OFF_K = ATTN_W
OFF_V = OFF_K + KV_W
OFF_POOL = OFF_V + KV_W
OFF_FOURIER = OFF_POOL + POOL_W
OFF_GATE = OFF_FOURIER + FOURIER_W
IN_W = OFF_GATE + D_MODEL
ROPE_THETA = 10000.0
AXIS_ROT = HEAD_DIM // 2
EPS = 1e-6

MOD_ROWS = 16
CTX_MOD_ROW = 8
SUBLANES = 8
ROW_PARTS = 2
NORM_PIECES = 8
POOL_HALO = 8
KEY_CHUNK = 256
EXP_CHUNK = 256
ONES_ROWS = 16
V7X_VMEM_BYTES = 64 * 1024 * 1024
VMEM_LIMIT = V7X_VMEM_BYTES - 8 * 1024 * 1024

BF16 = jnp.bfloat16
F32 = jnp.float32


def _silu(x):
    return x * jax.nn.sigmoid(x)


def _dot(a, b):
    return jnp.dot(a, b, preferred_element_type=F32)


def _dot_nt(a, b):
    return lax.dot_general(a, b, (((1,), (1,)), ((), ())), preferred_element_type=F32)


def _resident(shape):
    zeros = (0,) * len(shape)
    return pl.BlockSpec(shape, lambda *_: zeros, pipeline_mode=pl.Buffered(1))


def _layer(shape, l, tail=None):
    index = (l,) + (tail or (0,) * len(shape))
    return pl.BlockSpec((None,) + tuple(shape), lambda *_: index, pipeline_mode=pl.Buffered(1))


def _params(n_grid):
    return pltpu.CompilerParams(dimension_semantics=("arbitrary",) * n_grid,
                                vmem_limit_bytes=VMEM_LIMIT)


def _adaln_kernel(c_ref, w_ref, b_ref, o_ref):
    a = _silu(c_ref[...]).astype(BF16)
    o_ref[...] = _dot(a, w_ref[...].astype(BF16)) + b_ref[...]


def _adaln_specs(l, tn, step):
    in_specs = [pl.BlockSpec((MOD_ROWS, D_MODEL), lambda *i: (0, 0)),
                pl.BlockSpec((None, D_MODEL, tn), lambda *i: (l, 0, step(*i))),
                pl.BlockSpec((None, 1, tn), lambda *i: (l, 0, step(*i)))]
    out_spec = pl.BlockSpec((None, MOD_ROWS, tn), lambda *i: (0, 0, step(*i)))
    return in_specs, out_spec, jax.ShapeDtypeStruct((1, MOD_ROWS, 3 * D_MODEL), F32)


def _adaln(cond, ada_w, ada_b, l):
    tn = 1024
    in_specs, out_spec, out_shape = _adaln_specs(l, tn, lambda j: j)
    return pl.pallas_call(
        _adaln_kernel,
        grid=(3 * D_MODEL // tn,),
        in_specs=in_specs,
        out_specs=out_spec,
        out_shape=out_shape,
        compiler_params=_params(1),
        name="adaln",
    )(cond, ada_w, ada_b)


def _modulated_norm(x_ref, mod_ref, ng_ref, row, rows=slice(None)):
    x = x_ref[rows, :]
    ms = jnp.mean(x * x, axis=-1, keepdims=True)
    shift = mod_ref[pl.ds(row, 1), 0:D_MODEL]
    scale = mod_ref[pl.ds(row, 1), D_MODEL:2 * D_MODEL]
    y = (x * lax.rsqrt(ms + EPS)) * ng_ref[...]
    return (y * (1.0 + scale) + shift).astype(BF16)


def _head_norm(p, gain):
    ms = jnp.mean(p * p, axis=-1, keepdims=True)
    return (p * lax.rsqrt(ms + EPS)) * gain


def _rope(y, cos, sin, swap_lo):
    partner = jnp.where(swap_lo, pltpu.roll(y, 96, 1), pltpu.roll(y, 32, 1))
    return y * cos + partner * sin


def _inproj_kernel(*refs, rope, tiles_per_batch, n_convert):
    n_in = 8 if rope else 6
    x_ref, mod_ref, ng_ref, w_ref, qg_ref, kg_ref = refs[:6]
    cos_ref, sin_ref = refs[6:8] if rope else (None, None)
    f32_weight_refs = refs[n_in:n_in + n_convert]
    q_ref, k_ref, vt_ref, up_ref, uf_ref, g_ref = refs[n_in + n_convert:n_in + n_convert + 6]
    bf16_weight_refs = refs[n_in + n_convert + 6:n_in + 2 * n_convert + 6]
    h_ref = refs[-1]
    for src_ref, dst_ref in zip(f32_weight_refs, bf16_weight_refs):
        dst_ref[...] = src_ref[...].astype(BF16)
    tm = x_ref.shape[0]
    row = CTX_MOD_ROW if tiles_per_batch is None else pl.program_id(0) // tiles_per_batch
    q_gain = qg_ref[...] * (HEAD_DIM ** -0.5 * math.log2(math.e))
    k_gain = kg_ref[...]
    chunk = GQA_GROUP * HEAD_DIM

    def norm_steps(rows):
        piece_rows = (rows.stop - rows.start) // NORM_PIECES
        pieces = [slice(r0, r0 + piece_rows) for r0 in range(rows.start, rows.stop, piece_rows)]

        def norm(piece):
            h_ref[piece, :] = _modulated_norm(x_ref, mod_ref, ng_ref, row, piece)

        return [functools.partial(norm, piece) for piece in pieces]

    def matmul_steps(rows):
        n_rows = rows.stop - rows.start
        if rope:
            cos, sin = cos_ref[rows, :], sin_ref[rows, :]
            lane = lax.broadcasted_iota(jnp.int32, (n_rows, HEAD_DIM), 1)
            swap_lo = (lane & (AXIS_ROT // 2)) == 0

        def finish_head(p, gain):
            y = _head_norm(p, gain)
            if rope:
                y = _rope(y, cos, sin, swap_lo)
            return y.astype(BF16)

        def project(c0, width):
            return _dot(h_ref[rows, :], w_ref[:, c0:c0 + width])

        def q_chunk(c0):
            p = project(c0, chunk)
            for hh in range(GQA_GROUP):
                lo = hh * HEAD_DIM
                q_ref[rows, c0 + lo:c0 + lo + HEAD_DIM] = finish_head(p[:, lo:lo + HEAD_DIM], q_gain)

        def kv_chunk():
            p = project(OFF_K, 2 * KV_W)
            for hh in range(N_KV_HEADS):
                lo = hh * HEAD_DIM
                k_ref[rows, lo:lo + HEAD_DIM] = finish_head(p[:, lo:lo + HEAD_DIM], k_gain)
            vt_ref[:, rows] = p[:, KV_W:2 * KV_W].T.astype(BF16)

        def up_chunk():
            up_ref[rows, :] = project(OFF_POOL, POOL_W)

        def uf_chunk():
            uf_ref[rows, :] = project(OFF_FOURIER, FOURIER_W)

        def gate_chunk(c0):
            g_ref[rows, c0:c0 + chunk] = project(OFF_GATE + c0, chunk)

        return ([functools.partial(q_chunk, c0) for c0 in range(0, ATTN_W, chunk)]
                + [kv_chunk, up_chunk, uf_chunk]
                + [functools.partial(gate_chunk, c0) for c0 in range(0, D_MODEL, chunk)])

    part_rows = tm // ROW_PARTS
    parts = [slice(r0, r0 + part_rows) for r0 in range(0, tm, part_rows)]
    for step in norm_steps(parts[0]):
        step()
    for index, rows in enumerate(parts):
        fill = norm_steps(parts[index + 1]) if index + 1 < len(parts) else []
        for step in matmul_steps(rows):
            step()
            if fill:
                fill.pop(0)()
        for step in fill:
            step()


def _vt_spec(tm, tiles_per_seq):
    return pl.BlockSpec((None, KV_W, tm), lambda i: (i // tiles_per_seq, 0, i % tiles_per_seq))


def _inproj(x2d, mod, norm_g, w_in, q_g, k_g, rope_tabs, l, *, tm, seq_len, per_batch_mod,
            casts=()):
    rows = x2d.shape[0]
    steps = rows // tm
    rope = rope_tabs is not None
    tiles_per_seq = seq_len // tm
    row_spec = lambda w: pl.BlockSpec((tm, w), lambda i: (i, 0))
    in_specs = [row_spec(D_MODEL), _layer((MOD_ROWS, 3 * D_MODEL), 0), _layer((1, D_MODEL), l),
                _layer((D_MODEL, IN_W), 0), _layer((1, HEAD_DIM), l), _layer((1, HEAD_DIM), l)]
    args = [x2d, mod, norm_g, w_in, q_g, k_g]
    if rope:
        tab_spec = pl.BlockSpec((tm, HEAD_DIM), lambda i: (i % tiles_per_seq, 0))
        in_specs += [tab_spec, tab_spec]
        args += list(rope_tabs)
    row_out = lambda w, dt: jax.ShapeDtypeStruct((rows, w), dt)
    out_specs = [row_spec(ATTN_W), row_spec(KV_W), _vt_spec(tm, tiles_per_seq),
                 row_spec(POOL_W), row_spec(FOURIER_W), row_spec(D_MODEL)]
    out_shape = [row_out(ATTN_W, BF16), row_out(KV_W, BF16),
                 jax.ShapeDtypeStruct((rows // seq_len, KV_W, seq_len), BF16),
                 row_out(POOL_W, F32), row_out(FOURIER_W, F32), row_out(D_MODEL, F32)]
    for param, layer in casts:
        _, p_rows, p_cols = param.shape
        slab = (None, p_rows // steps, p_cols)
        in_specs.append(pl.BlockSpec(slab, lambda i, layer=layer: (layer, i, 0)))
        out_specs.append(pl.BlockSpec(slab, lambda i: (0, i, 0)))
        out_shape.append(jax.ShapeDtypeStruct((1, p_rows, p_cols), BF16))
        args.append(param)
    outs = pl.pallas_call(
        functools.partial(_inproj_kernel, rope=rope, n_convert=len(casts),
                          tiles_per_batch=tiles_per_seq if per_batch_mod else None),
        grid=(steps,),
        in_specs=in_specs,
        out_specs=out_specs,
        out_shape=out_shape,
        scratch_shapes=[pltpu.VMEM((tm, D_MODEL), BF16)],
        compiler_params=_params(1),
        name="inproj_rope" if rope else "inproj_ctx",
    )(*args)
    return outs[:6], outs[6:]


def _ctx_kv_kernel(x_ref, mod_ref, ng_ref, w_ref, kg_ref, k_ref, vt_ref):
    h = _modulated_norm(x_ref, mod_ref, ng_ref, CTX_MOD_ROW)
    p = _dot(h, w_ref[...])
    k_gain = kg_ref[...]
    for hh in range(N_KV_HEADS):
        lo = hh * HEAD_DIM
        k_ref[:, lo:lo + HEAD_DIM] = _head_norm(p[:, lo:lo + HEAD_DIM], k_gain).astype(BF16)
    n_seqs, _, seq_len = vt_ref.shape
    for seq in range(n_seqs):
        vt_ref[seq] = p[seq * seq_len:(seq + 1) * seq_len, KV_W:2 * KV_W].T.astype(BF16)


def _ctx_kv(x2d, mod, norm_g, w_in, k_g, l, *, seqs_per_tile, seq_len):
    rows = x2d.shape[0]
    tm = seqs_per_tile * seq_len
    row_spec = lambda w: pl.BlockSpec((tm, w), lambda i: (i, 0))
    kv_cols = (0, OFF_K // (2 * KV_W))
    return pl.pallas_call(
        _ctx_kv_kernel,
        grid=(rows // tm,),
        in_specs=[row_spec(D_MODEL), _layer((MOD_ROWS, 3 * D_MODEL), 0), _layer((1, D_MODEL), l),
                  _layer((D_MODEL, 2 * KV_W), 0, kv_cols), _layer((1, HEAD_DIM), l)],
        out_specs=[row_spec(KV_W),
                   pl.BlockSpec((seqs_per_tile, KV_W, seq_len), lambda i: (i, 0, 0))],
        out_shape=[jax.ShapeDtypeStruct((rows, KV_W), BF16),
                   jax.ShapeDtypeStruct((rows // seq_len, KV_W, seq_len), BF16)],
        compiler_params=_params(1),
        name="ctx_kv",
    )(x2d, mod, norm_g, w_in, k_g)


def _attn_kernel(*refs, lengths, tq, with_adaln):
    n_sources = len(lengths)
    q_ref, g_ref = refs[0], refs[1]
    k_refs = refs[2:2 + 2 * n_sources:2]
    vt_refs = refs[3:3 + 2 * n_sources:2]
    n_in = 2 + 2 * n_sources
    o_ref, s_ref, p_ref, vta_ref = refs[-4:]
    if with_adaln:
        _adaln_kernel(*refs[n_in:n_in + 4])
    offsets = [sum(lengths[:i]) for i in range(n_sources)]
    total = sum(lengths)
    n_tiles = q_ref.shape[0] // tq
    key_chunks = [(k_ref, off, r0, min(KEY_CHUNK, length))
                  for k_ref, off, length in zip(k_refs, offsets, lengths)
                  for r0 in range(0, length, KEY_CHUNK)]
    head_cols = [slice(hh * HEAD_DIM, (hh + 1) * HEAD_DIM) for hh in range(GQA_GROUP)]

    def q_rows(tile):
        return pl.ds(pl.multiple_of(tile * tq, tq), tq)

    def fold_rows(x, op):
        return op(x.reshape(x.shape[0] // SUBLANES, SUBLANES, tq), axis=0)

    def score_chunk(tile, hh, slot, chunk, col_max):
        k_ref, off, r0, size = key_chunks[chunk]
        s = _dot_nt(k_ref[r0:r0 + size, :], q_ref[q_rows(tile), head_cols[hh]])
        s_ref[slot, off + r0:off + r0 + size, :] = s
        cm = fold_rows(s, jnp.max)
        return cm if col_max is None else jnp.maximum(col_max, cm)

    def prob_chunk(slot, chunk, col_max):
        _, off, start, size = key_chunks[chunk]
        for r0 in range(off + start, off + start + size, EXP_CHUNK):
            p = jnp.exp2(s_ref[slot, r0:r0 + EXP_CHUNK, :] - col_max)
            p_ref[slot, r0:r0 + EXP_CHUNK, :] = p.astype(BF16)

    def finish(tile, hh, slot):
        acc = _dot(vta_ref[...], p_ref[slot])
        out = (acc[:HEAD_DIM] / acc[HEAD_DIM:HEAD_DIM + 1]).T
        gate = _silu(g_ref[q_rows(tile), head_cols[hh]])
        o_ref[q_rows(tile), head_cols[hh]] = (out * gate).astype(BF16)

    for vt_ref, off, length in zip(vt_refs, offsets, lengths):
        vta_ref[0:HEAD_DIM, off:off + length] = vt_ref[...]
    vta_ref[HEAD_DIM:HEAD_DIM + ONES_ROWS, :] = jnp.ones((ONES_ROWS, total), BF16)

    n_chunks = len(key_chunks)
    first_max = None
    for chunk in range(n_chunks):
        first_max = score_chunk(0, 0, 0, chunk, first_max)
    first_max = jnp.max(first_max, axis=0, keepdims=True)

    def tile_body(tile, cur_max):
        for hh in range(GQA_GROUP):
            slot, next_slot = hh % 2, (hh + 1) % 2
            next_hh = (hh + 1) % GQA_GROUP
            next_tile = tile if next_hh else jnp.minimum(tile + 1, n_tiles - 1)
            next_max = None
            for chunk in range(n_chunks):
                next_max = score_chunk(next_tile, next_hh, next_slot, chunk, next_max)
                prob_chunk(slot, chunk, cur_max)
            finish(tile, hh, slot)
            cur_max = jnp.max(next_max, axis=0, keepdims=True)
        return cur_max

    lax.fori_loop(0, n_tiles, tile_body, first_max)


def _attention(q, g, sources, *, batch, q_len, tq, adaln_next=None):
    q_spec = pl.BlockSpec((q_len, GQA_GROUP * HEAD_DIM), lambda b, h: (b, h))
    in_specs = [q_spec, q_spec]
    args = [q, g]
    for k, vt, length in sources:
        in_specs += [pl.BlockSpec((length, HEAD_DIM), lambda b, h: (b, h)),
                     pl.BlockSpec((None, HEAD_DIM, length), lambda b, h: (b, h, 0))]
        args += [k, vt]
    out_specs = [q_spec]
    out_shape = [jax.ShapeDtypeStruct((batch * q_len, ATTN_W), BF16)]
    if adaln_next is not None:
        cond, ada_w, ada_b, layer = adaln_next
        slab = 3 * D_MODEL // (batch * N_KV_HEADS)
        side_in, side_out, side_shape = _adaln_specs(layer, slab, lambda b, h: b * N_KV_HEADS + h)
        in_specs += side_in
        args += [cond, ada_w, ada_b]
        out_specs = [side_out] + out_specs
        out_shape = [side_shape] + out_shape
    lengths = tuple(length for _, _, length in sources)
    total = sum(lengths)
    outs = pl.pallas_call(
        functools.partial(_attn_kernel, lengths=lengths, tq=tq, with_adaln=adaln_next is not None),
        grid=(batch, N_KV_HEADS),
        in_specs=in_specs,
        out_specs=out_specs,
        out_shape=out_shape,
        scratch_shapes=[pltpu.VMEM((2, total, tq), F32), pltpu.VMEM((2, total, tq), BF16),
                        pltpu.VMEM((HEAD_DIM + ONES_ROWS, total), BF16)],
        compiler_params=_params(2),
        name="attention",
    )(*args)
    return outs[-1], (outs[0] if adaln_next is not None else None)


SHORT_ATTN_SEQS = 2


def _short_attn_kernel(q_ref, g_ref, k_ref, vt_ref, o_ref):
    n_seqs, _, length = vt_ref.shape
    for seq in range(n_seqs):
        rows = slice(seq * length, (seq + 1) * length)
        for kvh in range(N_KV_HEADS):
            kv_cols = slice(kvh * HEAD_DIM, (kvh + 1) * HEAD_DIM)
            heads = [slice((kvh * GQA_GROUP + hh) * HEAD_DIM, (kvh * GQA_GROUP + hh + 1) * HEAD_DIM)
                     for hh in range(GQA_GROUP)]
            q_rows = jnp.concatenate([q_ref[rows, cols] for cols in heads], axis=0)
            s = _dot_nt(k_ref[rows, kv_cols], q_rows)
            p = jnp.exp2(s - jnp.max(s, axis=0, keepdims=True)).astype(BF16)
            vt_aug = jnp.concatenate([vt_ref[seq, kv_cols, :],
                                      jnp.ones((ONES_ROWS, length), BF16)], axis=0)
            acc = _dot(vt_aug, p)
            for hh, cols in enumerate(heads):
                span = slice(hh * length, (hh + 1) * length)
                out = (acc[:HEAD_DIM, span] / acc[HEAD_DIM:HEAD_DIM + 1, span]).T
                o_ref[rows, cols] = (out * _silu(g_ref[rows, cols])).astype(BF16)


def _short_attention(q, g, k, vt, *, batch, length):
    seqs = SHORT_ATTN_SEQS
    row_spec = lambda w: pl.BlockSpec((seqs * length, w), lambda i: (i, 0))
    return pl.pallas_call(
        _short_attn_kernel,
        grid=(batch // seqs,),
        in_specs=[row_spec(ATTN_W), row_spec(ATTN_W), row_spec(KV_W),
                  pl.BlockSpec((seqs, KV_W, length), lambda i: (i, 0, 0))],
        out_specs=row_spec(ATTN_W),
        out_shape=jax.ShapeDtypeStruct((batch * length, ATTN_W), BF16),
        compiler_params=_params(1),
        name="short_attention",
    )(q, g, k, vt)


POOL_CHUNK = 256
OUT_CHUNK = 256


def _pool_steps(up_ref, prev_ref, next_ref, g_ref, pw_ref, ps_ref, dst_ref,
                pad_ref, s2_ref, s4_ref, s8_ref, *, tile_in_seq, tiles_per_seq, seq_len):
    tm = up_ref.shape[0]
    halo = POOL_HALO
    chunk = min(tm, POOL_CHUNK)

    def forward_sum(dst, src, shift, length, lane0):
        for r0 in range(0, length, chunk):
            rows = min(chunk, length - r0)
            dst[r0:r0 + rows, :] = (src[r0:r0 + rows, lane0:]
                                    + src[r0 + shift:r0 + shift + rows, lane0:])

    def window_sums():
        pad_ref[0:halo, :] = jnp.where(tile_in_seq > 0, prev_ref[...], 0.0)
        pad_ref[halo:halo + tm, :] = up_ref[...]
        pad_ref[halo + tm:2 * halo + tm, :] = jnp.where(tile_in_seq < tiles_per_seq - 1,
                                                        next_ref[...], 0.0)
        pad_ref[2 * halo + tm:4 * halo + tm, :] = jnp.zeros((2 * halo, POOL_W), F32)
        forward_sum(s2_ref, pad_ref, 1, tm + 3 * halo, 0)
        forward_sum(s4_ref, s2_ref, 2, tm + 2 * halo, GROUP_C)
        forward_sum(s8_ref, s4_ref, 4, tm + halo, GROUP_C)

    def window_sum(gi, r0):
        if gi == 0:
            return s2_ref[halo - 1 + r0:halo - 1 + r0 + chunk, 0:GROUP_C]
        if gi == 1:
            return s4_ref[halo - 2 + r0:halo - 2 + r0 + chunk, 0:GROUP_C]
        if gi == 2:
            return s8_ref[halo - 4 + r0:halo - 4 + r0 + chunk, 0:GROUP_C]
        return (s8_ref[r0:r0 + chunk, GROUP_C:] + s8_ref[halo + r0:halo + r0 + chunk, GROUP_C:])

    def edge_inverse_count(win, local_row):
        before, after = win // 2, win - win // 2 - 1
        t = tile_in_seq * tm + local_row + lax.broadcasted_iota(jnp.int32, (halo, GROUP_C), 0)
        cnt = jnp.minimum(t + after, seq_len - 1) - jnp.maximum(t - before, 0) + 1
        return 1.0 / cnt.astype(F32)

    def pooled_block(gi, win, r0):
        cols = slice(gi * GROUP_C, (gi + 1) * GROUP_C)
        pieces, inner = [], chunk
        if r0 == 0:
            pieces.append(edge_inverse_count(win, 0))
            inner -= halo
        tail = [edge_inverse_count(win, tm - halo)] if r0 + chunk == tm else []
        inner -= halo * len(tail)
        pieces.append(jnp.full((inner, GROUP_C), 1.0 / win, F32))
        inv_cnt = jnp.concatenate(pieces + tail, axis=0)
        pooled = window_sum(gi, r0) * inv_cnt - pad_ref[halo + r0:halo + r0 + chunk, cols]
        y = _dot(pooled.astype(BF16), pw_ref[gi].astype(BF16)) * ps_ref[:, cols]
        dst_ref[r0:r0 + chunk, cols] = (y * _silu(g_ref[r0:r0 + chunk, cols])).astype(BF16)

    return [window_sums] + [functools.partial(pooled_block, gi, win, r0)
                            for gi, win in enumerate(POOL_WINDOWS)
                            for r0 in range(0, tm, chunk)]


def _pool_scratch(tm):
    return [pltpu.VMEM((tm + 4 * POOL_HALO, POOL_W), F32),
            pltpu.VMEM((tm + 3 * POOL_HALO, POOL_W), F32),
            pltpu.VMEM((tm + 2 * POOL_HALO, POOL_W - GROUP_C), F32),
            pltpu.VMEM((tm + POOL_HALO, POOL_W - 2 * GROUP_C), F32),
            pltpu.VMEM((tm, POOL_W), BF16)]


def _fourier_kernel(u_ref, g_ref, fw_ref, cc_ref, sc_ref, cn_ref, sn_ref, o_ref, ua_ref, ub_ref,
                    *, n, chunk):
    for gi in range(MIX_GROUPS):
        cols = slice(gi * GROUP_C, (gi + 1) * GROUP_C)
        w = fw_ref[gi].astype(BF16)
        a = _dot(cc_ref[...], w).astype(BF16)
        b = _dot(sc_ref[...], w).astype(BF16)
        ug = u_ref[:, cols].astype(BF16)
        ua_ref[:, cols] = _dot(ug, a).astype(BF16)
        ub_ref[:, cols] = _dot(ug, b).astype(BF16)
    for r0 in range(0, n, chunk):
        rows = slice(r0, r0 + chunk)
        y = _dot(cn_ref[rows, :], ua_ref[...]) - _dot(sn_ref[rows, :], ub_ref[...])
        o_ref[rows, :] = (y * _silu(g_ref[rows, :])).astype(BF16)


def _fourier(uf, g, fourier_w, tabs, l, *, batch, n):
    cc, sc, cn, sn = tabs
    seq_spec = lambda col: pl.BlockSpec((n, FOURIER_W), lambda b: (b, col))
    return pl.pallas_call(
        functools.partial(_fourier_kernel, n=n, chunk=min(n, 512)),
        grid=(batch,),
        in_specs=[seq_spec(0), seq_spec((ATTN_W + POOL_W) // FOURIER_W),
                  _layer((MIX_GROUPS, GROUP_C, GROUP_C), l),
                  _resident((GROUP_C, GROUP_C)), _resident((GROUP_C, GROUP_C)),
                  _resident((n, n)), _resident((n, n))],
        out_specs=seq_spec(0),
        out_shape=jax.ShapeDtypeStruct((batch * n, FOURIER_W), BF16),
        scratch_shapes=[pltpu.VMEM((n, FOURIER_W), BF16), pltpu.VMEM((n, FOURIER_W), BF16)],
        compiler_params=_params(1),
        name="fourier",
    )(uf, g, fourier_w, cc, sc, cn, sn)


RADIX = 8
COMBINE_ROWS = 16
SQRT_HALF = math.sqrt(0.5)


def _radix8_real(yr, yn):
    sr = [yr[n] + yr[n + 4] for n in range(4)]
    dr = [yr[n] - yr[n + 4] for n in range(4)]
    sn = {n: yn[n] + yn[n + 4] for n in (1, 3)}
    dn = {n: yn[n] - yn[n + 4] for n in (1, 2, 3)}
    t1r, t1i = (dr[1] - dn[1]) * SQRT_HALF, (dr[1] + dn[1]) * -SQRT_HALF
    t3r, t3i = (dr[3] + dn[3]) * -SQRT_HALF, (dn[3] - dr[3]) * SQRT_HALF
    out = [None] * RADIX
    ea, eb = sr[0] + sr[2], sr[1] + sr[3]
    ec, ed = sr[0] - sr[2], sn[3] - sn[1]
    out[0], out[4] = ea + eb, ea - eb
    out[2], out[6] = ec + ed, ec - ed
    oa, ob = dr[0] - dn[2], t1r + t3r
    oc, od = dr[0] + dn[2], t1i - t3i
    out[1], out[5] = oa + ob, oa - ob
    out[3], out[7] = oc + od, oc - od
    return out


def _fourier_radix_kernel(*refs, n1):
    u_refs = refs[:MIX_GROUPS]
    g_ref, fw_ref, cc_ref, sc_ref, cs_ref, o_ref, ub_ref = refs[MIX_GROUPS:MIX_GROUPS + 7]
    scratch = refs[MIX_GROUPS + 7:]
    pq_refs, yr_refs, yn_refs = (scratch[i * MIX_GROUPS:(i + 1) * MIX_GROUPS] for i in range(3))
    group_lanes = [slice(gi * GROUP_C, (gi + 1) * GROUP_C) for gi in range(MIX_GROUPS)]

    for n2 in range(RADIX):
        for gi, lanes in enumerate(group_lanes):
            ub_ref[n2, :, lanes] = u_refs[gi][pl.ds(n2, n1, stride=RADIX), :].astype(BF16)
        pq = _dot(cs_ref[n2], ub_ref[n2])
        rows = slice(n2 * n1, (n2 + 1) * n1)
        for gi, lanes in enumerate(group_lanes):
            pq_refs[gi][rows, :GROUP_C] = pq[:n1, lanes].astype(BF16)
            pq_refs[gi][rows, GROUP_C:] = pq[n1:, lanes].astype(BF16)

    for gi, lanes in enumerate(group_lanes):
        w = fw_ref[gi].astype(BF16)
        a, b = _dot(cc_ref[...], w), _dot(sc_ref[...], w)
        mix = jnp.concatenate([jnp.concatenate([a, b], axis=1),
                               jnp.concatenate([-b, a], axis=1)], axis=0).astype(BF16)
        y = _dot(pq_refs[gi][...], mix)
        for n2 in range(RADIX):
            yr_refs[gi][n2] = y[n2 * n1:(n2 + 1) * n1, :GROUP_C]
            yn_refs[gi][n2] = y[n2 * n1:(n2 + 1) * n1, GROUP_C:]

        for r0 in range(0, n1, COMBINE_ROWS):
            rows = slice(r0, r0 + COMBINE_ROWS)
            yr = [yr_refs[gi][n2, rows, :] for n2 in range(RADIX)]
            yn = [yn_refs[gi][n2, rows, :] for n2 in range(RADIX)]
            for k2, val in enumerate(_radix8_real(yr, yn)):
                out_rows = slice(k2 * n1 + r0, k2 * n1 + r0 + COMBINE_ROWS)
                o_ref[out_rows, lanes] = (val * _silu(g_ref[out_rows, lanes])).astype(BF16)


def _fourier_radix(uf, g, fourier_w, tabs, l, *, batch, n):
    cc, sc, cs = tabs
    n1 = n // RADIX
    seq_spec = lambda col: pl.BlockSpec((n, FOURIER_W), lambda b: (b, col))
    group_specs = [pl.BlockSpec((n, GROUP_C), lambda b, gi=gi: (b, gi)) for gi in range(MIX_GROUPS)]
    return pl.pallas_call(
        functools.partial(_fourier_radix_kernel, n1=n1),
        grid=(batch,),
        in_specs=group_specs + [
            seq_spec((ATTN_W + POOL_W) // FOURIER_W),
            _layer((MIX_GROUPS, GROUP_C, GROUP_C), l),
            _resident((GROUP_C, GROUP_C)), _resident((GROUP_C, GROUP_C)),
            _resident((RADIX, 2 * n1, n1))],
        out_specs=seq_spec(0),
        out_shape=jax.ShapeDtypeStruct((batch * n, FOURIER_W), BF16),
        scratch_shapes=([pltpu.VMEM((RADIX, n1, FOURIER_W), BF16)]
                        + [pltpu.VMEM((n, 2 * GROUP_C), BF16)] * MIX_GROUPS
                        + [pltpu.VMEM((RADIX, n1, GROUP_C), F32)] * (2 * MIX_GROUPS)),
        compiler_params=_params(1),
        name="fourier_radix",
    )(*([uf] * MIX_GROUPS), g, fourier_w, cc, sc, cs)


def _outproj_kernel(*refs, tiles_per_seq, seq_len, per_batch_mod, final):
    (att_ref, up_ref, prev_ref, next_ref, gp_ref, four_ref, w_ref, pw_ref, ps_ref,
     x_ref, mod_ref) = refs[:11]
    fg_ref = refs[11] if final else None
    o_ref, pad_ref, s2_ref, s4_ref, s8_ref, pooled_ref = refs[-6:]
    step = pl.program_id(0)
    pool_steps = _pool_steps(up_ref, prev_ref, next_ref, gp_ref, pw_ref, ps_ref, pooled_ref,
                             pad_ref, s2_ref, s4_ref, s8_ref, tile_in_seq=step % tiles_per_seq,
                             tiles_per_seq=tiles_per_seq, seq_len=seq_len)
    col_chunks = [slice(c0, c0 + OUT_CHUNK) for c0 in range(0, D_MODEL, OUT_CHUNK)]

    pool_steps.pop(0)()
    for cols in col_chunks:
        o_ref[:, cols] = (_dot(att_ref[...], w_ref[0:ATTN_W, cols])
                          + _dot(four_ref[...], w_ref[ATTN_W + POOL_W:D_MODEL, cols]))
        if pool_steps:
            pool_steps.pop(0)()
    for pool_step in pool_steps:
        pool_step()

    row = step // tiles_per_seq if per_batch_mod else CTX_MOD_ROW
    sum_sq = None
    for cols in col_chunks:
        gate = mod_ref[pl.ds(row, 1), 2 * D_MODEL + cols.start:2 * D_MODEL + cols.stop]
        out = o_ref[:, cols] + _dot(pooled_ref[...], w_ref[ATTN_W:ATTN_W + POOL_W, cols])
        y = x_ref[:, cols] + gate * out
        o_ref[:, cols] = y
        if final:
            part = jnp.sum(y * y, axis=-1, keepdims=True)
            sum_sq = part if sum_sq is None else sum_sq + part
    if final:
        inv_rms = lax.rsqrt(sum_sq * (1.0 / D_MODEL) + EPS)
        for cols in col_chunks:
            o_ref[:, cols] = (o_ref[:, cols] * inv_rms) * fg_ref[:, cols]


def _outproj(att, up, g, four, w_out, pool_w, pool_scale, x2d, mod, final_g, l,
             *, tm, seq_len, per_batch_mod):
    rows = x2d.shape[0]
    final = final_g is not None
    tiles_per_seq = seq_len // tm
    halo_blocks = tm // POOL_HALO
    row_spec = lambda w: pl.BlockSpec((tm, w), lambda i: (i, 0))
    halo_spec = lambda index: pl.BlockSpec((POOL_HALO, POOL_W), lambda i: (index(i), 0))
    in_specs = [row_spec(ATTN_W), row_spec(POOL_W),
                halo_spec(lambda i: jnp.maximum(i * halo_blocks - 1, 0)),
                halo_spec(lambda i: jnp.minimum((i + 1) * halo_blocks, rows // POOL_HALO - 1)),
                pl.BlockSpec((tm, POOL_W), lambda i: (i, ATTN_W // POOL_W)), row_spec(FOURIER_W),
                _layer((D_MODEL, D_MODEL), 0), _layer((MIX_GROUPS, GROUP_C, GROUP_C), l),
                _layer((1, POOL_W), l), row_spec(D_MODEL), _layer((MOD_ROWS, 3 * D_MODEL), 0)]
    args = [att, up, up, up, g, four, w_out, pool_w, pool_scale, x2d, mod]
    if final:
        in_specs.append(_resident((1, D_MODEL)))
        args.append(final_g)
    return pl.pallas_call(
        functools.partial(_outproj_kernel, tiles_per_seq=tiles_per_seq, seq_len=seq_len,
                          per_batch_mod=per_batch_mod, final=final),
        grid=(rows // tm,),
        in_specs=in_specs,
        out_specs=row_spec(D_MODEL),
        out_shape=jax.ShapeDtypeStruct((rows, D_MODEL), F32),
        scratch_shapes=_pool_scratch(tm),
        compiler_params=_params(1),
        name="outproj_final" if final else "outproj",
    )(*args)


def _rope_tables(n):
    grid_h = n // GRID_W
    inv = ROPE_THETA ** (-jnp.arange(0, AXIS_ROT, 2, dtype=F32) / AXIS_ROT)
    ang_r = jnp.arange(grid_h, dtype=F32)[:, None] * inv
    ang_c = jnp.arange(GRID_W, dtype=F32)[:, None] * inv
    by_row = lambda t: jnp.repeat(t, GRID_W, axis=0)
    by_col = lambda t: jnp.tile(t, (grid_h, 1))
    cos_r, sin_r = by_row(jnp.cos(ang_r)), by_row(jnp.sin(ang_r))
    cos_c, sin_c = by_col(jnp.cos(ang_c)), by_col(jnp.sin(ang_c))
    cos = jnp.concatenate([cos_r, cos_r, cos_c, cos_c], axis=1)
    sin = jnp.concatenate([-sin_r, sin_r, -sin_c, sin_c], axis=1)
    return cos, sin


def _dft_cos_sin(n):
    k = jnp.arange(n, dtype=jnp.int32)
    ang = ((k[:, None] * k[None, :]) % n).astype(F32) * (2.0 * math.pi / n)
    return jnp.cos(ang), jnp.sin(ang)


def _fourier_tables(n):
    cc, sc = _dft_cos_sin(GROUP_C)
    cn, sn = _dft_cos_sin(n)
    norm = 1.0 / math.sqrt(n * GROUP_C)
    return (cc * norm).astype(BF16), (sc * norm).astype(BF16), cn.astype(BF16), sn.astype(BF16)


def _fourier_radix_tables(n):
    n1 = n // RADIX
    cc, sc = _dft_cos_sin(GROUP_C)
    norm = 1.0 / math.sqrt(n * GROUP_C)
    c1, s1 = _dft_cos_sin(n1)
    k1 = jnp.arange(n1, dtype=jnp.int32)[None, :]
    n2 = jnp.arange(RADIX, dtype=jnp.int32)[:, None]
    twiddle = (k1 * n2).astype(F32) * (2.0 * math.pi / n)
    ct, st = jnp.cos(twiddle)[:, :, None], jnp.sin(twiddle)[:, :, None]
    cs = jnp.concatenate([c1[None] * ct - s1[None] * st, s1[None] * ct + c1[None] * st], axis=1)
    return (cc * norm).astype(BF16), (sc * norm).astype(BF16), cs.astype(BF16)


def kernel(x, c, ctx, c_ctx, ada_w, ada_b, norm_g, w_in, q_norm_g, k_norm_g, pool_w, pool_scale,
           fourier_w, w_out, final_norm_g):
    B, N, _ = x.shape
    C = ctx.shape[1]
    tm = 512
    tm_ctx = 256

    cond = jnp.concatenate([c, c_ctx[None, :], jnp.zeros((MOD_ROWS - B - 1, D_MODEL), F32)], axis=0)
    ada_b3 = ada_b.reshape(DEPTH, 1, 3 * D_MODEL)
    mod = _adaln(cond, ada_w, ada_b3, 0)

    rope_tabs = _rope_tables(N)
    four_tabs = _fourier_radix_tables(N)
    four_tabs_ctx = _fourier_tables(C)
    w_in_b = w_in[0:1].astype(BF16)
    w_out_b = None
    ng = norm_g[:, None, :]
    qg = q_norm_g[:, None, :]
    kg = k_norm_g[:, None, :]
    ps = pool_scale[:, None, :]

    xl = x.reshape(B * N, D_MODEL)
    xc = ctx.reshape(B * C, D_MODEL)
    for l in range(DEPTH):
        last = l == DEPTH - 1
        if last:
            kc, vct = _ctx_kv(xc, mod, ng, w_in_b, kg, l, seqs_per_tile=2, seq_len=C)
        else:
            (qc, kc, vct, upc, ufc, gc), _ = _inproj(xc, mod, ng, w_in_b, qg, kg, None, l,
                                                     tm=tm_ctx, seq_len=C, per_batch_mod=False)
        casts = [] if w_out_b is not None else [(w_out, l)]
        casts += [] if last else [(w_in, l + 1), (w_out, l + 1)]
        (q, k, vt, up, uf, g), cast_weights = _inproj(
            xl, mod, ng, w_in_b, qg, kg, rope_tabs, l, tm=tm, seq_len=N, per_batch_mod=True,
            casts=casts)
        if w_out_b is None:
            w_out_b, cast_weights = cast_weights[0], cast_weights[1:]

        att, next_mod = _attention(q, g, [(kc, vct, C), (k, vt, N)], batch=B, q_len=N, tq=256,
                                   adaln_next=None if last else (cond, ada_w, ada_b3, l + 1))
        four = _fourier_radix(uf, g, fourier_w, four_tabs, l, batch=B, n=N)
        xl_new = _outproj(att, up, g, four, w_out_b, pool_w, ps, xl, mod,
                          final_norm_g[None, :] if last else None, l,
                          tm=tm, seq_len=N, per_batch_mod=True)

        if not last:
            attc = _short_attention(qc, gc, kc, vct, batch=B, length=C)
            fourc = _fourier(ufc, gc, fourier_w, four_tabs_ctx, l, batch=B, n=C)
            xc = _outproj(attc, upc, gc, fourc, w_out_b, pool_w, ps, xc, mod, None, l,
                          tm=tm_ctx, seq_len=C, per_batch_mod=False)
            w_in_b, w_out_b = cast_weights
            mod = next_mod
        xl = xl_new
    return xl.reshape(B, N, D_MODEL)
```
